```python
import math
import jax, jax.numpy as jnp
from jax import lax
import numpy as np

D_MODEL = 1024
BATCH = 16
SEQ = 2048
DEPTH = 1

CHUNK = 64
N_META = 16
D_FF = ((8 * D_MODEL // 3) + 127) // 128 * 128
RMS_EPS = 1e-6
SSM_WIDTH = D_MODEL // 2
SSM_GROUP = 16
SSM_GROUPS = SSM_WIDTH // SSM_GROUP
SSM_STATE = 64
HEAD_DIM = 64
ATT_WIDTH = D_MODEL // 2
N_HEADS = ATT_WIDTH // HEAD_DIM
KV_DIM = HEAD_DIM
IDX_HEADS = 8
IDX_DIM = 64
TOPK_MAX = 256
Q_BLOCK = 128
REL_BUCKETS = 32
REL_MAX_DIST = 128
N_BRANCH = 2
IN_SPLITS = (SSM_WIDTH, IDX_HEADS * IDX_DIM, IDX_DIM, IDX_HEADS, ATT_WIDTH, KV_DIM, KV_DIM, N_BRANCH * D_MODEL)
IN_WIDTH = sum(IN_SPLITS)

kernel_name = "hybrid_s5_dsa_gated_encoder_layer"


def split_columns(t, sizes):
    out, start = [], 0
    for s in sizes:
        out.append(t[..., start:start + s])
        start += s
    return out


def rms_norm(x, g):
    xf = x.astype(jnp.float32)
    y = xf * lax.rsqrt(jnp.mean(xf * xf, axis=-1, keepdims=True) + RMS_EPS)
    return (y * g.astype(jnp.float32)).astype(x.dtype)


def swiglu(x, w_gate, w_up, w_down):
    return (jax.nn.silu(x @ w_gate) * (x @ w_up)) @ w_down


def chunk_ids(pos):
    return jnp.where(pos < N_META, 0, 1 + (pos - N_META) // CHUNK)


def rel_bucket(rel):
    half = REL_BUCKETS // 2
    max_exact = half // 2
    base = jnp.where(rel > 0, half, 0)
    n = jnp.abs(rel)
    nf = jnp.maximum(n, 1).astype(jnp.float32)
    large = max_exact + (jnp.log(nf / max_exact) / math.log(REL_MAX_DIST / max_exact)
                         * (half - max_exact)).astype(jnp.int32)
    large = jnp.minimum(large, half - 1)
    return base + jnp.where(n < max_exact, n, large)


def s5_mixer(u, lambda_re, lambda_im, log_dt, b_re, b_im, c_re, c_im, d_skip, w_glu):
    f32 = jnp.float32
    bsz, L, _ = u.shape
    uf = u.astype(f32).reshape(bsz, L, SSM_GROUPS, SSM_GROUP)
    lr = lambda_re.astype(f32)
    li = lambda_im.astype(f32)
    dt = jnp.exp(log_dt.astype(f32))[:, None]
    mag = jnp.exp(lr * dt)
    a_re = mag * jnp.cos(li * dt)
    a_im = mag * jnp.sin(li * dt)
    den = lr * lr + li * li
    num_re = a_re - 1.0
    num_im = a_im
    coef_re = (num_re * lr + num_im * li) / den
    coef_im = (num_im * lr - num_re * li) / den
    br = b_re.astype(f32)
    bi = b_im.astype(f32)
    bbar_re = coef_re[..., None] * br - coef_im[..., None] * bi
    bbar_im = coef_re[..., None] * bi + coef_im[..., None] * br
    bu_re = jnp.einsum('blgm,gpm->blgp', uf, bbar_re)
    bu_im = jnp.einsum('blgm,gpm->blgp', uf, bbar_im)
    a_re_t = jnp.broadcast_to(a_re[None, None], (1, L, SSM_GROUPS, SSM_STATE))
    a_im_t = jnp.broadcast_to(a_im[None, None], (1, L, SSM_GROUPS, SSM_STATE))

    def combine(e1, e2):
        a1r, a1i, b1r, b1i = e1
        a2r, a2i, b2r, b2i = e2
        return (a2r * a1r - a2i * a1i,
                a2r * a1i + a2i * a1r,
                a2r * b1r - a2i * b1i + b2r,
                a2r * b1i + a2i * b1r + b2i)

    _, _, x_re, x_im = lax.associative_scan(combine, (a_re_t, a_im_t, bu_re, bu_im), axis=1)
    y = (jnp.einsum('blgp,gmp->blgm', x_re, c_re.astype(f32))
         - jnp.einsum('blgp,gmp->blgm', x_im, c_im.astype(f32))
         + d_skip.astype(f32) * uf)
    y = jax.nn.gelu(y.reshape(bsz, L, SSM_WIDTH)).astype(u.dtype)
    return y * jax.nn.sigmoid(y @ w_glu)


def dsa_mixer(q_idx, k_idx, w_idx, q, k, v, rel_bias, top_k):
    f32 = jnp.float32
    bsz, L, _ = q.shape
    Lp = -(-L // Q_BLOCK) * Q_BLOCK
    nblk = Lp // Q_BLOCK
    pad = lambda t: jnp.pad(t, ((0, 0), (0, Lp - L), (0, 0)))
    q_idx = pad(q_idx).reshape(bsz, Lp, IDX_HEADS, IDX_DIM)
    w_idx = pad(w_idx)
    q = pad(q).reshape(bsz, Lp, N_HEADS, HEAD_DIM)
    kidx_f = pad(k_idx).astype(f32)
    k = pad(k)
    v = pad(v)
    pos = jnp.arange(Lp, dtype=jnp.int32)
    cid = chunk_ids(pos)
    rb = rel_bias.astype(f32)

    def to_blocks(t):
        return jnp.moveaxis(t.reshape(bsz, nblk, Q_BLOCK, *t.shape[2:]), 1, 0)

    def block(args):
        qi, wi, qb, qpos = args
        s = jax.nn.relu(jnp.einsum('bqhd,bsd->bqhs', qi.astype(f32), kidx_f) * IDX_DIM ** -0.5)
        score = jnp.einsum('bqhs,bqh->bqs', s, wi.astype(f32)) * IDX_HEADS ** -0.5
        qcid = chunk_ids(qpos)
        adm = cid[None, :] <= qcid[:, None]
        score = jnp.where(adm[None], score, -1e30)
        _, sel = lax.top_k(score, top_k)
        valid = cid[sel] <= qcid[None, :, None]
        k_sel = jax.vmap(lambda kb, ib: kb[ib])(k, sel)
        v_sel = jax.vmap(lambda vb, ib: vb[ib])(v, sel)
        logits = jnp.einsum('bqhd,bqkd->bqhk', qb.astype(f32), k_sel.astype(f32)) * HEAD_DIM ** -0.5
        bias = rb[rel_bucket(sel - qpos[None, :, None])]
        logits = logits + jnp.moveaxis(bias, -1, 2)
        logits = jnp.where(valid[:, :, None, :], logits, -1e30)
        p = jax.nn.softmax(logits, axis=-1)
        return jnp.einsum('bqhk,bqkd->bqhd', p.astype(v.dtype), v_sel)

    outs = lax.map(block, (to_blocks(q_idx), to_blocks(w_idx), to_blocks(q), pos.reshape(nblk, Q_BLOCK)))
    out = jnp.moveaxis(outs, 0, 1).reshape(bsz, Lp, ATT_WIDTH)
    return out[:, :L]


def setup_inputs(seed: int = 0) -> dict:
    key = jax.random.key(seed)
    keys = jax.random.split(key, 32)
    f32 = jnp.float32

    def normal(i, shape, scale):
        return jax.random.normal(keys[i], shape, f32) * scale

    def gain(i):
        return 1.0 + 0.02 * jax.random.normal(keys[i], (DEPTH, D_MODEL), f32)

    G, P, M = SSM_GROUPS, SSM_STATE, SSM_GROUP
    n = jnp.arange(P, dtype=f32)
    return {
        "x": normal(0, (BATCH, SEQ, D_MODEL), 1.0),
        "meta_tokens": normal(1, (N_META, D_MODEL), 1.0),
        "ff1_norm_pre": gain(2),
        "ff1_norm_post": gain(3),
        "mix_norm_pre": gain(4),
        "mix_norm_post": gain(5),
        "ff2_norm_pre": gain(6),
        "ff2_norm_post": gain(7),
        "ff1_w_gate": normal(8, (DEPTH, D_MODEL, D_FF), D_MODEL ** -0.5),
        "ff1_w_up": normal(9, (DEPTH, D_MODEL, D_FF), D_MODEL ** -0.5),
        "ff1_w_down": normal(10, (DEPTH, D_FF, D_MODEL), D_FF ** -0.5),
        "ff2_w_gate": normal(11, (DEPTH, D_MODEL, D_FF), D_MODEL ** -0.5),
        "ff2_w_up": normal(12, (DEPTH, D_MODEL, D_FF), D_MODEL ** -0.5),
        "ff2_w_down": normal(13, (DEPTH, D_FF, D_MODEL), D_FF ** -0.5),
        "w_in": normal(14, (DEPTH, D_MODEL, IN_WIDTH), D_MODEL ** -0.5),
        "ssm_lambda_re": -0.5 + normal(15, (DEPTH, G, P), 0.01),
        "ssm_lambda_im": math.pi * n + normal(16, (DEPTH, G, P), 0.01),
        "ssm_log_dt": jax.random.uniform(keys[17], (DEPTH, G), f32, math.log(1e-3), math.log(1e-1)),
        "ssm_b_re": normal(18, (DEPTH, G, P, M), (2 * M) ** -0.5),
        "ssm_b_im": normal(19, (DEPTH, G, P, M), (2 * M) ** -0.5),
        "ssm_c_re": normal(20, (DEPTH, G, M, P), P ** -0.5),
        "ssm_c_im": normal(21, (DEPTH, G, M, P), P ** -0.5),
        "ssm_d": normal(22, (DEPTH, G, M), 1.0),
        "ssm_w_glu": normal(23, (DEPTH, SSM_WIDTH, SSM_WIDTH), SSM_WIDTH ** -0.5),
        "w_branch_a": normal(24, (DEPTH, SSM_WIDTH, D_MODEL), SSM_WIDTH ** -0.5),
        "rel_bias": normal(25, (REL_BUCKETS, N_HEADS), 0.2),
        "w_branch_b": normal(26, (DEPTH, ATT_WIDTH, D_MODEL), ATT_WIDTH ** -0.5),
        "w_out": normal(27, (DEPTH, D_MODEL, D_MODEL), D_MODEL ** -0.5),
    }


def reference(x, meta_tokens, ff1_norm_pre, ff1_norm_post, mix_norm_pre, mix_norm_post,
              ff2_norm_pre, ff2_norm_post, ff1_w_gate, ff1_w_up, ff1_w_down,
              ff2_w_gate, ff2_w_up, ff2_w_down, w_in, ssm_lambda_re, ssm_lambda_im, ssm_log_dt,
              ssm_b_re, ssm_b_im, ssm_c_re, ssm_c_im, ssm_d, ssm_w_glu, w_branch_a, rel_bias,
              w_branch_b, w_out):
    bsz = x.shape[0]
    meta = jnp.broadcast_to(meta_tokens[None].astype(x.dtype), (bsz, N_META, D_MODEL))
    h = jnp.concatenate([meta, x], axis=1)
    L = h.shape[1]
    top_k = min(TOPK_MAX, SEQ // 4)
    for l in range(DEPTH):
        hn = rms_norm(h, ff1_norm_pre[l])
        h = h + 0.5 * rms_norm(swiglu(hn, ff1_w_gate[l], ff1_w_up[l], ff1_w_down[l]), ff1_norm_post[l])
        hn = rms_norm(h, mix_norm_pre[l])
        proj = hn @ w_in[l]
        u_a, q_idx, k_idx, w_idx, q, k, v, gates = split_columns(proj, IN_SPLITS)
        y_a = s5_mixer(u_a, ssm_lambda_re[l], ssm_lambda_im[l], ssm_log_dt[l], ssm_b_re[l],
                       ssm_b_im[l], ssm_c_re[l], ssm_c_im[l], ssm_d[l], ssm_w_glu[l])
        y_b = dsa_mixer(q_idx, k_idx, w_idx, q, k, v, rel_bias, top_k)
        g = jax.nn.sigmoid(gates).reshape(bsz, L, N_BRANCH, D_MODEL)
        merged = g[:, :, 0] * (y_a @ w_branch_a[l]) + g[:, :, 1] * (y_b @ w_branch_b[l])
        h = h + rms_norm(merged @ w_out[l], mix_norm_post[l])
        hn = rms_norm(h, ff2_norm_pre[l])
        h = h + 0.5 * rms_norm(swiglu(hn, ff2_w_gate[l], ff2_w_up[l], ff2_w_down[l]), ff2_norm_post[l])
    return h[:, N_META:]
```

```python
import functools
import math

import numpy as np
import jax
import jax.numpy as jnp
from jax import lax
from jax.experimental import pallas as pl
from jax.experimental.pallas import tpu as pltpu

F32 = jnp.float32
BF16 = jnp.bfloat16

RMS_EPS = 1e-6
CHUNK = 64
N_META = 16
SSM_GROUP = 16
SSM_STATE = 64
HEAD_DIM = 64
IDX_DIM = 64
IDX_HEADS = 8
TOPK_MAX = 256
REL_BUCKETS = 32
REL_MAX_DIST = 128
NEG = -1e30
BISECT_STEPS = 40

VMEM_LIMIT = 56 * 1024 * 1024

FFN_TILE = 512
FFN_CHUNK = 256
PROJ_TILE = 512
S5_STEPS = 32
S5_LANES = 512
DSA_TILE = 256
MERGE_TILE = 512


def _cparams(sem):
    return pltpu.CompilerParams(dimension_semantics=sem, vmem_limit_bytes=VMEM_LIMIT)


def _const_spec(shape):
    nd = len(shape)
    return pl.BlockSpec(shape, lambda *_: (0,) * nd)


def _rms(x, g):
    return x * lax.rsqrt(jnp.mean(x * x, axis=-1, keepdims=True) + RMS_EPS) * g


def _dot(a, b):
    return jnp.dot(a, b, preferred_element_type=F32)


def _dot_nt(a, b):
    return lax.dot_general(a, b, (((1,), (1,)), ((), ())), preferred_element_type=F32)


def _ffn_kernel(x_ref, gpre_ref, gpost_ref, wg_ref, wu_ref, wd_ref, o_ref, acc_ref):
    x = x_ref[...]
    hn = _rms(x, gpre_ref[...]).astype(BF16)
    n_chunks = wg_ref.shape[1] // FFN_CHUNK
    for c in range(n_chunks):
        cols = slice(c * FFN_CHUNK, (c + 1) * FFN_CHUNK)
        g = _dot(hn, wg_ref[:, cols])
        u = _dot(hn, wu_ref[:, cols])
        a = (jax.nn.silu(g) * u).astype(BF16)
        part = _dot(a, wd_ref[cols, :])
        if c == 0:
            acc_ref[...] = part
        else:
            acc_ref[...] += part
    o_ref[...] = x + 0.5 * _rms(acc_ref[...], gpost_ref[...])


def _ffn(x, gpre, gpost, wg, wu, wd, tile):
    t, d = x.shape
    dff = wg.shape[1]
    return pl.pallas_call(
        _ffn_kernel,
        out_shape=jax.ShapeDtypeStruct((t, d), F32),
        grid=(t // tile,),
        in_specs=[
            pl.BlockSpec((tile, d), lambda i: (i, 0)),
            _const_spec((1, d)),
            _const_spec((1, d)),
            _const_spec((d, dff)),
            _const_spec((d, dff)),
            _const_spec((dff, d)),
        ],
        out_specs=pl.BlockSpec((tile, d), lambda i: (i, 0)),
        scratch_shapes=[pltpu.VMEM((tile, d), F32)],
        compiler_params=_cparams(("arbitrary",)),
        name="ffn_half_step",
    )(x, gpre, gpost, wg, wu, wd)


def _proj_kernel(h_ref, g_ref, wn_ref, wt_ref, u_ref, kidx_ref, k_ref, qidxT_ref, qT_ref, vT_ref, wT_ref):
    hn = _rms(h_ref[...], g_ref[...]).astype(BF16)
    nn = _dot(hn, wn_ref[...])
    su = u_ref.shape[1]
    u_ref[...] = nn[:, :su].astype(BF16)
    kidx_ref[...] = nn[:, su:su + IDX_DIM].astype(BF16)
    k_ref[...] = nn[:, su + IDX_DIM:su + IDX_DIM + HEAD_DIM].astype(BF16)
    tt = _dot_nt(wt_ref[...], hn)
    nq = qidxT_ref.shape[0]
    na = qT_ref.shape[0]
    qidxT_ref[...] = (tt[:nq] * (IDX_DIM ** -0.5)).astype(BF16)
    qT_ref[...] = (tt[nq:nq + na] * (HEAD_DIM ** -0.5)).astype(BF16)
    vT_ref[...] = tt[nq + na:nq + na + HEAD_DIM].astype(BF16)
    wT_ref[...] = tt[nq + na + HEAD_DIM:nq + na + HEAD_DIM + IDX_HEADS] * (IDX_HEADS ** -0.5)


def _proj(h, gain, w_nn, w_t, nb, seq, tile):
    d = h.shape[1]
    su = w_nn.shape[1] - IDX_DIM - HEAD_DIM
    nq = IDX_HEADS * IDX_DIM
    na = w_t.shape[0] - nq - HEAD_DIM - IDX_HEADS
    nt = seq // tile
    tok = lambda b, i: (b * nt + i, 0)
    tokT = lambda b, i: (0, b * nt + i)
    out_shape = (
        jax.ShapeDtypeStruct((seq, nb * su), BF16),
        jax.ShapeDtypeStruct((nb * seq, IDX_DIM), BF16),
        jax.ShapeDtypeStruct((nb * seq, HEAD_DIM), BF16),
        jax.ShapeDtypeStruct((nq, nb * seq), BF16),
        jax.ShapeDtypeStruct((na, nb * seq), BF16),
        jax.ShapeDtypeStruct((HEAD_DIM, nb * seq), BF16),
        jax.ShapeDtypeStruct((IDX_HEADS, nb * seq), F32),
    )
    out_specs = (
        pl.BlockSpec((tile, su), lambda b, i: (i, b)),
        pl.BlockSpec((tile, IDX_DIM), tok),
        pl.BlockSpec((tile, HEAD_DIM), tok),
        pl.BlockSpec((nq, tile), tokT),
        pl.BlockSpec((na, tile), tokT),
        pl.BlockSpec((HEAD_DIM, tile), tokT),
        pl.BlockSpec((IDX_HEADS, tile), tokT),
    )
    return pl.pallas_call(
        _proj_kernel,
        out_shape=out_shape,
        grid=(nb, nt),
        in_specs=[
            pl.BlockSpec((tile, d), tok),
            _const_spec((1, d)),
            _const_spec(w_nn.shape),
            _const_spec(w_t.shape),
        ],
        out_specs=out_specs,
        compiler_params=_cparams(("arbitrary", "arbitrary")),
        name="mixer_in_proj",
    )(h, gain, w_nn, w_t)


def _s5_scan(bu_ref, xs_ref, st_ref, are_ref, aim_ref, n_steps, nb, bcast_rows):
    n_state = are_ref.shape[1]
    for c in range(n_state // S5_LANES):
        lanes = slice(c * S5_LANES, (c + 1) * S5_LANES)
        lanes_im = slice(n_state + c * S5_LANES, n_state + (c + 1) * S5_LANES)
        a_re = jnp.broadcast_to(are_ref[:, lanes], (nb, S5_LANES))
        a_im = jnp.broadcast_to(aim_ref[:, lanes], (nb, S5_LANES))

        def step(t, carry):
            x_re, x_im = carry
            if bcast_rows:
                rows = pl.ds(t, 1)
            else:
                rows = pl.ds(pl.multiple_of(t * nb, nb), nb)
            b_re = bu_ref[rows, lanes]
            b_im = bu_ref[rows, lanes_im]
            n_re = a_re * x_re - a_im * x_im + b_re
            n_im = a_re * x_im + a_im * x_re + b_im
            if xs_ref is not None:
                xs_ref[rows, lanes] = n_re.astype(BF16)
                xs_ref[rows, lanes_im] = n_im.astype(BF16)
            return n_re, n_im

        x_re, x_im = lax.fori_loop(0, n_steps, step, (st_ref[0, :, lanes], st_ref[1, :, lanes]))
        st_ref[0, :, lanes] = x_re
        st_ref[1, :, lanes] = x_im


def _s5_kernel(u_ref, um_ref, bmat_ref, are_ref, aim_ref, cmat_ref, d_ref, wglu_ref, y_ref,
               bu_ref, xs_ref, st_ref, bum_ref):
    nb = st_ref.shape[1]
    n_state = are_ref.shape[1]
    half_u = u_ref.shape[1] // 2
    half_s = n_state // 2

    def b_project(u, dst_ref):
        for hf in range(2):
            uu = u[:, hf * half_u:(hf + 1) * half_u]
            bu = _dot(uu, bmat_ref[hf])
            dst_ref[:, hf * half_s:(hf + 1) * half_s] = bu[:, :half_s]
            dst_ref[:, n_state + hf * half_s:n_state + (hf + 1) * half_s] = bu[:, half_s:]

    @pl.when(pl.program_id(0) == 0)
    def _():
        st_ref[...] = jnp.zeros_like(st_ref)
        b_project(um_ref[...], bum_ref)
        _s5_scan(bum_ref, None, st_ref, are_ref, aim_ref, um_ref.shape[0], nb, True)

    u = u_ref[...]
    b_project(u, bu_ref)
    _s5_scan(bu_ref, xs_ref, st_ref, are_ref, aim_ref, u_ref.shape[0] // nb, nb, False)

    ys = []
    for hf in range(2):
        y_re = _dot(xs_ref[:, hf * half_s:(hf + 1) * half_s], cmat_ref[hf, 0])
        y_im = _dot(xs_ref[:, n_state + hf * half_s:n_state + (hf + 1) * half_s], cmat_ref[hf, 1])
        ys.append(y_re - y_im)
    y = jnp.concatenate(ys, axis=1) + d_ref[...] * u.astype(F32)
    y = jax.nn.gelu(y)
    y = y * jax.nn.sigmoid(_dot(y.astype(BF16), wglu_ref[...]))
    y_ref[...] = y.astype(BF16)


def _s5(u_tm, u_meta, bmat, a_re, a_im, cmat, d_skip, w_glu, nb):
    rows_total, su = u_tm.shape
    rows = S5_STEPS * nb
    n_state = a_re.shape[1]
    return pl.pallas_call(
        _s5_kernel,
        out_shape=jax.ShapeDtypeStruct((rows_total, su), BF16),
        grid=(rows_total // rows,),
        in_specs=[
            pl.BlockSpec((rows, su), lambda i: (i, 0)),
            _const_spec(u_meta.shape),
            _const_spec(bmat.shape),
            _const_spec(a_re.shape),
            _const_spec(a_im.shape),
            _const_spec(cmat.shape),
            _const_spec(d_skip.shape),
            _const_spec(w_glu.shape),
        ],
        out_specs=pl.BlockSpec((rows, su), lambda i: (i, 0)),
        scratch_shapes=[
            pltpu.VMEM((rows, 2 * n_state), F32),
            pltpu.VMEM((rows, 2 * n_state), BF16),
            pltpu.VMEM((2, nb, n_state), F32),
            pltpu.VMEM((u_meta.shape[0], 2 * n_state), F32),
        ],
        compiler_params=_cparams(("arbitrary",)),
        name="s5_mixer",
    )(u_tm, u_meta, bmat, a_re, a_im, cmat, d_skip, w_glu)


def _s5_params(lambda_re, lambda_im, log_dt, b_re, b_im, c_re, c_im, d_skip):
    g, p, m = b_re.shape
    lr = lambda_re.astype(F32)
    li = lambda_im.astype(F32)
    dt = jnp.exp(log_dt.astype(F32))[:, None]
    mag = jnp.exp(lr * dt)
    a_re = mag * jnp.cos(li * dt)
    a_im = mag * jnp.sin(li * dt)
    den = lr * lr + li * li
    num_re = a_re - 1.0
    num_im = a_im
    coef_re = (num_re * lr + num_im * li) / den
    coef_im = (num_im * lr - num_re * li) / den
    br = b_re.astype(F32)
    bi = b_im.astype(F32)
    bbar_re = coef_re[..., None] * br - coef_im[..., None] * bi
    bbar_im = coef_re[..., None] * bi + coef_im[..., None] * br
    gh = g // 2
    eye = jnp.eye(gh, dtype=F32)

    def in_mat(bb):
        bb = bb.reshape(2, gh, p, m)
        return jnp.einsum('hgpm,gk->hgmkp', bb, eye).reshape(2, gh * m, gh * p)

    def out_mat(cc):
        cc = cc.astype(F32).reshape(2, gh, m, p)
        return jnp.einsum('hgmp,gk->hgpkm', cc, eye).reshape(2, gh * p, gh * m)

    bmat = jnp.concatenate([in_mat(bbar_re), in_mat(bbar_im)], axis=2).astype(BF16)
    cmat = jnp.stack([out_mat(c_re), out_mat(c_im)], axis=1).astype(BF16)
    return (bmat, a_re.reshape(1, g * p), a_im.reshape(1, g * p), cmat,
            d_skip.astype(F32).reshape(1, g * m))


def _dsa_kernel(top_k, qidxT_ref, qT_ref, wT_ref, kidx_ref, k_ref, vT_ref, kidxm_ref, km_ref, vTm_ref,
                bias_ref, biasm_ref, tri_ref, trim_ref, y_ref,
                key_ref, keym_ref, msk_ref, mskm_ref, lg_ref, lgm_ref, outT_ref):
    i = pl.program_id(1)
    tq = qT_ref.shape[1]
    tk = tq
    nkb = i + 1

    def kblock(kb):
        return pl.ds(pl.multiple_of(kb * tk, tk), tk)

    kc = lax.broadcasted_iota(jnp.int32, (tk, tq), 0) // CHUNK
    qc = lax.broadcasted_iota(jnp.int32, (tk, tq), 1) // CHUNK
    adm_diag = kc <= qc

    def idx_score(kidx):
        s = jnp.zeros((kidx.shape[0], tq), F32)
        for h in range(IDX_HEADS):
            sh = _dot(kidx, qidxT_ref[h * IDX_DIM:(h + 1) * IDX_DIM, :])
            s = s + jnp.maximum(sh, 0.0) * wT_ref[h:h + 1, :]
        return s

    sm = idx_score(kidxm_ref[...])
    keym_ref[...] = sm

    def score_body(kb, carry):
        mn, mx = carry
        s = idx_score(kidx_ref[kblock(kb), :])
        adm = jnp.logical_or(kb < i, adm_diag)
        key_ref[kblock(kb), :] = jnp.where(adm, s, NEG)
        mn = jnp.minimum(mn, jnp.min(jnp.where(adm, s, -NEG), axis=0, keepdims=True))
        mx = jnp.maximum(mx, jnp.max(jnp.where(adm, s, NEG), axis=0, keepdims=True))
        return mn, mx

    mn, mx = lax.fori_loop(0, nkb, score_body, (jnp.min(sm, axis=0, keepdims=True),
                                                jnp.max(sm, axis=0, keepdims=True)))

    def count(pred_fn):
        c = jnp.sum(pred_fn(keym_ref[...]).astype(F32), axis=0, keepdims=True)

        def body(kb, c):
            return c + jnp.sum(pred_fn(key_ref[kblock(kb), :]).astype(F32), axis=0, keepdims=True)

        return lax.fori_loop(0, nkb, body, c)

    def bisect_body(_, carry):
        lo, hi = carry
        mid = 0.5 * lo + 0.5 * hi
        up = count(lambda s: s > mid) >= float(top_k)
        return jnp.where(up, mid, lo), jnp.where(up, hi, mid)

    lo, _ = lax.fori_loop(0, BISECT_STEPS, bisect_body, (mn - (1.0 + jnp.abs(mn)), mx))

    def above_lo_min(s):
        return jnp.min(jnp.where(s > lo, s, -NEG), axis=0, keepdims=True)

    thr = lax.fori_loop(0, nkb, lambda kb, t: jnp.minimum(t, above_lo_min(key_ref[kblock(kb), :])),
                        above_lo_min(keym_ref[...]))
    need = float(top_k) - count(lambda s: s > thr)

    def select(kk, tri, carry, adm):
        eq = kk == thr
        rank = carry + _dot(tri, eq.astype(BF16))
        sel = jnp.logical_or(kk > thr, jnp.logical_and(eq, rank < need))
        if adm is not None:
            sel = jnp.logical_and(sel, adm)
        return jnp.where(sel, 0.0, NEG), carry + jnp.sum(eq.astype(F32), axis=0, keepdims=True)

    mskm, carry0 = select(keym_ref[...], trim_ref[...], jnp.zeros((1, tq), F32), None)
    mskm_ref[...] = mskm

    def sel_body(kb, carry):
        adm = jnp.logical_or(kb < i, adm_diag)
        m, carry = select(key_ref[kblock(kb), :], tri_ref[...], carry, adm)
        msk_ref[kblock(kb), :] = m
        return carry

    lax.fori_loop(0, nkb, sel_body, carry0)

    def head_body(h, _):
        qh = qT_ref[pl.ds(pl.multiple_of(h * HEAD_DIM, HEAD_DIM), HEAD_DIM), :]
        lm = _dot(km_ref[...], qh) + biasm_ref[h] + mskm_ref[...]
        lgm_ref[...] = lm
        mx = jnp.max(lm, axis=0, keepdims=True)

        def logit_body(kb, mx):
            which = jnp.clip(kb - i + 2, 0, 2)
            lg = _dot(k_ref[kblock(kb), :], qh) + bias_ref[h, which] + msk_ref[kblock(kb), :]
            lg_ref[kblock(kb), :] = lg
            return jnp.maximum(mx, jnp.max(lg, axis=0, keepdims=True))

        mx = lax.fori_loop(0, nkb, logit_body, mx)

        pm = jnp.exp(lgm_ref[...] - mx)
        den = jnp.sum(pm, axis=0, keepdims=True)
        acc = _dot(vTm_ref[...], pm.astype(BF16))

        def pv_body(kb, carry):
            den, acc = carry
            p = jnp.exp(lg_ref[kblock(kb), :] - mx)
            den = den + jnp.sum(p, axis=0, keepdims=True)
            acc = acc + _dot(vT_ref[:, kblock(kb)], p.astype(BF16))
            return den, acc

        den, acc = lax.fori_loop(0, nkb, pv_body, (den, acc))
        outT_ref[pl.ds(pl.multiple_of(h * HEAD_DIM, HEAD_DIM), HEAD_DIM), :] = acc / den
        return 0

    lax.fori_loop(0, qT_ref.shape[0] // HEAD_DIM, head_body, 0)
    y_ref[...] = outT_ref[...].T.astype(BF16)


def _rel_bucket_np(rel):
    half = REL_BUCKETS // 2
    max_exact = half // 2
    base = np.where(rel > 0, half, 0)
    n = np.abs(rel)
    nf = np.maximum(n, 1).astype(np.float64)
    large = max_exact + (np.log(nf / max_exact) / math.log(REL_MAX_DIST / max_exact)
                         * (half - max_exact)).astype(np.int32)
    large = np.minimum(large, half - 1)
    return base + np.where(n < max_exact, n, large)


def _bias_tables(rel_bias, tq, seq):
    rb = rel_bias.astype(F32)
    kf = np.arange(tq)[:, None]
    qf = np.arange(tq)[None, :]
    far_bucket = _rel_bucket_np(np.array(-(tq + 1)))
    assert int(far_bucket) == int(_rel_bucket_np(np.array(-(seq + N_META))))
    assert np.all(_rel_bucket_np(np.arange(-(seq + N_META), -tq)) == far_bucket)
    buckets = np.stack([
        np.full((tq, tq), far_bucket),
        _rel_bucket_np(kf - tq - qf),
        _rel_bucket_np(kf - qf),
    ])
    bias = jnp.transpose(rb[buckets], (3, 0, 1, 2))
    meta_b = _rel_bucket_np(np.arange(N_META)[:, None] - (N_META + np.arange(seq))[None, :])
    bias_m = jnp.transpose(rb[meta_b], (2, 0, 1))
    return bias, bias_m


def _dsa(qidxT, qT, wT, kidx, k, vT, kidx_m, k_m, vT_m, bias, bias_m, nb, seq):
    tq = DSA_TILE
    nq = seq // tq
    n_att = qT.shape[0]
    tri = jnp.asarray(np.tril(np.ones((tq, tq), np.float32), -1), BF16)
    tri_m = jnp.asarray(np.tril(np.ones((N_META, N_META), np.float32), -1), BF16)
    qcol = lambda b, i: (0, b * nq + i)
    return pl.pallas_call(
        functools.partial(_dsa_kernel, min(TOPK_MAX, seq // 4)),
        out_shape=jax.ShapeDtypeStruct((nb * seq, n_att), BF16),
        grid=(nb, nq),
        in_specs=[
            pl.BlockSpec((qidxT.shape[0], tq), qcol),
            pl.BlockSpec((n_att, tq), qcol),
            pl.BlockSpec((IDX_HEADS, tq), qcol),
            pl.BlockSpec((seq, IDX_DIM), lambda b, i: (b, 0)),
            pl.BlockSpec((seq, HEAD_DIM), lambda b, i: (b, 0)),
            pl.BlockSpec((HEAD_DIM, seq), lambda b, i: (0, b)),
            _const_spec(kidx_m.shape),
            _const_spec(k_m.shape),
            _const_spec(vT_m.shape),
            _const_spec(bias.shape),
            pl.BlockSpec((bias_m.shape[0], N_META, tq), lambda b, i: (0, 0, i)),
            _const_spec(tri.shape),
            _const_spec(tri_m.shape),
        ],
        out_specs=pl.BlockSpec((tq, n_att), lambda b, i: (b * nq + i, 0)),
        scratch_shapes=[
            pltpu.VMEM((seq, tq), F32),
            pltpu.VMEM((N_META, tq), F32),
            pltpu.VMEM((seq, tq), F32),
            pltpu.VMEM((N_META, tq), F32),
            pltpu.VMEM((seq, tq), F32),
            pltpu.VMEM((N_META, tq), F32),
            pltpu.VMEM((n_att, tq), F32),
        ],
        compiler_params=_cparams(("arbitrary", "arbitrary")),
        name="dsa_mixer",
    )(qidxT, qT, wT, kidx, k, vT, kidx_m, k_m, vT_m, bias, bias_m, tri, tri_m)


def _merge_kernel(h_ref, ya_ref, yb_ref, gpre_ref, gpost_ref, wg_ref, wa_ref, wb_ref, wo_ref, o_ref):
    h = h_ref[...]
    d = h.shape[1]
    hn = _rms(h, gpre_ref[...]).astype(BF16)
    pa = _dot(ya_ref[...], wa_ref[...])
    pb = _dot(yb_ref[...], wb_ref[...])
    g0 = jax.nn.sigmoid(_dot(hn, wg_ref[:, :d]))
    g1 = jax.nn.sigmoid(_dot(hn, wg_ref[:, d:]))
    merged = (g0 * pa + g1 * pb).astype(BF16)
    o_ref[...] = h + _rms(_dot(merged, wo_ref[...]), gpost_ref[...])


def _merge(h, ya_tm, yb, gpre, gpost, wg, wa, wb, wo, nb, seq, tile):
    d = h.shape[1]
    sa = wa.shape[0]
    sb = wb.shape[0]
    nt = seq // tile
    tok = lambda b, i: (b * nt + i, 0)
    return pl.pallas_call(
        _merge_kernel,
        out_shape=jax.ShapeDtypeStruct(h.shape, F32),
        grid=(nb, nt),
        in_specs=[
            pl.BlockSpec((tile, d), tok),
            pl.BlockSpec((tile, sa), lambda b, i: (i, b)),
            pl.BlockSpec((tile, sb), tok),
            _const_spec((1, d)),
            _const_spec((1, d)),
            _const_spec(wg.shape),
            _const_spec(wa.shape),
            _const_spec(wb.shape),
            _const_spec(wo.shape),
        ],
        out_specs=pl.BlockSpec((tile, d), tok),
        compiler_params=_cparams(("arbitrary", "arbitrary")),
        name="mixer_merge",
    )(h, ya_tm, yb, gpre, gpost, wg, wa, wb, wo)


def kernel(x, meta_tokens, ff1_norm_pre, ff1_norm_post, mix_norm_pre, mix_norm_post, ff2_norm_pre, ff2_norm_post, ff1_w_gate, ff1_w_up, ff1_w_down, ff2_w_gate, ff2_w_up, ff2_w_down, w_in, ssm_lambda_re, ssm_lambda_im, ssm_log_dt, ssm_b_re, ssm_b_im, ssm_c_re, ssm_c_im, ssm_d, ssm_w_glu, w_branch_a, rel_bias, w_branch_b, w_out):
    nb, seq, d = x.shape
    depth = w_in.shape[0]
    assert depth == 1, "meta-token rows are only carried as keys/state of a single layer"
    assert meta_tokens.shape[0] == N_META
    su = ssm_w_glu.shape[1]
    n_att = w_branch_b.shape[1]
    row = lambda v: v.astype(F32).reshape(1, -1)
    bf = lambda w: w.astype(BF16)

    o_u, o_qi = 0, su
    o_ki = o_qi + IDX_HEADS * IDX_DIM
    o_wi = o_ki + IDX_DIM
    o_q = o_wi + IDX_HEADS
    o_k = o_q + n_att
    o_v = o_k + HEAD_DIM
    o_g = o_v + HEAD_DIM
    win = w_in[0]
    w_nn = bf(jnp.concatenate([win[:, o_u:o_qi], win[:, o_ki:o_wi], win[:, o_k:o_v]], axis=1))
    w_t = bf(jnp.concatenate([win[:, o_qi:o_ki], win[:, o_q:o_k], win[:, o_v:o_g], win[:, o_wi:o_q]], axis=1).T)
    w_gates = bf(win[:, o_g:])

    hx = x.reshape(nb * seq, d)
    pad = PROJ_TILE // 4
    hm = jnp.zeros((pad, d), x.dtype).at[:N_META].set(meta_tokens.astype(x.dtype))

    ffn1 = (row(ff1_norm_pre[0]), row(ff1_norm_post[0]), bf(ff1_w_gate[0]), bf(ff1_w_up[0]), bf(ff1_w_down[0]))
    ffn2 = (row(ff2_norm_pre[0]), row(ff2_norm_post[0]), bf(ff2_w_gate[0]), bf(ff2_w_up[0]), bf(ff2_w_down[0]))

    h1 = _ffn(hx, *ffn1, FFN_TILE)
    h1m = _ffn(hm, *ffn1, pad)

    g_mix = row(mix_norm_pre[0])
    u_tm, kidx, k, qidxT, qT, vT, wT = _proj(h1, g_mix, w_nn, w_t, nb, seq, PROJ_TILE)
    u_m, kidx_m, k_m, _, _, vT_m, _ = _proj(h1m, g_mix, w_nn, w_t, 1, pad, pad)

    bmat, a_re, a_im, cmat, d_skip = _s5_params(
        ssm_lambda_re[0], ssm_lambda_im[0], ssm_log_dt[0], ssm_b_re[0], ssm_b_im[0],
        ssm_c_re[0], ssm_c_im[0], ssm_d[0])
    ya = _s5(u_tm.reshape(seq * nb, su), u_m[:N_META], bmat, a_re, a_im, cmat, d_skip, bf(ssm_w_glu[0]), nb)
    ya_tm = ya.reshape(seq, nb * su)

    bias, bias_m = _bias_tables(rel_bias, DSA_TILE, seq)
    yb = _dsa(qidxT, qT, wT, kidx, k, vT, kidx_m[:N_META], k_m[:N_META], vT_m[:, :N_META],
              bias, bias_m, nb, seq)

    h2 = _merge(h1, ya_tm, yb, g_mix, row(mix_norm_post[0]), w_gates, bf(w_branch_a[0]),
                bf(w_branch_b[0]), bf(w_out[0]), nb, seq, MERGE_TILE)

    out = _ffn(h2, *ffn2, FFN_TILE)
    return out.reshape(nb, seq, d)
```

```python
import functools
import math

import numpy as np
import jax
import jax.numpy as jnp
from jax import lax
from jax.experimental import pallas as pl
from jax.experimental.pallas import tpu as pltpu

F32 = jnp.float32
BF16 = jnp.bfloat16

RMS_EPS = 1e-6
CHUNK = 64
N_META = 16
SSM_GROUP = 16
SSM_STATE = 64
HEAD_DIM = 64
IDX_DIM = 64
IDX_HEADS = 8
TOPK_MAX = 256
REL_BUCKETS = 32
REL_MAX_DIST = 128
NEG = -1e30
BISECT_STEPS = 64

VMEM_LIMIT = 56 * 1024 * 1024

FFN_TILE = 512
FFN_CHUNK = 256
PROJ_TILE = 512
S5_STEPS = 32
S5_LANES = 512
DSA_TILE = 256
MERGE_TILE = 512


def _cparams(sem):
    return pltpu.CompilerParams(dimension_semantics=sem, vmem_limit_bytes=VMEM_LIMIT)


def _const_spec(shape):
    nd = len(shape)
    return pl.BlockSpec(shape, lambda *_: (0,) * nd)


def _rms(x, g):
    return x * lax.rsqrt(jnp.mean(x * x, axis=-1, keepdims=True) + RMS_EPS) * g


def _dot(a, b):
    return jnp.dot(a, b, preferred_element_type=F32)


def _dot_nt(a, b):
    return lax.dot_general(a, b, (((1,), (1,)), ((), ())), preferred_element_type=F32)


def _ffn_kernel(x_ref, gpre_ref, gpost_ref, wg_ref, wu_ref, wd_ref, o_ref, acc_ref):
    x = x_ref[...]
    hn = _rms(x, gpre_ref[...]).astype(BF16)
    n_chunks = wg_ref.shape[1] // FFN_CHUNK
    for c in range(n_chunks):
        cols = slice(c * FFN_CHUNK, (c + 1) * FFN_CHUNK)
        g = _dot(hn, wg_ref[:, cols])
        u = _dot(hn, wu_ref[:, cols])
        a = (jax.nn.silu(g) * u).astype(BF16)
        part = _dot(a, wd_ref[cols, :])
        if c == 0:
            acc_ref[...] = part
        else:
            acc_ref[...] += part
    o_ref[...] = x + 0.5 * _rms(acc_ref[...], gpost_ref[...])


def _ffn(x, gpre, gpost, wg, wu, wd, tile):
    t, d = x.shape
    dff = wg.shape[1]
    return pl.pallas_call(
        _ffn_kernel,
        out_shape=jax.ShapeDtypeStruct((t, d), F32),
        grid=(t // tile,),
        in_specs=[
            pl.BlockSpec((tile, d), lambda i: (i, 0)),
            _const_spec((1, d)),
            _const_spec((1, d)),
            _const_spec((d, dff)),
            _const_spec((d, dff)),
            _const_spec((dff, d)),
        ],
        out_specs=pl.BlockSpec((tile, d), lambda i: (i, 0)),
        scratch_shapes=[pltpu.VMEM((tile, d), F32)],
        compiler_params=_cparams(("arbitrary",)),
        name="ffn_half_step",
    )(x, gpre, gpost, wg, wu, wd)


def _proj_kernel(h_ref, g_ref, wn_ref, wt_ref, u_ref, kidx_ref, k_ref, qidxT_ref, qT_ref, vT_ref, wT_ref):
    hn = _rms(h_ref[...], g_ref[...]).astype(BF16)
    nn = _dot(hn, wn_ref[...])
    su = u_ref.shape[1]
    u_ref[...] = nn[:, :su].astype(BF16)
    kidx_ref[...] = nn[:, su:su + IDX_DIM].astype(BF16)
    k_ref[...] = nn[:, su + IDX_DIM:su + IDX_DIM + HEAD_DIM].astype(BF16)
    tt = _dot_nt(wt_ref[...], hn)
    nq = qidxT_ref.shape[0]
    na = qT_ref.shape[0]
    qidxT_ref[...] = (tt[:nq] * (IDX_DIM ** -0.5)).astype(BF16)
    qT_ref[...] = (tt[nq:nq + na] * (HEAD_DIM ** -0.5)).astype(BF16)
    vT_ref[...] = tt[nq + na:nq + na + HEAD_DIM].astype(BF16)
    wT_ref[...] = tt[nq + na + HEAD_DIM:nq + na + HEAD_DIM + IDX_HEADS] * (IDX_HEADS ** -0.5)


def _proj(h, gain, w_nn, w_t, nb, seq, tile):
    d = h.shape[1]
    su = w_nn.shape[1] - IDX_DIM - HEAD_DIM
    nq = IDX_HEADS * IDX_DIM
    na = w_t.shape[0] - nq - HEAD_DIM - IDX_HEADS
    nt = seq // tile
    tok = lambda b, i: (b * nt + i, 0)
    tokT = lambda b, i: (0, b * nt + i)
    out_shape = (
        jax.ShapeDtypeStruct((seq, nb * su), BF16),
        jax.ShapeDtypeStruct((nb * seq, IDX_DIM), BF16),
        jax.ShapeDtypeStruct((nb * seq, HEAD_DIM), BF16),
        jax.ShapeDtypeStruct((nq, nb * seq), BF16),
        jax.ShapeDtypeStruct((na, nb * seq), BF16),
        jax.ShapeDtypeStruct((HEAD_DIM, nb * seq), BF16),
        jax.ShapeDtypeStruct((IDX_HEADS, nb * seq), F32),
    )
    out_specs = (
        pl.BlockSpec((tile, su), lambda b, i: (i, b)),
        pl.BlockSpec((tile, IDX_DIM), tok),
        pl.BlockSpec((tile, HEAD_DIM), tok),
        pl.BlockSpec((nq, tile), tokT),
        pl.BlockSpec((na, tile), tokT),
        pl.BlockSpec((HEAD_DIM, tile), tokT),
        pl.BlockSpec((IDX_HEADS, tile), tokT),
    )
    return pl.pallas_call(
        _proj_kernel,
        out_shape=out_shape,
        grid=(nb, nt),
        in_specs=[
            pl.BlockSpec((tile, d), tok),
            _const_spec((1, d)),
            _const_spec(w_nn.shape),
            _const_spec(w_t.shape),
        ],
        out_specs=out_specs,
        compiler_params=_cparams(("arbitrary", "arbitrary")),
        name="mixer_in_proj",
    )(h, gain, w_nn, w_t)


def _s5_scan(bu_ref, xs_ref, st_ref, are_ref, aim_ref, n_steps, nb, bcast_rows):
    n_state = are_ref.shape[1]
    for c in range(n_state // S5_LANES):
        lanes = slice(c * S5_LANES, (c + 1) * S5_LANES)
        lanes_im = slice(n_state + c * S5_LANES, n_state + (c + 1) * S5_LANES)
        a_re = jnp.broadcast_to(are_ref[:, lanes], (nb, S5_LANES))
        a_im = jnp.broadcast_to(aim_ref[:, lanes], (nb, S5_LANES))

        def step(t, carry):
            x_re, x_im = carry
            if bcast_rows:
                rows = pl.ds(t, 1)
            else:
                rows = pl.ds(pl.multiple_of(t * nb, nb), nb)
            b_re = bu_ref[rows, lanes]
            b_im = bu_ref[rows, lanes_im]
            n_re = a_re * x_re - a_im * x_im + b_re
            n_im = a_re * x_im + a_im * x_re + b_im
            if xs_ref is not None:
                xs_ref[rows, lanes] = n_re.astype(BF16)
                xs_ref[rows, lanes_im] = n_im.astype(BF16)
            return n_re, n_im

        x_re, x_im = lax.fori_loop(0, n_steps, step, (st_ref[0, :, lanes], st_ref[1, :, lanes]))
        st_ref[0, :, lanes] = x_re
        st_ref[1, :, lanes] = x_im


def _s5_kernel(u_ref, um_ref, bmat_ref, are_ref, aim_ref, cmat_ref, d_ref, wglu_ref, y_ref,
               bu_ref, xs_ref, st_ref, bum_ref):
    nb = st_ref.shape[1]
    n_state = are_ref.shape[1]
    half_u = u_ref.shape[1] // 2
    half_s = n_state // 2

    def b_project(u, dst_ref):
        for hf in range(2):
            uu = u[:, hf * half_u:(hf + 1) * half_u]
            bu = _dot(uu, bmat_ref[hf])
            dst_ref[:, hf * half_s:(hf + 1) * half_s] = bu[:, :half_s]
            dst_ref[:, n_state + hf * half_s:n_state + (hf + 1) * half_s] = bu[:, half_s:]

    @pl.when(pl.program_id(0) == 0)
    def _():
        st_ref[...] = jnp.zeros_like(st_ref)
        b_project(um_ref[...], bum_ref)
        _s5_scan(bum_ref, None, st_ref, are_ref, aim_ref, um_ref.shape[0], nb, True)

    u = u_ref[...]
    b_project(u, bu_ref)
    _s5_scan(bu_ref, xs_ref, st_ref, are_ref, aim_ref, u_ref.shape[0] // nb, nb, False)

    ys = []
    for hf in range(2):
        y_re = _dot(xs_ref[:, hf * half_s:(hf + 1) * half_s], cmat_ref[hf, 0])
        y_im = _dot(xs_ref[:, n_state + hf * half_s:n_state + (hf + 1) * half_s], cmat_ref[hf, 1])
        ys.append(y_re - y_im)
    y = jnp.concatenate(ys, axis=1) + d_ref[...] * u.astype(F32)
    y = jax.nn.gelu(y)
    y = y * jax.nn.sigmoid(_dot(y.astype(BF16), wglu_ref[...]))
    y_ref[...] = y.astype(BF16)


def _s5(u_tm, u_meta, bmat, a_re, a_im, cmat, d_skip, w_glu, nb):
    rows_total, su = u_tm.shape
    rows = S5_STEPS * nb
    n_state = a_re.shape[1]
    return pl.pallas_call(
        _s5_kernel,
        out_shape=jax.ShapeDtypeStruct((rows_total, su), BF16),
        grid=(rows_total // rows,),
        in_specs=[
            pl.BlockSpec((rows, su), lambda i: (i, 0)),
            _const_spec(u_meta.shape),
            _const_spec(bmat.shape),
            _const_spec(a_re.shape),
            _const_spec(a_im.shape),
            _const_spec(cmat.shape),
            _const_spec(d_skip.shape),
            _const_spec(w_glu.shape),
        ],
        out_specs=pl.BlockSpec((rows, su), lambda i: (i, 0)),
        scratch_shapes=[
            pltpu.VMEM((rows, 2 * n_state), F32),
            pltpu.VMEM((rows, 2 * n_state), BF16),
            pltpu.VMEM((2, nb, n_state), F32),
            pltpu.VMEM((u_meta.shape[0], 2 * n_state), F32),
        ],
        compiler_params=_cparams(("arbitrary",)),
        name="s5_mixer",
    )(u_tm, u_meta, bmat, a_re, a_im, cmat, d_skip, w_glu)


def _s5_params(lambda_re, lambda_im, log_dt, b_re, b_im, c_re, c_im, d_skip):
    g, p, m = b_re.shape
    lr = lambda_re.astype(F32)
    li = lambda_im.astype(F32)
    dt = jnp.exp(log_dt.astype(F32))[:, None]
    mag = jnp.exp(lr * dt)
    a_re = mag * jnp.cos(li * dt)
    a_im = mag * jnp.sin(li * dt)
    den = lr * lr + li * li
    num_re = a_re - 1.0
    num_im = a_im
    coef_re = (num_re * lr + num_im * li) / den
    coef_im = (num_im * lr - num_re * li) / den
    br = b_re.astype(F32)
    bi = b_im.astype(F32)
    bbar_re = coef_re[..., None] * br - coef_im[..., None] * bi
    bbar_im = coef_re[..., None] * bi + coef_im[..., None] * br
    gh = g // 2
    eye = jnp.eye(gh, dtype=F32)

    def in_mat(bb):
        bb = bb.reshape(2, gh, p, m)
        return jnp.einsum('hgpm,gk->hgmkp', bb, eye).reshape(2, gh * m, gh * p)

    def out_mat(cc):
        cc = cc.astype(F32).reshape(2, gh, m, p)
        return jnp.einsum('hgmp,gk->hgpkm', cc, eye).reshape(2, gh * p, gh * m)

    bmat = jnp.concatenate([in_mat(bbar_re), in_mat(bbar_im)], axis=2).astype(BF16)
    cmat = jnp.stack([out_mat(c_re), out_mat(c_im)], axis=1).astype(BF16)
    return (bmat, a_re.reshape(1, g * p), a_im.reshape(1, g * p), cmat,
            d_skip.astype(F32).reshape(1, g * m))


def _colsum(x):
    return jnp.sum(x.reshape(x.shape[0] // 8, 8, x.shape[1]), axis=0)


def _colmax(x):
    return jnp.max(x.reshape(x.shape[0] // 8, 8, x.shape[1]), axis=0)


def _colmin(x):
    return jnp.min(x.reshape(x.shape[0] // 8, 8, x.shape[1]), axis=0)


def _dsa_kernel(top_k, qidxT_ref, qT_ref, wT_ref, kidx_ref, k_ref, vT_ref, kidxm_ref, km_ref, vTm_ref,
                bias_ref, biasm_ref, tri_ref, trim_ref, y_ref,
                sc_ref, scm_ref, m_ref, l_ref, acc_ref, lg_ref, p_ref):
    i = pl.program_id(1)
    tq = qT_ref.shape[1]
    tk = tq
    nkb = i + 1
    n_heads = qT_ref.shape[0] // HEAD_DIM
    kf = float(top_k)

    def kblock(kb):
        return pl.ds(pl.multiple_of(kb * tk, tk), tk)

    kc = lax.broadcasted_iota(jnp.int32, (tk, tq), 0) // CHUNK
    qc = lax.broadcasted_iota(jnp.int32, (tk, tq), 1) // CHUNK
    adm_diag = kc <= qc

    def idx_score(kidx):
        s = jnp.zeros((kidx.shape[0], tq), F32)
        for h in range(IDX_HEADS):
            sh = _dot(kidx, qidxT_ref[h * IDX_DIM:(h + 1) * IDX_DIM, :])
            s = s + jnp.maximum(sh, 0.0) * wT_ref[h:h + 1, :]
        return s

    sm = idx_score(kidxm_ref[...])
    scm_ref[...] = sm

    def score_body(kb, carry):
        mn, mx = carry
        s = idx_score(kidx_ref[kblock(kb), :])
        adm = jnp.logical_or(kb < i, adm_diag)
        sc_ref[kblock(kb), :] = jnp.where(adm, s, NEG)
        mn = jnp.minimum(mn, _colmin(jnp.where(adm, s, -NEG)))
        mx = jnp.maximum(mx, _colmax(jnp.where(adm, s, NEG)))
        return mn, mx

    mn, mx = lax.fori_loop(0, nkb, score_body, (_colmin(sm), _colmax(sm)))
    mn = jnp.min(mn, axis=0, keepdims=True)
    mx = jnp.max(mx, axis=0, keepdims=True)

    def reduce_keys(fn, combine, finish):
        acc = fn(scm_ref[...])
        acc = lax.fori_loop(0, nkb, lambda kb, a: combine(a, fn(sc_ref[kblock(kb), :])), acc)
        return finish(acc)

    def count_gt(t):
        return reduce_keys(lambda s: _colsum(jnp.where(s > t, 1.0, 0.0)), jnp.add,
                           lambda a: jnp.sum(a, axis=0, keepdims=True))

    n_adm = (N_META + i * tk + (lax.broadcasted_iota(jnp.int32, (1, tq), 1) // CHUNK + 1) * CHUNK).astype(F32)
    c_pos = count_gt(jnp.zeros((1, tq), F32))
    c_nonneg = reduce_keys(lambda s: _colsum(jnp.where(s >= 0.0, 1.0, 0.0)), jnp.add,
                           lambda a: jnp.sum(a, axis=0, keepdims=True))
    zero_vk = jnp.logical_and(c_pos < kf, c_nonneg >= kf)
    pos = c_pos >= kf
    lo0 = jnp.where(pos, 0.0, mn - (1.0 + jnp.abs(mn)))
    hi0 = jnp.where(pos, mx, 0.0)
    clo0 = jnp.where(pos, c_pos, n_adm)

    def open_rows(clo):
        return jnp.max(jnp.where(zero_vk, 0.0, clo)) > kf

    def search_cond(carry):
        it, _, _, _, go = carry
        return jnp.logical_and(it < BISECT_STEPS, go)

    def search_body(carry):
        it, lo, hi, clo, _ = carry
        for _ in range(2):
            mid = 0.5 * lo + 0.5 * hi
            c = count_gt(mid)
            up = c >= kf
            lo = jnp.where(up, mid, lo)
            clo = jnp.where(up, c, clo)
            hi = jnp.where(up, hi, mid)
        return it + 2, lo, hi, clo, open_rows(clo)

    _, lo, _, _, _ = lax.while_loop(search_cond, search_body, (0, lo0, hi0, clo0, open_rows(clo0)))

    thr = reduce_keys(lambda s: _colmin(jnp.where(s > lo, s, -NEG)), jnp.minimum,
                      lambda a: jnp.min(a, axis=0, keepdims=True))
    thr = jnp.where(zero_vk, 0.0, thr)
    need = kf - count_gt(thr)

    def sel_mask(s, tri, carry):
        eq = jnp.where(s == thr, 1.0, 0.0)
        rank = carry + _dot(tri, eq.astype(BF16))
        sel = jnp.logical_or(s > thr, jnp.logical_and(s == thr, rank < need))
        return jnp.where(sel, 0.0, NEG), carry + jnp.sum(_colsum(eq), axis=0, keepdims=True)

    def head_rows(h):
        return slice(h * HEAD_DIM, (h + 1) * HEAD_DIM)

    mskm, carry0 = sel_mask(scm_ref[...], trim_ref[...], jnp.zeros((1, tq), F32))
    km = km_ref[...]
    vTm = vTm_ref[...]
    for h in range(n_heads):
        lg = _dot(km, qT_ref[head_rows(h), :]) + biasm_ref[h] + mskm
        m = jnp.max(lg, axis=0, keepdims=True)
        p = jnp.exp(lg - m)
        m_ref[h:h + 1, :] = m
        l_ref[h:h + 1, :] = jnp.sum(p, axis=0, keepdims=True)
        acc_ref[head_rows(h), :] = _dot(vTm, p.astype(BF16))

    def att_body(kb, carry):
        msk, carry = sel_mask(sc_ref[kblock(kb), :], tri_ref[...], carry)
        which = jnp.clip(kb - i + 2, 0, 2)
        kblk = k_ref[kblock(kb), :]
        vblk = vT_ref[:, kblock(kb)]
        for h in range(n_heads):
            lg_ref[h] = _dot(kblk, qT_ref[head_rows(h), :])
        alphas = []
        for h in range(n_heads):
            lg = lg_ref[h] + bias_ref[h, which] + msk
            m_old = m_ref[h:h + 1, :]
            m_new = jnp.maximum(m_old, jnp.max(_colmax(lg), axis=0, keepdims=True))
            alpha = jnp.exp(m_old - m_new)
            p = jnp.exp(lg - m_new)
            m_ref[h:h + 1, :] = m_new
            l_ref[h:h + 1, :] = alpha * l_ref[h:h + 1, :] + jnp.sum(_colsum(p), axis=0, keepdims=True)
            p_ref[h] = p.astype(BF16)
            alphas.append(alpha)
        for h in range(n_heads):
            acc_ref[head_rows(h), :] = alphas[h] * acc_ref[head_rows(h), :] + _dot(vblk, p_ref[h])
        return carry

    lax.fori_loop(0, nkb, att_body, carry0)

    for h in range(n_heads):
        acc_ref[head_rows(h), :] = acc_ref[head_rows(h), :] / l_ref[h:h + 1, :]
    y_ref[...] = acc_ref[...].T.astype(BF16)


def _rel_bucket_np(rel):
    half = REL_BUCKETS // 2
    max_exact = half // 2
    base = np.where(rel > 0, half, 0)
    n = np.abs(rel)
    nf = np.maximum(n, 1).astype(np.float64)
    large = max_exact + (np.log(nf / max_exact) / math.log(REL_MAX_DIST / max_exact)
                         * (half - max_exact)).astype(np.int32)
    large = np.minimum(large, half - 1)
    return base + np.where(n < max_exact, n, large)


def _bias_of_rel(rb, rel, rel_lo, rel_hi):
    rels = np.arange(rel_lo, rel_hi + 1)
    buckets = _rel_bucket_np(rels)
    col = lambda b: rb[int(b)].reshape((-1,) + (1,) * rel.ndim)
    val = jnp.broadcast_to(col(buckets[0]), (rb.shape[1],) + rel.shape)
    for j in range(1, len(rels)):
        if buckets[j] != buckets[j - 1]:
            val = jnp.where(rel[None] >= int(rels[j]), col(buckets[j]), val)
    return val


def _bias_tables(rel_bias, tq, seq):
    rb = rel_bias.astype(F32)
    far = -2 * tq
    assert np.all(_rel_bucket_np(np.arange(-(seq + N_META), -tq)) == _rel_bucket_np(np.array(far)))
    d = lax.broadcasted_iota(jnp.int32, (tq, tq), 0) - lax.broadcasted_iota(jnp.int32, (tq, tq), 1)
    rel = jnp.stack([jnp.full((tq, tq), far, jnp.int32), d - tq, d])
    bias = _bias_of_rel(rb, rel, far, tq - 1)
    rel_m = (lax.broadcasted_iota(jnp.int32, (N_META, seq), 0)
             - lax.broadcasted_iota(jnp.int32, (N_META, seq), 1) - N_META)
    bias_m = _bias_of_rel(rb, rel_m, -(seq + N_META), -1)
    return bias, bias_m


def _dsa(qidxT, qT, wT, kidx, k, vT, kidx_m, k_m, vT_m, bias, bias_m, nb, seq):
    tq = DSA_TILE
    nq = seq // tq
    n_att = qT.shape[0]
    n_heads = n_att // HEAD_DIM
    tri = jnp.asarray(np.tril(np.ones((tq, tq), np.float32), -1), BF16)
    tri_m = jnp.asarray(np.tril(np.ones((N_META, N_META), np.float32), -1), BF16)
    qcol = lambda b, i: (0, b * nq + i)
    return pl.pallas_call(
        functools.partial(_dsa_kernel, min(TOPK_MAX, seq // 4)),
        out_shape=jax.ShapeDtypeStruct((nb * seq, n_att), BF16),
        grid=(nb, nq),
        in_specs=[
            pl.BlockSpec((qidxT.shape[0], tq), qcol),
            pl.BlockSpec((n_att, tq), qcol),
            pl.BlockSpec((IDX_HEADS, tq), qcol),
            pl.BlockSpec((seq, IDX_DIM), lambda b, i: (b, 0)),
            pl.BlockSpec((seq, HEAD_DIM), lambda b, i: (b, 0)),
            pl.BlockSpec((HEAD_DIM, seq), lambda b, i: (0, b)),
            _const_spec(kidx_m.shape),
            _const_spec(k_m.shape),
            _const_spec(vT_m.shape),
            _const_spec(bias.shape),
            pl.BlockSpec((bias_m.shape[0], N_META, tq), lambda b, i: (0, 0, i)),
            _const_spec(tri.shape),
            _const_spec(tri_m.shape),
        ],
        out_specs=pl.BlockSpec((tq, n_att), lambda b, i: (b * nq + i, 0)),
        scratch_shapes=[
            pltpu.VMEM((seq, tq), F32),
            pltpu.VMEM((N_META, tq), F32),
            pltpu.VMEM((n_heads, tq), F32),
            pltpu.VMEM((n_heads, tq), F32),
            pltpu.VMEM((n_att, tq), F32),
            pltpu.VMEM((n_heads, tq, tq), F32),
            pltpu.VMEM((n_heads, tq, tq), BF16),
        ],
        compiler_params=_cparams(("arbitrary", "arbitrary")),
        name="dsa_mixer",
    )(qidxT, qT, wT, kidx, k, vT, kidx_m, k_m, vT_m, bias, bias_m, tri, tri_m)


def _merge_kernel(h_ref, ya_ref, yb_ref, gpre_ref, gpost_ref, wg_ref, wa_ref, wb_ref, wo_ref, o_ref):
    h = h_ref[...]
    d = h.shape[1]
    hn = _rms(h, gpre_ref[...]).astype(BF16)
    pa = _dot(ya_ref[...], wa_ref[...])
    pb = _dot(yb_ref[...], wb_ref[...])
    g0 = jax.nn.sigmoid(_dot(hn, wg_ref[:, :d]))
    g1 = jax.nn.sigmoid(_dot(hn, wg_ref[:, d:]))
    merged = (g0 * pa + g1 * pb).astype(BF16)
    o_ref[...] = h + _rms(_dot(merged, wo_ref[...]), gpost_ref[...])


def _merge(h, ya_tm, yb, gpre, gpost, wg, wa, wb, wo, nb, seq, tile):
    d = h.shape[1]
    sa = wa.shape[0]
    sb = wb.shape[0]
    nt = seq // tile
    tok = lambda b, i: (b * nt + i, 0)
    return pl.pallas_call(
        _merge_kernel,
        out_shape=jax.ShapeDtypeStruct(h.shape, F32),
        grid=(nb, nt),
        in_specs=[
            pl.BlockSpec((tile, d), tok),
            pl.BlockSpec((tile, sa), lambda b, i: (i, b)),
            pl.BlockSpec((tile, sb), tok),
            _const_spec((1, d)),
            _const_spec((1, d)),
            _const_spec(wg.shape),
            _const_spec(wa.shape),
            _const_spec(wb.shape),
            _const_spec(wo.shape),
        ],
        out_specs=pl.BlockSpec((tile, d), tok),
        compiler_params=_cparams(("arbitrary", "arbitrary")),
        name="mixer_merge",
    )(h, ya_tm, yb, gpre, gpost, wg, wa, wb, wo)


def kernel(x, meta_tokens, ff1_norm_pre, ff1_norm_post, mix_norm_pre, mix_norm_post, ff2_norm_pre, ff2_norm_post, ff1_w_gate, ff1_w_up, ff1_w_down, ff2_w_gate, ff2_w_up, ff2_w_down, w_in, ssm_lambda_re, ssm_lambda_im, ssm_log_dt, ssm_b_re, ssm_b_im, ssm_c_re, ssm_c_im, ssm_d, ssm_w_glu, w_branch_a, rel_bias, w_branch_b, w_out):
    nb, seq, d = x.shape
    depth = w_in.shape[0]
    assert depth == 1, "meta-token rows are only carried as keys/state of a single layer"
    assert meta_tokens.shape[0] == N_META
    su = ssm_w_glu.shape[1]
    n_att = w_branch_b.shape[1]
    row = lambda v: v.astype(F32).reshape(1, -1)
    bf = lambda w: w.astype(BF16)

    o_u, o_qi = 0, su
    o_ki = o_qi + IDX_HEADS * IDX_DIM
    o_wi = o_ki + IDX_DIM
    o_q = o_wi + IDX_HEADS
    o_k = o_q + n_att
    o_v = o_k + HEAD_DIM
    o_g = o_v + HEAD_DIM
    win = w_in[0]
    w_nn = bf(jnp.concatenate([win[:, o_u:o_qi], win[:, o_ki:o_wi], win[:, o_k:o_v]], axis=1))
    w_t = bf(jnp.concatenate([win[:, o_qi:o_ki], win[:, o_q:o_k], win[:, o_v:o_g], win[:, o_wi:o_q]], axis=1).T)
    w_gates = bf(win[:, o_g:])

    hx = x.reshape(nb * seq, d)
    pad = PROJ_TILE // 4
    hm = jnp.zeros((pad, d), x.dtype).at[:N_META].set(meta_tokens.astype(x.dtype))

    ffn1 = (row(ff1_norm_pre[0]), row(ff1_norm_post[0]), bf(ff1_w_gate[0]), bf(ff1_w_up[0]), bf(ff1_w_down[0]))
    ffn2 = (row(ff2_norm_pre[0]), row(ff2_norm_post[0]), bf(ff2_w_gate[0]), bf(ff2_w_up[0]), bf(ff2_w_down[0]))

    h1 = _ffn(hx, *ffn1, FFN_TILE)
    h1m = _ffn(hm, *ffn1, pad)

    g_mix = row(mix_norm_pre[0])
    u_tm, kidx, k, qidxT, qT, vT, wT = _proj(h1, g_mix, w_nn, w_t, nb, seq, PROJ_TILE)
    u_m, kidx_m, k_m, _, _, vT_m, _ = _proj(h1m, g_mix, w_nn, w_t, 1, pad, pad)

    bmat, a_re, a_im, cmat, d_skip = _s5_params(
        ssm_lambda_re[0], ssm_lambda_im[0], ssm_log_dt[0], ssm_b_re[0], ssm_b_im[0],
        ssm_c_re[0], ssm_c_im[0], ssm_d[0])
    ya = _s5(u_tm.reshape(seq * nb, su), u_m[:N_META], bmat, a_re, a_im, cmat, d_skip, bf(ssm_w_glu[0]), nb)
    ya_tm = ya.reshape(seq, nb * su)

    bias, bias_m = _bias_tables(rel_bias, DSA_TILE, seq)
    yb = _dsa(qidxT, qT, wT, kidx, k, vT, kidx_m[:N_META], k_m[:N_META], vT_m[:, :N_META],
              bias, bias_m, nb, seq)

    h2 = _merge(h1, ya_tm, yb, g_mix, row(mix_norm_post[0]), w_gates, bf(w_branch_a[0]),
                bf(w_branch_b[0]), bf(w_out[0]), nb, seq, MERGE_TILE)

    out = _ffn(h2, *ffn2, FFN_TILE)
    return out.reshape(nb, seq, d)
```

```python
import functools
import math

import numpy as np
import jax
import jax.numpy as jnp
from jax import lax
from jax.experimental import pallas as pl
from jax.experimental.pallas import tpu as pltpu

F32 = jnp.float32
BF16 = jnp.bfloat16

RMS_EPS = 1e-6
CHUNK = 64
N_META = 16
SSM_GROUP = 16
SSM_STATE = 64
HEAD_DIM = 64
IDX_DIM = 64
IDX_HEADS = 8
TOPK_MAX = 256
REL_BUCKETS = 32
REL_MAX_DIST = 128
NEG = -1e30
BISECT_STEPS = 64

VMEM_LIMIT = 56 * 1024 * 1024

FFN_TILE = 512
FFN_CHUNK = 256
PROJ_TILE = 512
S5_STEPS = 32
S5_LANES = 512
DSA_TILE = 256
SCAN_ROWS = 64
MERGE_TILE = 512


def _cparams(sem):
    return pltpu.CompilerParams(dimension_semantics=sem, vmem_limit_bytes=VMEM_LIMIT)


def _const_spec(shape):
    nd = len(shape)
    return pl.BlockSpec(shape, lambda *_: (0,) * nd)


def _rms(x, g):
    return x * lax.rsqrt(jnp.mean(x * x, axis=-1, keepdims=True) + RMS_EPS) * g


def _dot(a, b):
    return jnp.dot(a, b, preferred_element_type=F32)


def _dot_nt(a, b):
    return lax.dot_general(a, b, (((1,), (1,)), ((), ())), preferred_element_type=F32)


def _ffn_kernel(x_ref, gpre_ref, gpost_ref, wg_ref, wu_ref, wd_ref, o_ref, acc_ref):
    x = x_ref[...]
    hn = _rms(x, gpre_ref[...]).astype(BF16)
    n_chunks = wg_ref.shape[1] // FFN_CHUNK
    for c in range(n_chunks):
        cols = slice(c * FFN_CHUNK, (c + 1) * FFN_CHUNK)
        g = _dot(hn, wg_ref[:, cols])
        u = _dot(hn, wu_ref[:, cols])
        a = (jax.nn.silu(g) * u).astype(BF16)
        part = _dot(a, wd_ref[cols, :])
        if c == 0:
            acc_ref[...] = part
        else:
            acc_ref[...] += part
    o_ref[...] = x + 0.5 * _rms(acc_ref[...], gpost_ref[...])


def _ffn(x, gpre, gpost, wg, wu, wd, tile):
    t, d = x.shape
    dff = wg.shape[1]
    return pl.pallas_call(
        _ffn_kernel,
        out_shape=jax.ShapeDtypeStruct((t, d), F32),
        grid=(t // tile,),
        in_specs=[
            pl.BlockSpec((tile, d), lambda i: (i, 0)),
            _const_spec((1, d)),
            _const_spec((1, d)),
            _const_spec((d, dff)),
            _const_spec((d, dff)),
            _const_spec((dff, d)),
        ],
        out_specs=pl.BlockSpec((tile, d), lambda i: (i, 0)),
        scratch_shapes=[pltpu.VMEM((tile, d), F32)],
        compiler_params=_cparams(("arbitrary",)),
        name="ffn_half_step",
    )(x, gpre, gpost, wg, wu, wd)


def _proj_kernel(h_ref, g_ref, wn_ref, wt_ref, u_ref, kidx_ref, k_ref, qidxT_ref, qT_ref, vT_ref, wT_ref):
    hn = _rms(h_ref[...], g_ref[...]).astype(BF16)
    nn = _dot(hn, wn_ref[...])
    su = u_ref.shape[1]
    u_ref[...] = nn[:, :su].astype(BF16)
    kidx_ref[...] = nn[:, su:su + IDX_DIM].astype(BF16)
    k_ref[...] = nn[:, su + IDX_DIM:su + IDX_DIM + HEAD_DIM].astype(BF16)
    tt = _dot_nt(wt_ref[...], hn)
    nq = qidxT_ref.shape[0]
    na = qT_ref.shape[0]
    qidxT_ref[...] = (tt[:nq] * (IDX_DIM ** -0.5)).astype(BF16)
    qT_ref[...] = (tt[nq:nq + na] * (HEAD_DIM ** -0.5)).astype(BF16)
    vT_ref[...] = tt[nq + na:nq + na + HEAD_DIM].astype(BF16)
    wT_ref[...] = tt[nq + na + HEAD_DIM:nq + na + HEAD_DIM + IDX_HEADS] * (IDX_HEADS ** -0.5)


def _proj(h, gain, w_nn, w_t, nb, seq, tile):
    d = h.shape[1]
    su = w_nn.shape[1] - IDX_DIM - HEAD_DIM
    nq = IDX_HEADS * IDX_DIM
    na = w_t.shape[0] - nq - HEAD_DIM - IDX_HEADS
    nt = seq // tile
    tok = lambda b, i: (b * nt + i, 0)
    tokT = lambda b, i: (0, b * nt + i)
    out_shape = (
        jax.ShapeDtypeStruct((nb * seq, su), BF16),
        jax.ShapeDtypeStruct((nb * seq, IDX_DIM), BF16),
        jax.ShapeDtypeStruct((nb * seq, HEAD_DIM), BF16),
        jax.ShapeDtypeStruct((nq, nb * seq), BF16),
        jax.ShapeDtypeStruct((na, nb * seq), BF16),
        jax.ShapeDtypeStruct((HEAD_DIM, nb * seq), BF16),
        jax.ShapeDtypeStruct((IDX_HEADS, nb * seq), F32),
    )
    out_specs = (
        pl.BlockSpec((tile, su), tok),
        pl.BlockSpec((tile, IDX_DIM), tok),
        pl.BlockSpec((tile, HEAD_DIM), tok),
        pl.BlockSpec((nq, tile), tokT),
        pl.BlockSpec((na, tile), tokT),
        pl.BlockSpec((HEAD_DIM, tile), tokT),
        pl.BlockSpec((IDX_HEADS, tile), tokT),
    )
    return pl.pallas_call(
        _proj_kernel,
        out_shape=out_shape,
        grid=(nb, nt),
        in_specs=[
            pl.BlockSpec((tile, d), tok),
            _const_spec((1, d)),
            _const_spec(w_nn.shape),
            _const_spec(w_t.shape),
        ],
        out_specs=out_specs,
        compiler_params=_cparams(("arbitrary", "arbitrary")),
        name="mixer_in_proj",
    )(h, gain, w_nn, w_t)


def _s5_scan(bu_ref, xs_ref, st_ref, are_ref, aim_ref, chunks, n_steps, nb, bcast_rows):
    n_state = are_ref.shape[1]
    for c in chunks:
        lanes = slice(c * S5_LANES, (c + 1) * S5_LANES)
        lanes_im = slice(n_state + c * S5_LANES, n_state + (c + 1) * S5_LANES)
        a_re = jnp.broadcast_to(are_ref[:, lanes], (nb, S5_LANES))
        a_im = jnp.broadcast_to(aim_ref[:, lanes], (nb, S5_LANES))
        x_re = st_ref[0, :, lanes]
        x_im = st_ref[1, :, lanes]
        for t in range(n_steps):
            rows = slice(t, t + 1) if bcast_rows else slice(t * nb, (t + 1) * nb)
            b_re = bu_ref[rows, lanes]
            b_im = bu_ref[rows, lanes_im]
            x_re, x_im = (a_re * x_re - a_im * x_im + b_re,
                          a_re * x_im + a_im * x_re + b_im)
            if xs_ref is not None:
                xs_ref[rows, lanes] = x_re.astype(BF16)
                xs_ref[rows, lanes_im] = x_im.astype(BF16)
        st_ref[0, :, lanes] = x_re
        st_ref[1, :, lanes] = x_im


def _s5_kernel(u_ref, um_ref, bmat_ref, are_ref, aim_ref, cmat_ref, d_ref, wglu_ref, y_ref,
               bu_ref, xs_ref, st_ref, bum_ref):
    nb, n_steps, su = u_ref.shape
    n_state = are_ref.shape[1]
    half_u = su // 2
    half_s = n_state // 2
    chunks_per_half = half_s // S5_LANES

    def b_project(u, dst_ref):
        for hf in range(2):
            uu = u[:, hf * half_u:(hf + 1) * half_u]
            bu = _dot(uu, bmat_ref[hf])
            dst_ref[:, hf * half_s:(hf + 1) * half_s] = bu[:, :half_s]
            dst_ref[:, n_state + hf * half_s:n_state + (hf + 1) * half_s] = bu[:, half_s:]

    @pl.when(pl.program_id(0) == 0)
    def _():
        st_ref[...] = jnp.zeros_like(st_ref)
        b_project(um_ref[...], bum_ref)
        _s5_scan(bum_ref, None, st_ref, are_ref, aim_ref, range(2 * chunks_per_half), um_ref.shape[0], nb, True)

    u = pltpu.einshape("btc->tbc", u_ref[...]).reshape(n_steps * nb, su)
    b_project(u, bu_ref)
    ys = []
    for hf in range(2):
        _s5_scan(bu_ref, xs_ref, st_ref, are_ref, aim_ref,
                 range(hf * chunks_per_half, (hf + 1) * chunks_per_half), n_steps, nb, False)
        y_re = _dot(xs_ref[:, hf * half_s:(hf + 1) * half_s], cmat_ref[hf, 0])
        y_im = _dot(xs_ref[:, n_state + hf * half_s:n_state + (hf + 1) * half_s], cmat_ref[hf, 1])
        ys.append(y_re - y_im)
    y = jnp.concatenate(ys, axis=1) + d_ref[...] * u.astype(F32)
    y = jax.nn.gelu(y)
    y = y * jax.nn.sigmoid(_dot(y.astype(BF16), wglu_ref[...]))
    y_ref[...] = pltpu.einshape("tbc->btc", y.astype(BF16).reshape(n_steps, nb, su))


def _s5(u, u_meta, bmat, a_re, a_im, cmat, d_skip, w_glu):
    nb, seq, su = u.shape
    rows = S5_STEPS * nb
    n_state = a_re.shape[1]
    return pl.pallas_call(
        _s5_kernel,
        out_shape=jax.ShapeDtypeStruct((nb, seq, su), BF16),
        grid=(seq // S5_STEPS,),
        in_specs=[
            pl.BlockSpec((nb, S5_STEPS, su), lambda i: (0, i, 0)),
            _const_spec(u_meta.shape),
            _const_spec(bmat.shape),
            _const_spec(a_re.shape),
            _const_spec(a_im.shape),
            _const_spec(cmat.shape),
            _const_spec(d_skip.shape),
            _const_spec(w_glu.shape),
        ],
        out_specs=pl.BlockSpec((nb, S5_STEPS, su), lambda i: (0, i, 0)),
        scratch_shapes=[
            pltpu.VMEM((rows, 2 * n_state), F32),
            pltpu.VMEM((rows, 2 * n_state), BF16),
            pltpu.VMEM((2, nb, n_state), F32),
            pltpu.VMEM((u_meta.shape[0], 2 * n_state), F32),
        ],
        compiler_params=_cparams(("arbitrary",)),
        name="s5_mixer",
    )(u, u_meta, bmat, a_re, a_im, cmat, d_skip, w_glu)


def _s5_params(lambda_re, lambda_im, log_dt, b_re, b_im, c_re, c_im, d_skip):
    g, p, m = b_re.shape
    lr = lambda_re.astype(F32)
    li = lambda_im.astype(F32)
    dt = jnp.exp(log_dt.astype(F32))[:, None]
    mag = jnp.exp(lr * dt)
    a_re = mag * jnp.cos(li * dt)
    a_im = mag * jnp.sin(li * dt)
    den = lr * lr + li * li
    num_re = a_re - 1.0
    num_im = a_im
    coef_re = (num_re * lr + num_im * li) / den
    coef_im = (num_im * lr - num_re * li) / den
    br = b_re.astype(F32)
    bi = b_im.astype(F32)
    bbar_re = coef_re[..., None] * br - coef_im[..., None] * bi
    bbar_im = coef_re[..., None] * bi + coef_im[..., None] * br
    gh = g // 2
    eye = jnp.eye(gh, dtype=F32)

    def in_mat(bb):
        bb = bb.reshape(2, gh, p, m)
        return jnp.einsum('hgpm,gk->hgmkp', bb, eye).reshape(2, gh * m, gh * p)

    def out_mat(cc):
        cc = cc.astype(F32).reshape(2, gh, m, p)
        return jnp.einsum('hgmp,gk->hgpkm', cc, eye).reshape(2, gh * p, gh * m)

    bmat = jnp.concatenate([in_mat(bbar_re), in_mat(bbar_im)], axis=2).astype(BF16)
    cmat = jnp.stack([out_mat(c_re), out_mat(c_im)], axis=1).astype(BF16)
    return (bmat, a_re.reshape(1, g * p), a_im.reshape(1, g * p), cmat,
            d_skip.astype(F32).reshape(1, g * m))


def _colsum(x):
    return jnp.sum(x.reshape(x.shape[0] // 8, 8, x.shape[1]), axis=0)


def _colmax(x):
    return jnp.max(x.reshape(x.shape[0] // 8, 8, x.shape[1]), axis=0)


def _colmin(x):
    return jnp.min(x.reshape(x.shape[0] // 8, 8, x.shape[1]), axis=0)


def _dsa_kernel(top_k, qidxT_ref, qT_ref, wT_ref, kidx_ref, k_ref, vT_ref, kidxm_ref, km_ref, vTm_ref,
                bias_ref, biasm_ref, tri_ref, trim_ref, y_ref,
                sc_ref, scm_ref, m_ref, l_ref, acc_ref, lg_ref, p_ref):
    i = pl.program_id(1)
    tq = qT_ref.shape[1]
    tk = tq
    nkb = i + 1
    n_heads = qT_ref.shape[0] // HEAD_DIM
    kf = float(top_k)

    def kblock(kb):
        return pl.ds(pl.multiple_of(kb * tk, tk), tk)

    kc = lax.broadcasted_iota(jnp.int32, (tk, tq), 0) // CHUNK
    qc = lax.broadcasted_iota(jnp.int32, (tk, tq), 1) // CHUNK
    adm_diag = kc <= qc

    def idx_score(kidx):
        s = jnp.zeros((kidx.shape[0], tq), F32)
        for h in range(IDX_HEADS):
            sh = _dot(kidx, qidxT_ref[h * IDX_DIM:(h + 1) * IDX_DIM, :])
            s = s + jnp.maximum(sh, 0.0) * wT_ref[h:h + 1, :]
        return s

    sm = idx_score(kidxm_ref[...])
    scm_ref[...] = sm

    def score_body(kb, carry):
        mn, mx = carry
        s = idx_score(kidx_ref[kblock(kb), :])
        adm = jnp.logical_or(kb < i, adm_diag)
        sc_ref[kblock(kb), :] = jnp.where(adm, s, NEG)
        mn = jnp.minimum(mn, _colmin(jnp.where(adm, s, -NEG)))
        mx = jnp.maximum(mx, _colmax(jnp.where(adm, s, NEG)))
        return mn, mx

    mn, mx = lax.fori_loop(0, nkb, score_body, (_colmin(sm), _colmax(sm)))
    mn = jnp.min(mn, axis=0, keepdims=True)
    mx = jnp.max(mx, axis=0, keepdims=True)

    def reduce_keys(fn, combine, finish):
        def body(kb, acc):
            base = pl.multiple_of(kb * tk, tk)
            for r in range(0, tk, SCAN_ROWS):
                acc = combine(acc, fn(sc_ref[pl.ds(base + r, SCAN_ROWS), :]))
            return acc

        return finish(lax.fori_loop(0, nkb, body, fn(scm_ref[...])))

    def count_gt(t):
        return reduce_keys(lambda s: _colsum(jnp.where(s > t, 1.0, 0.0)), jnp.add,
                           lambda a: jnp.sum(a, axis=0, keepdims=True))

    n_adm = (N_META + i * tk + (lax.broadcasted_iota(jnp.int32, (1, tq), 1) // CHUNK + 1) * CHUNK).astype(F32)
    c_pos = count_gt(jnp.zeros((1, tq), F32))
    c_nonneg = reduce_keys(lambda s: _colsum(jnp.where(s >= 0.0, 1.0, 0.0)), jnp.add,
                           lambda a: jnp.sum(a, axis=0, keepdims=True))
    zero_vk = jnp.logical_and(c_pos < kf, c_nonneg >= kf)
    pos = c_pos >= kf
    lo0 = jnp.where(pos, 0.0, mn - (1.0 + jnp.abs(mn)))
    hi0 = jnp.where(pos, mx, 0.0)
    clo0 = jnp.where(pos, c_pos, n_adm)

    def open_rows(clo):
        return jnp.max(jnp.where(zero_vk, 0.0, clo)) > kf

    def search_cond(carry):
        it, _, _, _, go = carry
        return jnp.logical_and(it < BISECT_STEPS, go)

    def search_body(carry):
        it, lo, hi, clo, _ = carry
        for _ in range(2):
            mid = 0.5 * lo + 0.5 * hi
            c = count_gt(mid)
            up = c >= kf
            lo = jnp.where(up, mid, lo)
            clo = jnp.where(up, c, clo)
            hi = jnp.where(up, hi, mid)
        return it + 2, lo, hi, clo, open_rows(clo)

    _, lo, _, _, _ = lax.while_loop(search_cond, search_body, (0, lo0, hi0, clo0, open_rows(clo0)))

    thr = reduce_keys(lambda s: _colmin(jnp.where(s > lo, s, -NEG)), jnp.minimum,
                      lambda a: jnp.min(a, axis=0, keepdims=True))
    thr = jnp.where(zero_vk, 0.0, thr)
    need = kf - count_gt(thr)

    def sel_mask(s, tri, carry):
        eq = jnp.where(s == thr, 1.0, 0.0)
        rank = carry + _dot(tri, eq.astype(BF16))
        sel = jnp.logical_or(s > thr, jnp.logical_and(s == thr, rank < need))
        return jnp.where(sel, 0.0, NEG), carry + jnp.sum(_colsum(eq), axis=0, keepdims=True)

    def head_rows(h):
        return slice(h * HEAD_DIM, (h + 1) * HEAD_DIM)

    mskm, carry0 = sel_mask(scm_ref[...], trim_ref[...], jnp.zeros((1, tq), F32))
    km = km_ref[...]
    vTm = vTm_ref[...]
    lgs = [_dot(km, qT_ref[head_rows(h), :]) for h in range(n_heads)]
    ps = []
    for h in range(n_heads):
        lg = lgs[h] + biasm_ref[h] + mskm
        m = jnp.max(lg, axis=0, keepdims=True)
        p = jnp.exp(lg - m)
        m_ref[h:h + 1, :] = m
        l_ref[h:h + 1, :] = jnp.sum(p, axis=0, keepdims=True)
        ps.append(p.astype(BF16))
    for h in range(n_heads):
        acc_ref[head_rows(h), :] = _dot(vTm, ps[h])

    def att_body(kb, carry):
        msk, carry = sel_mask(sc_ref[kblock(kb), :], tri_ref[...], carry)
        which = jnp.clip(kb - i + 2, 0, 2)
        kblk = k_ref[kblock(kb), :]
        vblk = vT_ref[:, kblock(kb)]
        for h in range(n_heads):
            lg_ref[h] = _dot(kblk, qT_ref[head_rows(h), :])
        alphas = []
        for h in range(n_heads):
            lg = lg_ref[h] + bias_ref[h, which] + msk
            m_old = m_ref[h:h + 1, :]
            m_new = jnp.maximum(m_old, jnp.max(_colmax(lg), axis=0, keepdims=True))
            alpha = jnp.exp(m_old - m_new)
            p = jnp.exp(lg - m_new)
            m_ref[h:h + 1, :] = m_new
            l_ref[h:h + 1, :] = alpha * l_ref[h:h + 1, :] + jnp.sum(_colsum(p), axis=0, keepdims=True)
            p_ref[h] = p.astype(BF16)
            alphas.append(alpha)
        for h in range(n_heads):
            acc_ref[head_rows(h), :] = alphas[h] * acc_ref[head_rows(h), :] + _dot(vblk, p_ref[h])
        return carry

    lax.fori_loop(0, nkb, att_body, carry0)

    for h in range(n_heads):
        acc_ref[head_rows(h), :] = acc_ref[head_rows(h), :] / l_ref[h:h + 1, :]
    y_ref[...] = acc_ref[...].T.astype(BF16)


def _rel_bucket_np(rel):
    half = REL_BUCKETS // 2
    max_exact = half // 2
    base = np.where(rel > 0, half, 0)
    n = np.abs(rel)
    nf = np.maximum(n, 1).astype(np.float64)
    large = max_exact + (np.log(nf / max_exact) / math.log(REL_MAX_DIST / max_exact)
                         * (half - max_exact)).astype(np.int32)
    large = np.minimum(large, half - 1)
    return base + np.where(n < max_exact, n, large)


def _bias_of_rel(rb, rel, rel_lo, rel_hi):
    rels = np.arange(rel_lo, rel_hi + 1)
    buckets = _rel_bucket_np(rels)
    col = lambda b: rb[int(b)].reshape((-1,) + (1,) * rel.ndim)
    val = jnp.broadcast_to(col(buckets[0]), (rb.shape[1],) + rel.shape)
    for j in range(1, len(rels)):
        if buckets[j] != buckets[j - 1]:
            val = jnp.where(rel[None] >= int(rels[j]), col(buckets[j]), val)
    return val


def _bias_tables(rel_bias, tq, seq):
    rb = rel_bias.astype(F32)
    far = -2 * tq
    assert np.all(_rel_bucket_np(np.arange(-(seq + N_META), -tq)) == _rel_bucket_np(np.array(far)))
    d = lax.broadcasted_iota(jnp.int32, (tq, tq), 0) - lax.broadcasted_iota(jnp.int32, (tq, tq), 1)
    rel = jnp.stack([jnp.full((tq, tq), far, jnp.int32), d - tq, d])
    bias = _bias_of_rel(rb, rel, far, tq - 1)
    rel_m = (lax.broadcasted_iota(jnp.int32, (N_META, seq), 0)
             - lax.broadcasted_iota(jnp.int32, (N_META, seq), 1) - N_META)
    bias_m = _bias_of_rel(rb, rel_m, -(seq + N_META), -1)
    return bias, bias_m


def _dsa(qidxT, qT, wT, kidx, k, vT, kidx_m, k_m, vT_m, bias, bias_m, nb, seq):
    tq = DSA_TILE
    nq = seq // tq
    n_att = qT.shape[0]
    n_heads = n_att // HEAD_DIM
    tri = jnp.asarray(np.tril(np.ones((tq, tq), np.float32), -1), BF16)
    tri_m = jnp.asarray(np.tril(np.ones((N_META, N_META), np.float32), -1), BF16)
    qcol = lambda b, i: (0, b * nq + i)
    return pl.pallas_call(
        functools.partial(_dsa_kernel, min(TOPK_MAX, seq // 4)),
        out_shape=jax.ShapeDtypeStruct((nb * seq, n_att), BF16),
        grid=(nb, nq),
        in_specs=[
            pl.BlockSpec((qidxT.shape[0], tq), qcol),
            pl.BlockSpec((n_att, tq), qcol),
            pl.BlockSpec((IDX_HEADS, tq), qcol),
            pl.BlockSpec((seq, IDX_DIM), lambda b, i: (b, 0)),
            pl.BlockSpec((seq, HEAD_DIM), lambda b, i: (b, 0)),
            pl.BlockSpec((HEAD_DIM, seq), lambda b, i: (0, b)),
            _const_spec(kidx_m.shape),
            _const_spec(k_m.shape),
            _const_spec(vT_m.shape),
            _const_spec(bias.shape),
            pl.BlockSpec((bias_m.shape[0], N_META, tq), lambda b, i: (0, 0, i)),
            _const_spec(tri.shape),
            _const_spec(tri_m.shape),
        ],
        out_specs=pl.BlockSpec((tq, n_att), lambda b, i: (b * nq + i, 0)),
        scratch_shapes=[
            pltpu.VMEM((seq, tq), F32),
            pltpu.VMEM((N_META, tq), F32),
            pltpu.VMEM((n_heads, tq), F32),
            pltpu.VMEM((n_heads, tq), F32),
            pltpu.VMEM((n_att, tq), F32),
            pltpu.VMEM((n_heads, tq, tq), F32),
            pltpu.VMEM((n_heads, tq, tq), BF16),
        ],
        compiler_params=_cparams(("arbitrary", "arbitrary")),
        name="dsa_mixer",
    )(qidxT, qT, wT, kidx, k, vT, kidx_m, k_m, vT_m, bias, bias_m, tri, tri_m)


def _merge_kernel(h_ref, ya_ref, yb_ref, gpre_ref, gpost_ref, wg_ref, wa_ref, wb_ref, wo_ref, o_ref):
    h = h_ref[...]
    d = h.shape[1]
    hn = _rms(h, gpre_ref[...]).astype(BF16)
    pa = _dot(ya_ref[...], wa_ref[...])
    pb = _dot(yb_ref[...], wb_ref[...])
    g0 = jax.nn.sigmoid(_dot(hn, wg_ref[:, :d]))
    g1 = jax.nn.sigmoid(_dot(hn, wg_ref[:, d:]))
    merged = (g0 * pa + g1 * pb).astype(BF16)
    o_ref[...] = h + _rms(_dot(merged, wo_ref[...]), gpost_ref[...])


def _merge(h, ya, yb, gpre, gpost, wg, wa, wb, wo, nb, seq, tile):
    d = h.shape[1]
    sa = wa.shape[0]
    sb = wb.shape[0]
    nt = seq // tile
    tok = lambda b, i: (b * nt + i, 0)
    return pl.pallas_call(
        _merge_kernel,
        out_shape=jax.ShapeDtypeStruct(h.shape, F32),
        grid=(nb, nt),
        in_specs=[
            pl.BlockSpec((tile, d), tok),
            pl.BlockSpec((tile, sa), tok),
            pl.BlockSpec((tile, sb), tok),
            _const_spec((1, d)),
            _const_spec((1, d)),
            _const_spec(wg.shape),
            _const_spec(wa.shape),
            _const_spec(wb.shape),
            _const_spec(wo.shape),
        ],
        out_specs=pl.BlockSpec((tile, d), tok),
        compiler_params=_cparams(("arbitrary", "arbitrary")),
        name="mixer_merge",
    )(h, ya, yb, gpre, gpost, wg, wa, wb, wo)


def kernel(x, meta_tokens, ff1_norm_pre, ff1_norm_post, mix_norm_pre, mix_norm_post, ff2_norm_pre, ff2_norm_post, ff1_w_gate, ff1_w_up, ff1_w_down, ff2_w_gate, ff2_w_up, ff2_w_down, w_in, ssm_lambda_re, ssm_lambda_im, ssm_log_dt, ssm_b_re, ssm_b_im, ssm_c_re, ssm_c_im, ssm_d, ssm_w_glu, w_branch_a, rel_bias, w_branch_b, w_out):
    nb, seq, d = x.shape
    depth = w_in.shape[0]
    assert depth == 1, "meta-token rows are only carried as keys/state of a single layer"
    assert meta_tokens.shape[0] == N_META
    su = ssm_w_glu.shape[1]
    n_att = w_branch_b.shape[1]
    row = lambda v: v.astype(F32).reshape(1, -1)
    bf = lambda w: w.astype(BF16)

    o_u, o_qi = 0, su
    o_ki = o_qi + IDX_HEADS * IDX_DIM
    o_wi = o_ki + IDX_DIM
    o_q = o_wi + IDX_HEADS
    o_k = o_q + n_att
    o_v = o_k + HEAD_DIM
    o_g = o_v + HEAD_DIM
    win = w_in[0]
    w_nn = bf(jnp.concatenate([win[:, o_u:o_qi], win[:, o_ki:o_wi], win[:, o_k:o_v]], axis=1))
    w_t = bf(jnp.concatenate([win[:, o_qi:o_ki], win[:, o_q:o_k], win[:, o_v:o_g], win[:, o_wi:o_q]], axis=1).T)
    w_gates = bf(win[:, o_g:])

    hx = x.reshape(nb * seq, d)
    pad = PROJ_TILE // 4
    hm = jnp.zeros((pad, d), x.dtype).at[:N_META].set(meta_tokens.astype(x.dtype))

    ffn1 = (row(ff1_norm_pre[0]), row(ff1_norm_post[0]), bf(ff1_w_gate[0]), bf(ff1_w_up[0]), bf(ff1_w_down[0]))
    ffn2 = (row(ff2_norm_pre[0]), row(ff2_norm_post[0]), bf(ff2_w_gate[0]), bf(ff2_w_up[0]), bf(ff2_w_down[0]))

    h1 = _ffn(hx, *ffn1, FFN_TILE)
    h1m = _ffn(hm, *ffn1, pad)

    g_mix = row(mix_norm_pre[0])
    u, kidx, k, qidxT, qT, vT, wT = _proj(h1, g_mix, w_nn, w_t, nb, seq, PROJ_TILE)
    u_m, kidx_m, k_m, _, _, vT_m, _ = _proj(h1m, g_mix, w_nn, w_t, 1, pad, pad)

    bmat, a_re, a_im, cmat, d_skip = _s5_params(
        ssm_lambda_re[0], ssm_lambda_im[0], ssm_log_dt[0], ssm_b_re[0], ssm_b_im[0],
        ssm_c_re[0], ssm_c_im[0], ssm_d[0])
    ya = _s5(u.reshape(nb, seq, su), u_m[:N_META], bmat, a_re, a_im, cmat, d_skip, bf(ssm_w_glu[0]))
    ya = ya.reshape(nb * seq, su)

    bias, bias_m = _bias_tables(rel_bias, DSA_TILE, seq)
    yb = _dsa(qidxT, qT, wT, kidx, k, vT, kidx_m[:N_META], k_m[:N_META], vT_m[:, :N_META],
              bias, bias_m, nb, seq)

    h2 = _merge(h1, ya, yb, g_mix, row(mix_norm_post[0]), w_gates, bf(w_branch_a[0]),
                bf(w_branch_b[0]), bf(w_out[0]), nb, seq, MERGE_TILE)

    out = _ffn(h2, *ffn2, FFN_TILE)
    return out.reshape(nb, seq, d)
```

```python
import functools
import math

import numpy as np
import jax
import jax.numpy as jnp
from jax import lax
from jax.experimental import pallas as pl
from jax.experimental.pallas import tpu as pltpu

F32 = jnp.float32
BF16 = jnp.bfloat16

RMS_EPS = 1e-6
CHUNK = 64
N_META = 16
SSM_GROUP = 16
SSM_STATE = 64
HEAD_DIM = 64
V_ROWS = HEAD_DIM + 16
IDX_DIM = 64
IDX_HEADS = 8
TOPK_MAX = 256
REL_BUCKETS = 32
REL_MAX_DIST = 128
NEG = -1e30
BISECT_STEPS = 64

VMEM_LIMIT = 56 * 1024 * 1024

FFN_TILE = 512
FFN_CHUNK = 256
PROJ_TILE = 512
S5_STEPS = 32
S5_LANES = 512
DSA_TILE = 256
SCAN_ROWS = 64
MERGE_TILE = 512


def _cparams(sem):
    return pltpu.CompilerParams(dimension_semantics=sem, vmem_limit_bytes=VMEM_LIMIT)


def _const_spec(shape):
    nd = len(shape)
    return pl.BlockSpec(shape, lambda *_: (0,) * nd)


def _rms(x, g):
    return x * lax.rsqrt(jnp.mean(x * x, axis=-1, keepdims=True) + RMS_EPS) * g


def _dot(a, b):
    return jnp.dot(a, b, preferred_element_type=F32)


def _dot_nt(a, b):
    return lax.dot_general(a, b, (((1,), (1,)), ((), ())), preferred_element_type=F32)


def _ffn_kernel(x_ref, gpre_ref, gpost_ref, wg_ref, wu_ref, wd_ref, o_ref, acc_ref):
    x = x_ref[...]
    hn = _rms(x, gpre_ref[...]).astype(BF16)
    n_chunks = wg_ref.shape[1] // FFN_CHUNK
    for c in range(n_chunks):
        cols = slice(c * FFN_CHUNK, (c + 1) * FFN_CHUNK)
        g = _dot(hn, wg_ref[:, cols])
        u = _dot(hn, wu_ref[:, cols])
        a = (jax.nn.silu(g) * u).astype(BF16)
        part = _dot(a, wd_ref[cols, :])
        if c == 0:
            acc_ref[...] = part
        else:
            acc_ref[...] += part
    o_ref[...] = x + 0.5 * _rms(acc_ref[...], gpost_ref[...])


def _ffn(x, gpre, gpost, wg, wu, wd, tile):
    t, d = x.shape
    dff = wg.shape[1]
    return pl.pallas_call(
        _ffn_kernel,
        out_shape=jax.ShapeDtypeStruct((t, d), F32),
        grid=(t // tile,),
        in_specs=[
            pl.BlockSpec((tile, d), lambda i: (i, 0)),
            _const_spec((1, d)),
            _const_spec((1, d)),
            _const_spec((d, dff)),
            _const_spec((d, dff)),
            _const_spec((dff, d)),
        ],
        out_specs=pl.BlockSpec((tile, d), lambda i: (i, 0)),
        scratch_shapes=[pltpu.VMEM((tile, d), F32)],
        compiler_params=_cparams(("arbitrary",)),
        name="ffn_half_step",
    )(x, gpre, gpost, wg, wu, wd)


def _proj_kernel(h_ref, g_ref, wn_ref, wt_ref, u_ref, kidx_ref, k_ref, qidxT_ref, qT_ref, vT_ref, wT_ref):
    hn = _rms(h_ref[...], g_ref[...]).astype(BF16)
    nn = _dot(hn, wn_ref[...])
    su = u_ref.shape[1]
    u_ref[...] = nn[:, :su].astype(BF16)
    kidx_ref[...] = nn[:, su:su + IDX_DIM].astype(BF16)
    k_ref[...] = nn[:, su + IDX_DIM:su + IDX_DIM + HEAD_DIM].astype(BF16)
    tt = _dot_nt(wt_ref[...], hn)
    nq = qidxT_ref.shape[0]
    na = qT_ref.shape[0]
    qidxT_ref[...] = (tt[:nq] * (IDX_DIM ** -0.5)).astype(BF16)
    qT_ref[...] = (tt[nq:nq + na] * (HEAD_DIM ** -0.5)).astype(BF16)
    ones = jnp.ones((V_ROWS - HEAD_DIM, tt.shape[1]), F32)
    vT_ref[...] = jnp.concatenate([tt[nq + na:nq + na + HEAD_DIM], ones], axis=0).astype(BF16)
    wT_ref[...] = tt[nq + na + HEAD_DIM:nq + na + HEAD_DIM + IDX_HEADS] * (IDX_HEADS ** -0.5)


def _proj(h, gain, w_nn, w_t, nb, seq, tile):
    d = h.shape[1]
    su = w_nn.shape[1] - IDX_DIM - HEAD_DIM
    nq = IDX_HEADS * IDX_DIM
    na = w_t.shape[0] - nq - HEAD_DIM - IDX_HEADS
    nt = seq // tile
    tok = lambda b, i: (b * nt + i, 0)
    tokT = lambda b, i: (0, b * nt + i)
    out_shape = (
        jax.ShapeDtypeStruct((nb * seq, su), BF16),
        jax.ShapeDtypeStruct((nb * seq, IDX_DIM), BF16),
        jax.ShapeDtypeStruct((nb * seq, HEAD_DIM), BF16),
        jax.ShapeDtypeStruct((nq, nb * seq), BF16),
        jax.ShapeDtypeStruct((na, nb * seq), BF16),
        jax.ShapeDtypeStruct((V_ROWS, nb * seq), BF16),
        jax.ShapeDtypeStruct((IDX_HEADS, nb * seq), F32),
    )
    out_specs = (
        pl.BlockSpec((tile, su), tok),
        pl.BlockSpec((tile, IDX_DIM), tok),
        pl.BlockSpec((tile, HEAD_DIM), tok),
        pl.BlockSpec((nq, tile), tokT),
        pl.BlockSpec((na, tile), tokT),
        pl.BlockSpec((V_ROWS, tile), tokT),
        pl.BlockSpec((IDX_HEADS, tile), tokT),
    )
    return pl.pallas_call(
        _proj_kernel,
        out_shape=out_shape,
        grid=(nb, nt),
        in_specs=[
            pl.BlockSpec((tile, d), tok),
            _const_spec((1, d)),
            _const_spec(w_nn.shape),
            _const_spec(w_t.shape),
        ],
        out_specs=out_specs,
        compiler_params=_cparams(("arbitrary", "arbitrary")),
        name="mixer_in_proj",
    )(h, gain, w_nn, w_t)


def _s5_scan(bu_ref, xs_ref, st_ref, are_ref, aim_ref, chunks, n_steps, nb, bcast_rows):
    n_state = are_ref.shape[1]
    for c in chunks:
        lanes = slice(c * S5_LANES, (c + 1) * S5_LANES)
        lanes_im = slice(n_state + c * S5_LANES, n_state + (c + 1) * S5_LANES)
        a_re = jnp.broadcast_to(are_ref[:, lanes], (nb, S5_LANES))
        a_im = jnp.broadcast_to(aim_ref[:, lanes], (nb, S5_LANES))
        x_re = st_ref[0, :, lanes]
        x_im = st_ref[1, :, lanes]
        for t in range(n_steps):
            rows = slice(t, t + 1) if bcast_rows else slice(t * nb, (t + 1) * nb)
            b_re = bu_ref[rows, lanes]
            b_im = bu_ref[rows, lanes_im]
            x_re, x_im = (a_re * x_re - a_im * x_im + b_re,
                          a_re * x_im + a_im * x_re + b_im)
            if xs_ref is not None:
                xs_ref[rows, lanes] = x_re.astype(BF16)
                xs_ref[rows, lanes_im] = x_im.astype(BF16)
        st_ref[0, :, lanes] = x_re
        st_ref[1, :, lanes] = x_im


def _s5_kernel(u_ref, um_ref, bmat_ref, are_ref, aim_ref, cmat_ref, d_ref, wglu_ref, y_ref,
               bu_ref, xs_ref, st_ref, bum_ref):
    nb, n_steps, su = u_ref.shape
    n_state = are_ref.shape[1]
    half_u = su // 2
    half_s = n_state // 2
    chunks_per_half = half_s // S5_LANES

    def b_project(u, dst_ref):
        for hf in range(2):
            uu = u[:, hf * half_u:(hf + 1) * half_u]
            bu = _dot(uu, bmat_ref[hf])
            dst_ref[:, hf * half_s:(hf + 1) * half_s] = bu[:, :half_s]
            dst_ref[:, n_state + hf * half_s:n_state + (hf + 1) * half_s] = bu[:, half_s:]

    @pl.when(pl.program_id(0) == 0)
    def _():
        st_ref[...] = jnp.zeros_like(st_ref)
        b_project(um_ref[...], bum_ref)
        _s5_scan(bum_ref, None, st_ref, are_ref, aim_ref, range(2 * chunks_per_half), um_ref.shape[0], nb, True)

    u = pltpu.einshape("btc->tbc", u_ref[...]).reshape(n_steps * nb, su)
    b_project(u, bu_ref)
    ys = []
    for hf in range(2):
        _s5_scan(bu_ref, xs_ref, st_ref, are_ref, aim_ref,
                 range(hf * chunks_per_half, (hf + 1) * chunks_per_half), n_steps, nb, False)
        y_re = _dot(xs_ref[:, hf * half_s:(hf + 1) * half_s], cmat_ref[hf, 0])
        y_im = _dot(xs_ref[:, n_state + hf * half_s:n_state + (hf + 1) * half_s], cmat_ref[hf, 1])
        ys.append(y_re - y_im)
    y = jnp.concatenate(ys, axis=1) + d_ref[...] * u.astype(F32)
    y = jax.nn.gelu(y)
    y = y * jax.nn.sigmoid(_dot(y.astype(BF16), wglu_ref[...]))
    y_ref[...] = pltpu.einshape("tbc->btc", y.astype(BF16).reshape(n_steps, nb, su))


def _s5(u, u_meta, bmat, a_re, a_im, cmat, d_skip, w_glu):
    nb, seq, su = u.shape
    rows = S5_STEPS * nb
    n_state = a_re.shape[1]
    return pl.pallas_call(
        _s5_kernel,
        out_shape=jax.ShapeDtypeStruct((nb, seq, su), BF16),
        grid=(seq // S5_STEPS,),
        in_specs=[
            pl.BlockSpec((nb, S5_STEPS, su), lambda i: (0, i, 0)),
            _const_spec(u_meta.shape),
            _const_spec(bmat.shape),
            _const_spec(a_re.shape),
            _const_spec(a_im.shape),
            _const_spec(cmat.shape),
            _const_spec(d_skip.shape),
            _const_spec(w_glu.shape),
        ],
        out_specs=pl.BlockSpec((nb, S5_STEPS, su), lambda i: (0, i, 0)),
        scratch_shapes=[
            pltpu.VMEM((rows, 2 * n_state), F32),
            pltpu.VMEM((rows, 2 * n_state), BF16),
            pltpu.VMEM((2, nb, n_state), F32),
            pltpu.VMEM((u_meta.shape[0], 2 * n_state), F32),
        ],
        compiler_params=_cparams(("arbitrary",)),
        name="s5_mixer",
    )(u, u_meta, bmat, a_re, a_im, cmat, d_skip, w_glu)


def _s5_params(lambda_re, lambda_im, log_dt, b_re, b_im, c_re, c_im, d_skip):
    g, p, m = b_re.shape
    lr = lambda_re.astype(F32)
    li = lambda_im.astype(F32)
    dt = jnp.exp(log_dt.astype(F32))[:, None]
    mag = jnp.exp(lr * dt)
    a_re = mag * jnp.cos(li * dt)
    a_im = mag * jnp.sin(li * dt)
    den = lr * lr + li * li
    num_re = a_re - 1.0
    num_im = a_im
    coef_re = (num_re * lr + num_im * li) / den
    coef_im = (num_im * lr - num_re * li) / den
    br = b_re.astype(F32)
    bi = b_im.astype(F32)
    bbar_re = coef_re[..., None] * br - coef_im[..., None] * bi
    bbar_im = coef_re[..., None] * bi + coef_im[..., None] * br
    gh = g // 2
    eye = jnp.eye(gh, dtype=F32)

    def in_mat(bb):
        bb = bb.reshape(2, gh, p, m)
        return jnp.einsum('hgpm,gk->hgmkp', bb, eye).reshape(2, gh * m, gh * p)

    def out_mat(cc):
        cc = cc.astype(F32).reshape(2, gh, m, p)
        return jnp.einsum('hgmp,gk->hgpkm', cc, eye).reshape(2, gh * p, gh * m)

    bmat = jnp.concatenate([in_mat(bbar_re), in_mat(bbar_im)], axis=2).astype(BF16)
    cmat = jnp.stack([out_mat(c_re), out_mat(c_im)], axis=1).astype(BF16)
    return (bmat, a_re.reshape(1, g * p), a_im.reshape(1, g * p), cmat,
            d_skip.astype(F32).reshape(1, g * m))


def _colsum(x):
    return jnp.sum(x.reshape(x.shape[0] // 8, 8, x.shape[1]), axis=0)


def _colmax(x):
    return jnp.max(x.reshape(x.shape[0] // 8, 8, x.shape[1]), axis=0)


def _colmin(x):
    return jnp.min(x.reshape(x.shape[0] // 8, 8, x.shape[1]), axis=0)


def _dsa_kernel(top_k, qidxT_ref, qT_ref, wT_ref, kidx_ref, k_ref, vT_ref, kidxm_ref, km_ref, vTm_ref,
                bias_ref, biasm_ref, tri_ref, trim_ref, y_ref,
                sc_ref, scm_ref, m_ref, acc_ref, out_ref, lg_ref, p_ref):
    i = pl.program_id(1)
    tq = qT_ref.shape[1]
    tk = tq
    nkb = i + 1
    n_heads = qT_ref.shape[0] // HEAD_DIM
    kf = float(top_k)

    def kblock(kb):
        return pl.ds(pl.multiple_of(kb * tk, tk), tk)

    kc = lax.broadcasted_iota(jnp.int32, (tk, tq), 0) // CHUNK
    qc = lax.broadcasted_iota(jnp.int32, (tk, tq), 1) // CHUNK
    adm_diag = kc <= qc

    def idx_score(kidx):
        s = jnp.zeros((kidx.shape[0], tq), F32)
        for h in range(IDX_HEADS):
            sh = _dot(kidx, qidxT_ref[h * IDX_DIM:(h + 1) * IDX_DIM, :])
            s = s + jnp.maximum(sh, 0.0) * wT_ref[h:h + 1, :]
        return s

    sm = idx_score(kidxm_ref[...])
    scm_ref[...] = sm

    def score_body(kb, carry):
        mn, mx = carry
        s = idx_score(kidx_ref[kblock(kb), :])
        adm = jnp.logical_or(kb < i, adm_diag)
        sc_ref[kblock(kb), :] = jnp.where(adm, s, NEG)
        mn = jnp.minimum(mn, _colmin(jnp.where(adm, s, -NEG)))
        mx = jnp.maximum(mx, _colmax(jnp.where(adm, s, NEG)))
        return mn, mx

    mn, mx = lax.fori_loop(0, nkb, score_body, (_colmin(sm), _colmax(sm)))
    mn = jnp.min(mn, axis=0, keepdims=True)
    mx = jnp.max(mx, axis=0, keepdims=True)

    def reduce_keys(fn, combine, finish):
        def body(kb, acc):
            base = pl.multiple_of(kb * tk, tk)
            for r in range(0, tk, SCAN_ROWS):
                acc = combine(acc, fn(sc_ref[pl.ds(base + r, SCAN_ROWS), :]))
            return acc

        return finish(lax.fori_loop(0, nkb, body, fn(scm_ref[...])))

    def count_gt(t):
        return reduce_keys(lambda s: _colsum(jnp.where(s > t, 1.0, 0.0)), jnp.add,
                           lambda a: jnp.sum(a, axis=0, keepdims=True))

    n_adm = (N_META + i * tk + (lax.broadcasted_iota(jnp.int32, (1, tq), 1) // CHUNK + 1) * CHUNK).astype(F32)
    c_pos = count_gt(jnp.zeros((1, tq), F32))
    c_nonneg = reduce_keys(lambda s: _colsum(jnp.where(s >= 0.0, 1.0, 0.0)), jnp.add,
                           lambda a: jnp.sum(a, axis=0, keepdims=True))
    zero_vk = jnp.logical_and(c_pos < kf, c_nonneg >= kf)
    pos = c_pos >= kf
    lo0 = jnp.where(pos, 0.0, mn - (1.0 + jnp.abs(mn)))
    hi0 = jnp.where(pos, mx, 0.0)
    clo0 = jnp.where(pos, c_pos, n_adm)

    def open_rows(clo):
        return jnp.max(jnp.where(zero_vk, 0.0, clo)) > kf

    def search_cond(carry):
        it, _, _, _, go = carry
        return jnp.logical_and(it < BISECT_STEPS, go)

    def search_body(carry):
        it, lo, hi, clo, _ = carry
        for _ in range(2):
            mid = 0.5 * lo + 0.5 * hi
            c = count_gt(mid)
            up = c >= kf
            lo = jnp.where(up, mid, lo)
            clo = jnp.where(up, c, clo)
            hi = jnp.where(up, hi, mid)
        return it + 2, lo, hi, clo, open_rows(clo)

    _, lo, _, _, _ = lax.while_loop(search_cond, search_body, (0, lo0, hi0, clo0, open_rows(clo0)))

    thr = reduce_keys(lambda s: _colmin(jnp.where(s > lo, s, -NEG)), jnp.minimum,
                      lambda a: jnp.min(a, axis=0, keepdims=True))
    thr = jnp.where(zero_vk, 0.0, thr)
    need = kf - count_gt(thr)

    def sel_mask(s, tri, carry):
        eq = jnp.where(s == thr, 1.0, 0.0)
        rank = carry + _dot(tri, eq.astype(BF16))
        sel = jnp.logical_or(s > thr, jnp.logical_and(s == thr, rank < need))
        return (jnp.where(sel, 0.0, NEG).astype(BF16),
                carry + jnp.sum(_colsum(eq), axis=0, keepdims=True))

    v_rows = vT_ref.shape[0]

    def head_rows(h):
        return slice(h * HEAD_DIM, (h + 1) * HEAD_DIM)

    def acc_rows(h):
        return slice(h * v_rows, (h + 1) * v_rows)

    def colmax16(t):
        t = jnp.max(t.reshape(t.shape[0] // 16, 16, t.shape[1]), axis=0)
        return jnp.max(t, axis=0, keepdims=True).astype(F32)

    mskm, carry0 = sel_mask(scm_ref[...], trim_ref[...], jnp.zeros((1, tq), F32))
    km = km_ref[...]
    vTm = vTm_ref[...]
    lgs = [_dot(km, qT_ref[head_rows(h), :]).astype(BF16) for h in range(n_heads)]
    ps = []
    for h in range(n_heads):
        t = lgs[h] + biasm_ref[h] + mskm
        m = colmax16(t)
        m_ref[h:h + 1, :] = m
        ps.append(jnp.exp(t - m.astype(BF16)))
    for h in range(n_heads):
        acc_ref[acc_rows(h), :] = _dot(vTm, ps[h])

    def att_body(kb, carry):
        msk, carry = sel_mask(sc_ref[kblock(kb), :], tri_ref[...], carry)
        which = jnp.clip(kb - i + 2, 0, 2)
        kblk = k_ref[kblock(kb), :]
        vblk = vT_ref[:, kblock(kb)]
        for h in range(n_heads):
            lg_ref[h] = _dot(kblk, qT_ref[head_rows(h), :]).astype(BF16)
        alphas = []
        for h in range(n_heads):
            t = lg_ref[h] + bias_ref[h, which] + msk
            m_old = m_ref[h:h + 1, :]
            m_new = jnp.maximum(m_old, colmax16(t))
            m_ref[h:h + 1, :] = m_new
            p_ref[h] = jnp.exp(t - m_new.astype(BF16))
            alphas.append(jnp.exp(m_old - m_new))
        for h in range(n_heads):
            acc_ref[acc_rows(h), :] = alphas[h] * acc_ref[acc_rows(h), :] + _dot(vblk, p_ref[h])
        return carry

    lax.fori_loop(0, nkb, att_body, carry0)

    for h in range(n_heads):
        a = acc_ref[acc_rows(h), :]
        out_ref[head_rows(h), :] = a[:HEAD_DIM] / a[HEAD_DIM:HEAD_DIM + 1]
    y_ref[...] = out_ref[...].T.astype(BF16)


def _rel_bucket_np(rel):
    half = REL_BUCKETS // 2
    max_exact = half // 2
    base = np.where(rel > 0, half, 0)
    n = np.abs(rel)
    nf = np.maximum(n, 1).astype(np.float64)
    large = max_exact + (np.log(nf / max_exact) / math.log(REL_MAX_DIST / max_exact)
                         * (half - max_exact)).astype(np.int32)
    large = np.minimum(large, half - 1)
    return base + np.where(n < max_exact, n, large)


def _bias_of_rel(rb, rel, rel_lo, rel_hi):
    rels = np.arange(rel_lo, rel_hi + 1)
    buckets = _rel_bucket_np(rels)
    col = lambda b: rb[int(b)].reshape((-1,) + (1,) * rel.ndim)
    val = jnp.broadcast_to(col(buckets[0]), (rb.shape[1],) + rel.shape)
    for j in range(1, len(rels)):
        if buckets[j] != buckets[j - 1]:
            val = jnp.where(rel[None] >= int(rels[j]), col(buckets[j]), val)
    return val


def _bias_tables(rel_bias, tq, seq):
    rb = rel_bias.astype(F32)
    far = -2 * tq
    assert np.all(_rel_bucket_np(np.arange(-(seq + N_META), -tq)) == _rel_bucket_np(np.array(far)))
    d = lax.broadcasted_iota(jnp.int32, (tq, tq), 0) - lax.broadcasted_iota(jnp.int32, (tq, tq), 1)
    rel = jnp.stack([jnp.full((tq, tq), far, jnp.int32), d - tq, d])
    bias = _bias_of_rel(rb, rel, far, tq - 1)
    rel_m = (lax.broadcasted_iota(jnp.int32, (N_META, seq), 0)
             - lax.broadcasted_iota(jnp.int32, (N_META, seq), 1) - N_META)
    bias_m = _bias_of_rel(rb, rel_m, -(seq + N_META), -1)
    return bias.astype(BF16), bias_m.astype(BF16)


def _dsa(qidxT, qT, wT, kidx, k, vT, kidx_m, k_m, vT_m, bias, bias_m, nb, seq):
    tq = DSA_TILE
    nq = seq // tq
    n_att = qT.shape[0]
    n_heads = n_att // HEAD_DIM
    tri = jnp.asarray(np.tril(np.ones((tq, tq), np.float32), -1), BF16)
    tri_m = jnp.asarray(np.tril(np.ones((N_META, N_META), np.float32), -1), BF16)
    qcol = lambda b, i: (0, b * nq + i)
    return pl.pallas_call(
        functools.partial(_dsa_kernel, min(TOPK_MAX, seq // 4)),
        out_shape=jax.ShapeDtypeStruct((nb * seq, n_att), BF16),
        grid=(nb, nq),
        in_specs=[
            pl.BlockSpec((qidxT.shape[0], tq), qcol),
            pl.BlockSpec((n_att, tq), qcol),
            pl.BlockSpec((IDX_HEADS, tq), qcol),
            pl.BlockSpec((seq, IDX_DIM), lambda b, i: (b, 0)),
            pl.BlockSpec((seq, HEAD_DIM), lambda b, i: (b, 0)),
            pl.BlockSpec((V_ROWS, seq), lambda b, i: (0, b)),
            _const_spec(kidx_m.shape),
            _const_spec(k_m.shape),
            _const_spec(vT_m.shape),
            _const_spec(bias.shape),
            pl.BlockSpec((bias_m.shape[0], N_META, tq), lambda b, i: (0, 0, i)),
            _const_spec(tri.shape),
            _const_spec(tri_m.shape),
        ],
        out_specs=pl.BlockSpec((tq, n_att), lambda b, i: (b * nq + i, 0)),
        scratch_shapes=[
            pltpu.VMEM((seq, tq), F32),
            pltpu.VMEM((N_META, tq), F32),
            pltpu.VMEM((n_heads, tq), F32),
            pltpu.VMEM((n_heads * V_ROWS, tq), F32),
            pltpu.VMEM((n_att, tq), F32),
            pltpu.VMEM((n_heads, tq, tq), BF16),
            pltpu.VMEM((n_heads, tq, tq), BF16),
        ],
        compiler_params=_cparams(("arbitrary", "arbitrary")),
        name="dsa_mixer",
    )(qidxT, qT, wT, kidx, k, vT, kidx_m, k_m, vT_m, bias, bias_m, tri, tri_m)


def _merge_kernel(h_ref, ya_ref, yb_ref, gpre_ref, gpost_ref, wg_ref, wa_ref, wb_ref, wo_ref, o_ref):
    h = h_ref[...]
    d = h.shape[1]
    hn = _rms(h, gpre_ref[...]).astype(BF16)
    pa = _dot(ya_ref[...], wa_ref[...])
    pb = _dot(yb_ref[...], wb_ref[...])
    g0 = jax.nn.sigmoid(_dot(hn, wg_ref[:, :d]))
    g1 = jax.nn.sigmoid(_dot(hn, wg_ref[:, d:]))
    merged = (g0 * pa + g1 * pb).astype(BF16)
    o_ref[...] = h + _rms(_dot(merged, wo_ref[...]), gpost_ref[...])


def _merge(h, ya, yb, gpre, gpost, wg, wa, wb, wo, nb, seq, tile):
    d = h.shape[1]
    sa = wa.shape[0]
    sb = wb.shape[0]
    nt = seq // tile
    tok = lambda b, i: (b * nt + i, 0)
    return pl.pallas_call(
        _merge_kernel,
        out_shape=jax.ShapeDtypeStruct(h.shape, F32),
        grid=(nb, nt),
        in_specs=[
            pl.BlockSpec((tile, d), tok),
            pl.BlockSpec((tile, sa), tok),
            pl.BlockSpec((tile, sb), tok),
            _const_spec((1, d)),
            _const_spec((1, d)),
            _const_spec(wg.shape),
            _const_spec(wa.shape),
            _const_spec(wb.shape),
            _const_spec(wo.shape),
        ],
        out_specs=pl.BlockSpec((tile, d), tok),
        compiler_params=_cparams(("arbitrary", "arbitrary")),
        name="mixer_merge",
    )(h, ya, yb, gpre, gpost, wg, wa, wb, wo)


def kernel(x, meta_tokens, ff1_norm_pre, ff1_norm_post, mix_norm_pre, mix_norm_post, ff2_norm_pre, ff2_norm_post, ff1_w_gate, ff1_w_up, ff1_w_down, ff2_w_gate, ff2_w_up, ff2_w_down, w_in, ssm_lambda_re, ssm_lambda_im, ssm_log_dt, ssm_b_re, ssm_b_im, ssm_c_re, ssm_c_im, ssm_d, ssm_w_glu, w_branch_a, rel_bias, w_branch_b, w_out):
    nb, seq, d = x.shape
    depth = w_in.shape[0]
    assert depth == 1, "meta-token rows are only carried as keys/state of a single layer"
    assert meta_tokens.shape[0] == N_META
    su = ssm_w_glu.shape[1]
    n_att = w_branch_b.shape[1]
    row = lambda v: v.astype(F32).reshape(1, -1)
    bf = lambda w: w.astype(BF16)

    o_u, o_qi = 0, su
    o_ki = o_qi + IDX_HEADS * IDX_DIM
    o_wi = o_ki + IDX_DIM
    o_q = o_wi + IDX_HEADS
    o_k = o_q + n_att
    o_v = o_k + HEAD_DIM
    o_g = o_v + HEAD_DIM
    win = w_in[0]
    w_nn = bf(jnp.concatenate([win[:, o_u:o_qi], win[:, o_ki:o_wi], win[:, o_k:o_v]], axis=1))
    w_t = bf(jnp.concatenate([win[:, o_qi:o_ki], win[:, o_q:o_k], win[:, o_v:o_g], win[:, o_wi:o_q]], axis=1).T)
    w_gates = bf(win[:, o_g:])

    hx = x.reshape(nb * seq, d)
    pad = PROJ_TILE // 4
    hm = jnp.zeros((pad, d), x.dtype).at[:N_META].set(meta_tokens.astype(x.dtype))

    ffn1 = (row(ff1_norm_pre[0]), row(ff1_norm_post[0]), bf(ff1_w_gate[0]), bf(ff1_w_up[0]), bf(ff1_w_down[0]))
    ffn2 = (row(ff2_norm_pre[0]), row(ff2_norm_post[0]), bf(ff2_w_gate[0]), bf(ff2_w_up[0]), bf(ff2_w_down[0]))

    h1 = _ffn(hx, *ffn1, FFN_TILE)
    h1m = _ffn(hm, *ffn1, pad)

    g_mix = row(mix_norm_pre[0])
    u, kidx, k, qidxT, qT, vT, wT = _proj(h1, g_mix, w_nn, w_t, nb, seq, PROJ_TILE)
    u_m, kidx_m, k_m, _, _, vT_m, _ = _proj(h1m, g_mix, w_nn, w_t, 1, pad, pad)

    bmat, a_re, a_im, cmat, d_skip = _s5_params(
        ssm_lambda_re[0], ssm_lambda_im[0], ssm_log_dt[0], ssm_b_re[0], ssm_b_im[0],
        ssm_c_re[0], ssm_c_im[0], ssm_d[0])
    ya = _s5(u.reshape(nb, seq, su), u_m[:N_META], bmat, a_re, a_im, cmat, d_skip, bf(ssm_w_glu[0]))
    ya = ya.reshape(nb * seq, su)

    bias, bias_m = _bias_tables(rel_bias, DSA_TILE, seq)
    yb = _dsa(qidxT, qT, wT, kidx, k, vT, kidx_m[:N_META], k_m[:N_META], vT_m[:, :N_META],
              bias, bias_m, nb, seq)

    h2 = _merge(h1, ya, yb, g_mix, row(mix_norm_post[0]), w_gates, bf(w_branch_a[0]),
                bf(w_branch_b[0]), bf(w_out[0]), nb, seq, MERGE_TILE)

    out = _ffn(h2, *ffn2, FFN_TILE)
    return out.reshape(nb, seq, d)
```

```python
import functools
import math

import numpy as np
import jax
import jax.numpy as jnp
from jax import lax
from jax.experimental import pallas as pl
from jax.experimental.pallas import tpu as pltpu

F32 = jnp.float32
BF16 = jnp.bfloat16

RMS_EPS = 1e-6
CHUNK = 64
N_META = 16
SSM_GROUP = 16
SSM_STATE = 64
HEAD_DIM = 64
V_ROWS = HEAD_DIM + 16
IDX_DIM = 64
IDX_HEADS = 8
TOPK_MAX = 256
REL_BUCKETS = 32
REL_MAX_DIST = 128
NEG = -1e30
BISECT_STEPS = 64
SEARCH_WARMUP = 12

VMEM_LIMIT = 56 * 1024 * 1024

FFN_TILE = 512
FFN_CHUNK = 256
PROJ_TILE = 512
S5_STEPS = 32
S5_LANES = 512
DSA_TILE = 256
SCAN_ROWS = 64
MERGE_TILE = 512


def _cparams(sem):
    return pltpu.CompilerParams(dimension_semantics=sem, vmem_limit_bytes=VMEM_LIMIT)


def _const_spec(shape):
    nd = len(shape)
    return pl.BlockSpec(shape, lambda *_: (0,) * nd)


def _rms(x, g):
    return x * lax.rsqrt(jnp.mean(x * x, axis=-1, keepdims=True) + RMS_EPS) * g


def _dot(a, b):
    return jnp.dot(a, b, preferred_element_type=F32)


def _dot_nt(a, b):
    return lax.dot_general(a, b, (((1,), (1,)), ((), ())), preferred_element_type=F32)


def _ffn_kernel(x_ref, gpre_ref, gpost_ref, wg_ref, wu_ref, wd_ref, o_ref, acc_ref):
    x = x_ref[...]
    hn = _rms(x, gpre_ref[...]).astype(BF16)
    n_chunks = wg_ref.shape[1] // FFN_CHUNK
    for c in range(n_chunks):
        cols = slice(c * FFN_CHUNK, (c + 1) * FFN_CHUNK)
        g = _dot(hn, wg_ref[:, cols])
        u = _dot(hn, wu_ref[:, cols])
        a = (jax.nn.silu(g) * u).astype(BF16)
        part = _dot(a, wd_ref[cols, :])
        if c == 0:
            acc_ref[...] = part
        else:
            acc_ref[...] += part
    o_ref[...] = x + 0.5 * _rms(acc_ref[...], gpost_ref[...])


def _ffn(x, gpre, gpost, wg, wu, wd, tile):
    t, d = x.shape
    dff = wg.shape[1]
    return pl.pallas_call(
        _ffn_kernel,
        out_shape=jax.ShapeDtypeStruct((t, d), F32),
        grid=(t // tile,),
        in_specs=[
            pl.BlockSpec((tile, d), lambda i: (i, 0)),
            _const_spec((1, d)),
            _const_spec((1, d)),
            _const_spec((d, dff)),
            _const_spec((d, dff)),
            _const_spec((dff, d)),
        ],
        out_specs=pl.BlockSpec((tile, d), lambda i: (i, 0)),
        scratch_shapes=[pltpu.VMEM((tile, d), F32)],
        compiler_params=_cparams(("arbitrary",)),
        name="ffn_half_step",
    )(x, gpre, gpost, wg, wu, wd)


def _proj_kernel(h_ref, g_ref, wn_ref, wt_ref, u_ref, kidx_ref, k_ref, qidxT_ref, qT_ref, vT_ref, wT_ref):
    hn = _rms(h_ref[...], g_ref[...]).astype(BF16)
    nn = _dot(hn, wn_ref[...])
    su = u_ref.shape[1]
    u_ref[...] = nn[:, :su].astype(BF16)
    kidx_ref[...] = nn[:, su:su + IDX_DIM].astype(BF16)
    k_ref[...] = nn[:, su + IDX_DIM:su + IDX_DIM + HEAD_DIM].astype(BF16)
    tt = _dot_nt(wt_ref[...], hn)
    nq = qidxT_ref.shape[0]
    na = qT_ref.shape[0]
    qidxT_ref[...] = (tt[:nq] * (IDX_DIM ** -0.5)).astype(BF16)
    qT_ref[...] = (tt[nq:nq + na] * (HEAD_DIM ** -0.5)).astype(BF16)
    ones = jnp.ones((V_ROWS - HEAD_DIM, tt.shape[1]), F32)
    vT_ref[...] = jnp.concatenate([tt[nq + na:nq + na + HEAD_DIM], ones], axis=0).astype(BF16)
    wT_ref[...] = tt[nq + na + HEAD_DIM:nq + na + HEAD_DIM + IDX_HEADS] * (IDX_HEADS ** -0.5)


def _proj(h, gain, w_nn, w_t, nb, seq, tile):
    d = h.shape[1]
    su = w_nn.shape[1] - IDX_DIM - HEAD_DIM
    nq = IDX_HEADS * IDX_DIM
    na = w_t.shape[0] - nq - HEAD_DIM - IDX_HEADS
    nt = seq // tile
    tok = lambda b, i: (b * nt + i, 0)
    tokT = lambda b, i: (0, b * nt + i)
    out_shape = (
        jax.ShapeDtypeStruct((nb * seq, su), BF16),
        jax.ShapeDtypeStruct((nb * seq, IDX_DIM), BF16),
        jax.ShapeDtypeStruct((nb * seq, HEAD_DIM), BF16),
        jax.ShapeDtypeStruct((nq, nb * seq), BF16),
        jax.ShapeDtypeStruct((na, nb * seq), BF16),
        jax.ShapeDtypeStruct((V_ROWS, nb * seq), BF16),
        jax.ShapeDtypeStruct((IDX_HEADS, nb * seq), F32),
    )
    out_specs = (
        pl.BlockSpec((tile, su), tok),
        pl.BlockSpec((tile, IDX_DIM), tok),
        pl.BlockSpec((tile, HEAD_DIM), tok),
        pl.BlockSpec((nq, tile), tokT),
        pl.BlockSpec((na, tile), tokT),
        pl.BlockSpec((V_ROWS, tile), tokT),
        pl.BlockSpec((IDX_HEADS, tile), tokT),
    )
    return pl.pallas_call(
        _proj_kernel,
        out_shape=out_shape,
        grid=(nb, nt),
        in_specs=[
            pl.BlockSpec((tile, d), tok),
            _const_spec((1, d)),
            _const_spec(w_nn.shape),
            _const_spec(w_t.shape),
        ],
        out_specs=out_specs,
        compiler_params=_cparams(("arbitrary", "arbitrary")),
        name="mixer_in_proj",
    )(h, gain, w_nn, w_t)


def _s5_scan(bu_ref, xs_ref, st_ref, are_ref, aim_ref, chunks, n_steps, nb, bcast_rows):
    n_state = are_ref.shape[1]
    for c in chunks:
        lanes = slice(c * S5_LANES, (c + 1) * S5_LANES)
        lanes_im = slice(n_state + c * S5_LANES, n_state + (c + 1) * S5_LANES)
        a_re = jnp.broadcast_to(are_ref[:, lanes], (nb, S5_LANES))
        a_im = jnp.broadcast_to(aim_ref[:, lanes], (nb, S5_LANES))
        x_re = st_ref[0, :, lanes]
        x_im = st_ref[1, :, lanes]
        for t in range(n_steps):
            rows = slice(t, t + 1) if bcast_rows else slice(t * nb, (t + 1) * nb)
            b_re = bu_ref[rows, lanes]
            b_im = bu_ref[rows, lanes_im]
            x_re, x_im = (a_re * x_re - a_im * x_im + b_re,
                          a_re * x_im + a_im * x_re + b_im)
            if xs_ref is not None:
                xs_ref[rows, lanes] = x_re.astype(BF16)
                xs_ref[rows, lanes_im] = x_im.astype(BF16)
        st_ref[0, :, lanes] = x_re
        st_ref[1, :, lanes] = x_im


def _s5_kernel(u_ref, um_ref, bmat_ref, are_ref, aim_ref, cmat_ref, d_ref, wglu_ref, y_ref,
               bu_ref, xs_ref, st_ref, bum_ref):
    nb, n_steps, su = u_ref.shape
    n_state = are_ref.shape[1]
    half_u = su // 2
    half_s = n_state // 2
    chunks_per_half = half_s // S5_LANES

    def b_project(u, dst_ref):
        for hf in range(2):
            uu = u[:, hf * half_u:(hf + 1) * half_u]
            bu = _dot(uu, bmat_ref[hf])
            dst_ref[:, hf * half_s:(hf + 1) * half_s] = bu[:, :half_s]
            dst_ref[:, n_state + hf * half_s:n_state + (hf + 1) * half_s] = bu[:, half_s:]

    @pl.when(pl.program_id(0) == 0)
    def _():
        st_ref[...] = jnp.zeros_like(st_ref)
        b_project(um_ref[...], bum_ref)
        _s5_scan(bum_ref, None, st_ref, are_ref, aim_ref, range(2 * chunks_per_half), um_ref.shape[0], nb, True)

    u = pltpu.einshape("btc->tbc", u_ref[...]).reshape(n_steps * nb, su)
    b_project(u, bu_ref)
    ys = []
    for hf in range(2):
        _s5_scan(bu_ref, xs_ref, st_ref, are_ref, aim_ref,
                 range(hf * chunks_per_half, (hf + 1) * chunks_per_half), n_steps, nb, False)
        y_re = _dot(xs_ref[:, hf * half_s:(hf + 1) * half_s], cmat_ref[hf, 0])
        y_im = _dot(xs_ref[:, n_state + hf * half_s:n_state + (hf + 1) * half_s], cmat_ref[hf, 1])
        ys.append(y_re - y_im)
    y = jnp.concatenate(ys, axis=1) + d_ref[...] * u.astype(F32)
    y = jax.nn.gelu(y)
    y = y * jax.nn.sigmoid(_dot(y.astype(BF16), wglu_ref[...]))
    y_ref[...] = pltpu.einshape("tbc->btc", y.astype(BF16).reshape(n_steps, nb, su))


def _s5(u, u_meta, bmat, a_re, a_im, cmat, d_skip, w_glu):
    nb, seq, su = u.shape
    rows = S5_STEPS * nb
    n_state = a_re.shape[1]
    return pl.pallas_call(
        _s5_kernel,
        out_shape=jax.ShapeDtypeStruct((nb, seq, su), BF16),
        grid=(seq // S5_STEPS,),
        in_specs=[
            pl.BlockSpec((nb, S5_STEPS, su), lambda i: (0, i, 0)),
            _const_spec(u_meta.shape),
            _const_spec(bmat.shape),
            _const_spec(a_re.shape),
            _const_spec(a_im.shape),
            _const_spec(cmat.shape),
            _const_spec(d_skip.shape),
            _const_spec(w_glu.shape),
        ],
        out_specs=pl.BlockSpec((nb, S5_STEPS, su), lambda i: (0, i, 0)),
        scratch_shapes=[
            pltpu.VMEM((rows, 2 * n_state), F32),
            pltpu.VMEM((rows, 2 * n_state), BF16),
            pltpu.VMEM((2, nb, n_state), F32),
            pltpu.VMEM((u_meta.shape[0], 2 * n_state), F32),
        ],
        compiler_params=_cparams(("arbitrary",)),
        name="s5_mixer",
    )(u, u_meta, bmat, a_re, a_im, cmat, d_skip, w_glu)


def _s5_params(lambda_re, lambda_im, log_dt, b_re, b_im, c_re, c_im, d_skip):
    g, p, m = b_re.shape
    lr = lambda_re.astype(F32)
    li = lambda_im.astype(F32)
    dt = jnp.exp(log_dt.astype(F32))[:, None]
    mag = jnp.exp(lr * dt)
    a_re = mag * jnp.cos(li * dt)
    a_im = mag * jnp.sin(li * dt)
    den = lr * lr + li * li
    num_re = a_re - 1.0
    num_im = a_im
    coef_re = (num_re * lr + num_im * li) / den
    coef_im = (num_im * lr - num_re * li) / den
    br = b_re.astype(F32)
    bi = b_im.astype(F32)
    bbar_re = coef_re[..., None] * br - coef_im[..., None] * bi
    bbar_im = coef_re[..., None] * bi + coef_im[..., None] * br
    gh = g // 2
    eye = jnp.eye(gh, dtype=F32)

    def in_mat(bb):
        bb = bb.reshape(2, gh, p, m)
        return jnp.einsum('hgpm,gk->hgmkp', bb, eye).reshape(2, gh * m, gh * p)

    def out_mat(cc):
        cc = cc.astype(F32).reshape(2, gh, m, p)
        return jnp.einsum('hgmp,gk->hgpkm', cc, eye).reshape(2, gh * p, gh * m)

    bmat = jnp.concatenate([in_mat(bbar_re), in_mat(bbar_im)], axis=2).astype(BF16)
    cmat = jnp.stack([out_mat(c_re), out_mat(c_im)], axis=1).astype(BF16)
    return (bmat, a_re.reshape(1, g * p), a_im.reshape(1, g * p), cmat,
            d_skip.astype(F32).reshape(1, g * m))


def _colsum(x):
    return jnp.sum(x.reshape(x.shape[0] // 8, 8, x.shape[1]), axis=0)


def _colmax(x):
    return jnp.max(x.reshape(x.shape[0] // 8, 8, x.shape[1]), axis=0)


def _colmin(x):
    return jnp.min(x.reshape(x.shape[0] // 8, 8, x.shape[1]), axis=0)


def _dsa_kernel(top_k, qidxT_ref, qT_ref, wT_ref, kidx_ref, k_ref, vT_ref, kidxm_ref, km_ref, vTm_ref,
                bias_ref, biasm_ref, tri_ref, trim_ref, y_ref,
                sc_ref, scm_ref, m_ref, acc_ref, out_ref, lg_ref, p_ref):
    i = pl.program_id(1)
    tq = qT_ref.shape[1]
    tk = tq
    nkb = i + 1
    n_heads = qT_ref.shape[0] // HEAD_DIM
    kf = float(top_k)

    def kblock(kb):
        return pl.ds(pl.multiple_of(kb * tk, tk), tk)

    kc = lax.broadcasted_iota(jnp.int32, (tk, tq), 0) // CHUNK
    qc = lax.broadcasted_iota(jnp.int32, (tk, tq), 1) // CHUNK
    adm_diag = kc <= qc

    def idx_score(kidx):
        s = jnp.zeros((kidx.shape[0], tq), F32)
        for h in range(IDX_HEADS):
            sh = _dot(kidx, qidxT_ref[h * IDX_DIM:(h + 1) * IDX_DIM, :])
            s = s + jnp.maximum(sh, 0.0) * wT_ref[h:h + 1, :]
        return s

    sm = idx_score(kidxm_ref[...])
    scm_ref[...] = sm

    def score_body(kb, carry):
        mn, mx = carry
        s = idx_score(kidx_ref[kblock(kb), :])
        adm = jnp.logical_or(kb < i, adm_diag)
        sc_ref[kblock(kb), :] = jnp.where(adm, s, NEG)
        mn = jnp.minimum(mn, _colmin(jnp.where(adm, s, -NEG)))
        mx = jnp.maximum(mx, _colmax(jnp.where(adm, s, NEG)))
        return mn, mx

    mn, mx = lax.fori_loop(0, nkb, score_body, (_colmin(sm), _colmax(sm)))
    mn = jnp.min(mn, axis=0, keepdims=True)
    mx = jnp.max(mx, axis=0, keepdims=True)

    def reduce_keys(fn, combine, finish):
        def body(kb, acc):
            base = pl.multiple_of(kb * tk, tk)
            for r in range(0, tk, SCAN_ROWS):
                acc = combine(acc, fn(sc_ref[pl.ds(base + r, SCAN_ROWS), :]))
            return acc

        return finish(lax.fori_loop(0, nkb, body, fn(scm_ref[...])))

    def count_gt(t):
        return reduce_keys(lambda s: _colsum(jnp.where(s > t, 1.0, 0.0)), jnp.add,
                           lambda a: jnp.sum(a, axis=0, keepdims=True))

    n_adm = (N_META + i * tk + (lax.broadcasted_iota(jnp.int32, (1, tq), 1) // CHUNK + 1) * CHUNK).astype(F32)
    c_pos, c_nonneg = reduce_keys(
        lambda s: (_colsum(jnp.where(s > 0.0, 1.0, 0.0)), _colsum(jnp.where(s >= 0.0, 1.0, 0.0))),
        lambda a, b: (a[0] + b[0], a[1] + b[1]),
        lambda a: (jnp.sum(a[0], axis=0, keepdims=True), jnp.sum(a[1], axis=0, keepdims=True)))
    zero_vk = jnp.logical_and(c_pos < kf, c_nonneg >= kf)
    pos = c_pos >= kf
    lo0 = jnp.where(pos, 0.0, mn - (1.0 + jnp.abs(mn)))
    hi0 = jnp.where(pos, mx, 0.0)
    clo0 = jnp.where(pos, c_pos, n_adm)

    def open_rows(clo):
        return jnp.max(jnp.where(zero_vk, 0.0, clo)) > kf

    def search_cond(carry):
        it, _, _, _, go = carry
        return jnp.logical_and(it < BISECT_STEPS, go)

    def halve(carry):
        lo, hi, clo = carry
        mid = 0.5 * lo + 0.5 * hi
        c = count_gt(mid)
        up = c >= kf
        return jnp.where(up, mid, lo), jnp.where(up, hi, mid), jnp.where(up, c, clo)

    def search_body(carry):
        it, lo, hi, clo, _ = carry
        lo, hi, clo = halve(halve((lo, hi, clo)))
        return it + 2, lo, hi, clo, open_rows(clo)

    lo, hi, clo = lax.fori_loop(0, SEARCH_WARMUP, lambda _, c: halve(c), (lo0, hi0, clo0))
    _, lo, _, _, _ = lax.while_loop(search_cond, search_body, (SEARCH_WARMUP, lo, hi, clo, open_rows(clo)))

    thr = reduce_keys(lambda s: _colmin(jnp.where(s > lo, s, -NEG)), jnp.minimum,
                      lambda a: jnp.min(a, axis=0, keepdims=True))
    thr = jnp.where(zero_vk, 0.0, thr)
    need = kf - count_gt(thr)

    def sel_mask(s, tri, carry):
        eq = jnp.where(s == thr, 1.0, 0.0)
        rank = carry + _dot(tri, eq.astype(BF16))
        sel = jnp.logical_or(s > thr, jnp.logical_and(s == thr, rank < need))
        return (jnp.where(sel, 0.0, NEG).astype(BF16),
                carry + jnp.sum(_colsum(eq), axis=0, keepdims=True))

    v_rows = vT_ref.shape[0]

    def head_rows(h):
        return slice(h * HEAD_DIM, (h + 1) * HEAD_DIM)

    def acc_rows(h):
        return slice(h * v_rows, (h + 1) * v_rows)

    def colmax16(t):
        t = jnp.max(t.reshape(t.shape[0] // 16, 16, t.shape[1]), axis=0)
        return jnp.max(t, axis=0, keepdims=True).astype(F32)

    mskm, carry0 = sel_mask(scm_ref[...], trim_ref[...], jnp.zeros((1, tq), F32))
    km = km_ref[...]
    vTm = vTm_ref[...]
    lgs = [_dot(km, qT_ref[head_rows(h), :]).astype(BF16) for h in range(n_heads)]
    ps = []
    for h in range(n_heads):
        t = lgs[h] + biasm_ref[h] + mskm
        m = colmax16(t)
        m_ref[h:h + 1, :] = m
        ps.append(jnp.exp(t - m.astype(BF16)))
    for h in range(n_heads):
        acc_ref[acc_rows(h), :] = _dot(vTm, ps[h])

    def att_body(kb, carry):
        msk, carry = sel_mask(sc_ref[kblock(kb), :], tri_ref[...], carry)
        which = jnp.clip(kb - i + 2, 0, 2)
        kblk = k_ref[kblock(kb), :]
        vblk = vT_ref[:, kblock(kb)]
        for h in range(n_heads):
            lg_ref[h] = _dot(kblk, qT_ref[head_rows(h), :]).astype(BF16)
        alphas = []
        for h in range(n_heads):
            t = lg_ref[h] + bias_ref[h, which] + msk
            m_old = m_ref[h:h + 1, :]
            m_new = jnp.maximum(m_old, colmax16(t))
            m_ref[h:h + 1, :] = m_new
            p_ref[h] = jnp.exp(t - m_new.astype(BF16))
            alphas.append(jnp.exp(m_old - m_new))
        for h in range(n_heads):
            acc_ref[acc_rows(h), :] = alphas[h] * acc_ref[acc_rows(h), :] + _dot(vblk, p_ref[h])
        return carry

    lax.fori_loop(0, nkb, att_body, carry0)

    for h in range(n_heads):
        a = acc_ref[acc_rows(h), :]
        out_ref[head_rows(h), :] = a[:HEAD_DIM] / a[HEAD_DIM:HEAD_DIM + 1]
    y_ref[...] = out_ref[...].T.astype(BF16)


def _rel_bucket_np(rel):
    half = REL_BUCKETS // 2
    max_exact = half // 2
    base = np.where(rel > 0, half, 0)
    n = np.abs(rel)
    nf = np.maximum(n, 1).astype(np.float64)
    large = max_exact + (np.log(nf / max_exact) / math.log(REL_MAX_DIST / max_exact)
                         * (half - max_exact)).astype(np.int32)
    large = np.minimum(large, half - 1)
    return base + np.where(n < max_exact, n, large)


def _bias_of_rel(rb, rel, rel_lo, rel_hi):
    rels = np.arange(rel_lo, rel_hi + 1)
    buckets = _rel_bucket_np(rels)
    col = lambda b: rb[int(b)].reshape((-1,) + (1,) * rel.ndim)
    val = jnp.broadcast_to(col(buckets[0]), (rb.shape[1],) + rel.shape)
    for j in range(1, len(rels)):
        if buckets[j] != buckets[j - 1]:
            val = jnp.where(rel[None] >= int(rels[j]), col(buckets[j]), val)
    return val


def _bias_tables(rel_bias, tq, seq):
    rb = rel_bias.astype(F32)
    far = -2 * tq
    assert np.all(_rel_bucket_np(np.arange(-(seq + N_META), -tq)) == _rel_bucket_np(np.array(far)))
    d = lax.broadcasted_iota(jnp.int32, (tq, tq), 0) - lax.broadcasted_iota(jnp.int32, (tq, tq), 1)
    rel = jnp.stack([jnp.full((tq, tq), far, jnp.int32), d - tq, d])
    bias = _bias_of_rel(rb, rel, far, tq - 1)
    rel_m = (lax.broadcasted_iota(jnp.int32, (N_META, seq), 0)
             - lax.broadcasted_iota(jnp.int32, (N_META, seq), 1) - N_META)
    bias_m = _bias_of_rel(rb, rel_m, -(seq + N_META), -1)
    return bias.astype(BF16), bias_m.astype(BF16)


def _dsa(qidxT, qT, wT, kidx, k, vT, kidx_m, k_m, vT_m, bias, bias_m, nb, seq):
    tq = DSA_TILE
    nq = seq // tq
    n_att = qT.shape[0]
    n_heads = n_att // HEAD_DIM
    tri = jnp.asarray(np.tril(np.ones((tq, tq), np.float32), -1), BF16)
    tri_m = jnp.asarray(np.tril(np.ones((N_META, N_META), np.float32), -1), BF16)
    qcol = lambda b, i: (0, b * nq + i)
    return pl.pallas_call(
        functools.partial(_dsa_kernel, min(TOPK_MAX, seq // 4)),
        out_shape=jax.ShapeDtypeStruct((nb * seq, n_att), BF16),
        grid=(nb, nq),
        in_specs=[
            pl.BlockSpec((qidxT.shape[0], tq), qcol),
            pl.BlockSpec((n_att, tq), qcol),
            pl.BlockSpec((IDX_HEADS, tq), qcol),
            pl.BlockSpec((seq, IDX_DIM), lambda b, i: (b, 0)),
            pl.BlockSpec((seq, HEAD_DIM), lambda b, i: (b, 0)),
            pl.BlockSpec((V_ROWS, seq), lambda b, i: (0, b)),
            _const_spec(kidx_m.shape),
            _const_spec(k_m.shape),
            _const_spec(vT_m.shape),
            _const_spec(bias.shape),
            pl.BlockSpec((bias_m.shape[0], N_META, tq), lambda b, i: (0, 0, i)),
            _const_spec(tri.shape),
            _const_spec(tri_m.shape),
        ],
        out_specs=pl.BlockSpec((tq, n_att), lambda b, i: (b * nq + i, 0)),
        scratch_shapes=[
            pltpu.VMEM((seq, tq), F32),
            pltpu.VMEM((N_META, tq), F32),
            pltpu.VMEM((n_heads, tq), F32),
            pltpu.VMEM((n_heads * V_ROWS, tq), F32),
            pltpu.VMEM((n_att, tq), F32),
            pltpu.VMEM((n_heads, tq, tq), BF16),
            pltpu.VMEM((n_heads, tq, tq), BF16),
        ],
        compiler_params=_cparams(("arbitrary", "arbitrary")),
        name="dsa_mixer",
    )(qidxT, qT, wT, kidx, k, vT, kidx_m, k_m, vT_m, bias, bias_m, tri, tri_m)


def _merge_kernel(h_ref, ya_ref, yb_ref, gpre_ref, gpost_ref, wg_ref, wa_ref, wb_ref, wo_ref, o_ref):
    h = h_ref[...]
    d = h.shape[1]
    hn = _rms(h, gpre_ref[...]).astype(BF16)
    pa = _dot(ya_ref[...], wa_ref[...])
    pb = _dot(yb_ref[...], wb_ref[...])
    g0 = jax.nn.sigmoid(_dot(hn, wg_ref[:, :d]))
    g1 = jax.nn.sigmoid(_dot(hn, wg_ref[:, d:]))
    merged = (g0 * pa + g1 * pb).astype(BF16)
    o_ref[...] = h + _rms(_dot(merged, wo_ref[...]), gpost_ref[...])


def _merge(h, ya, yb, gpre, gpost, wg, wa, wb, wo, nb, seq, tile):
    d = h.shape[1]
    sa = wa.shape[0]
    sb = wb.shape[0]
    nt = seq // tile
    tok = lambda b, i: (b * nt + i, 0)
    return pl.pallas_call(
        _merge_kernel,
        out_shape=jax.ShapeDtypeStruct(h.shape, F32),
        grid=(nb, nt),
        in_specs=[
            pl.BlockSpec((tile, d), tok),
            pl.BlockSpec((tile, sa), tok),
            pl.BlockSpec((tile, sb), tok),
            _const_spec((1, d)),
            _const_spec((1, d)),
            _const_spec(wg.shape),
            _const_spec(wa.shape),
            _const_spec(wb.shape),
            _const_spec(wo.shape),
        ],
        out_specs=pl.BlockSpec((tile, d), tok),
        compiler_params=_cparams(("arbitrary", "arbitrary")),
        name="mixer_merge",
    )(h, ya, yb, gpre, gpost, wg, wa, wb, wo)


def kernel(x, meta_tokens, ff1_norm_pre, ff1_norm_post, mix_norm_pre, mix_norm_post, ff2_norm_pre, ff2_norm_post, ff1_w_gate, ff1_w_up, ff1_w_down, ff2_w_gate, ff2_w_up, ff2_w_down, w_in, ssm_lambda_re, ssm_lambda_im, ssm_log_dt, ssm_b_re, ssm_b_im, ssm_c_re, ssm_c_im, ssm_d, ssm_w_glu, w_branch_a, rel_bias, w_branch_b, w_out):
    nb, seq, d = x.shape
    depth = w_in.shape[0]
    assert depth == 1, "meta-token rows are only carried as keys/state of a single layer"
    assert meta_tokens.shape[0] == N_META
    su = ssm_w_glu.shape[1]
    n_att = w_branch_b.shape[1]
    row = lambda v: v.astype(F32).reshape(1, -1)
    bf = lambda w: w.astype(BF16)

    o_u, o_qi = 0, su
    o_ki = o_qi + IDX_HEADS * IDX_DIM
    o_wi = o_ki + IDX_DIM
    o_q = o_wi + IDX_HEADS
    o_k = o_q + n_att
    o_v = o_k + HEAD_DIM
    o_g = o_v + HEAD_DIM
    win = w_in[0]
    w_nn = bf(jnp.concatenate([win[:, o_u:o_qi], win[:, o_ki:o_wi], win[:, o_k:o_v]], axis=1))
    w_t = bf(jnp.concatenate([win[:, o_qi:o_ki], win[:, o_q:o_k], win[:, o_v:o_g], win[:, o_wi:o_q]], axis=1).T)
    w_gates = bf(win[:, o_g:])

    hx = x.reshape(nb * seq, d)
    pad = PROJ_TILE // 4
    hm = jnp.zeros((pad, d), x.dtype).at[:N_META].set(meta_tokens.astype(x.dtype))

    ffn1 = (row(ff1_norm_pre[0]), row(ff1_norm_post[0]), bf(ff1_w_gate[0]), bf(ff1_w_up[0]), bf(ff1_w_down[0]))
    ffn2 = (row(ff2_norm_pre[0]), row(ff2_norm_post[0]), bf(ff2_w_gate[0]), bf(ff2_w_up[0]), bf(ff2_w_down[0]))

    h1 = _ffn(hx, *ffn1, FFN_TILE)
    h1m = _ffn(hm, *ffn1, pad)

    g_mix = row(mix_norm_pre[0])
    u, kidx, k, qidxT, qT, vT, wT = _proj(h1, g_mix, w_nn, w_t, nb, seq, PROJ_TILE)
    u_m, kidx_m, k_m, _, _, vT_m, _ = _proj(h1m, g_mix, w_nn, w_t, 1, pad, pad)

    bmat, a_re, a_im, cmat, d_skip = _s5_params(
        ssm_lambda_re[0], ssm_lambda_im[0], ssm_log_dt[0], ssm_b_re[0], ssm_b_im[0],
        ssm_c_re[0], ssm_c_im[0], ssm_d[0])
    ya = _s5(u.reshape(nb, seq, su), u_m[:N_META], bmat, a_re, a_im, cmat, d_skip, bf(ssm_w_glu[0]))
    ya = ya.reshape(nb * seq, su)

    bias, bias_m = _bias_tables(rel_bias, DSA_TILE, seq)
    yb = _dsa(qidxT, qT, wT, kidx, k, vT, kidx_m[:N_META], k_m[:N_META], vT_m[:, :N_META],
              bias, bias_m, nb, seq)

    h2 = _merge(h1, ya, yb, g_mix, row(mix_norm_post[0]), w_gates, bf(w_branch_a[0]),
                bf(w_branch_b[0]), bf(w_out[0]), nb, seq, MERGE_TILE)

    out = _ffn(h2, *ffn2, FFN_TILE)
    return out.reshape(nb, seq, d)
```

```python
import functools
import math

import numpy as np
import jax
import jax.numpy as jnp
from jax import lax
from jax.experimental import pallas as pl
from jax.experimental.pallas import tpu as pltpu

F32 = jnp.float32
BF16 = jnp.bfloat16

RMS_EPS = 1e-6
CHUNK = 64
N_META = 16
SSM_GROUP = 16
SSM_STATE = 64
HEAD_DIM = 64
V_ROWS = HEAD_DIM + 16
IDX_DIM = 64
IDX_HEADS = 8
TOPK_MAX = 256
REL_BUCKETS = 32
REL_MAX_DIST = 128
NEG = -1e30
BISECT_STEPS = 64
SEARCH_WARMUP = 12

VMEM_LIMIT = 56 * 1024 * 1024

FFN_TILE = 512
FFN_CHUNK = 256
PROJ_TILE = 512
S5_STEPS = 32
S5_LANES = 512
DSA_TILE = 256
SCAN_ROWS = 64
MERGE_TILE = 512


def _cparams(sem):
    return pltpu.CompilerParams(dimension_semantics=sem, vmem_limit_bytes=VMEM_LIMIT)


def _const_spec(shape):
    nd = len(shape)
    return pl.BlockSpec(shape, lambda *_: (0,) * nd)


def _resident_spec(shape):
    nd = len(shape)
    return pl.BlockSpec(shape, lambda *_: (0,) * nd, pipeline_mode=pl.Buffered(1))


def _rms(x, g):
    return x * lax.rsqrt(jnp.mean(x * x, axis=-1, keepdims=True) + RMS_EPS) * g


def _dot(a, b):
    return jnp.dot(a, b, preferred_element_type=F32)


def _dot_nt(a, b):
    return lax.dot_general(a, b, (((1,), (1,)), ((), ())), preferred_element_type=F32)


def _ffn_kernel(x_ref, gpre_ref, gpost_ref, wg_ref, wu_ref, wd_ref, o_ref, acc_ref):
    x = x_ref[...]
    hn = _rms(x, gpre_ref[...]).astype(BF16)
    n_chunks = wg_ref.shape[1] // FFN_CHUNK
    for c in range(n_chunks):
        cols = slice(c * FFN_CHUNK, (c + 1) * FFN_CHUNK)
        g = _dot(hn, wg_ref[:, cols].astype(BF16))
        u = _dot(hn, wu_ref[:, cols].astype(BF16))
        a = (jax.nn.silu(g) * u).astype(BF16)
        part = _dot(a, wd_ref[cols, :].astype(BF16))
        if c == 0:
            acc_ref[...] = part
        else:
            acc_ref[...] += part
    o_ref[...] = x + 0.5 * _rms(acc_ref[...], gpost_ref[...])


def _ffn(x, gpre, gpost, wg, wu, wd, tile):
    t, d = x.shape
    dff = wg.shape[1]
    return pl.pallas_call(
        _ffn_kernel,
        out_shape=jax.ShapeDtypeStruct((t, d), F32),
        grid=(t // tile,),
        in_specs=[
            pl.BlockSpec((tile, d), lambda i: (i, 0)),
            _const_spec((1, d)),
            _const_spec((1, d)),
            _resident_spec((d, dff)),
            _resident_spec((d, dff)),
            _resident_spec((dff, d)),
        ],
        out_specs=pl.BlockSpec((tile, d), lambda i: (i, 0)),
        scratch_shapes=[pltpu.VMEM((tile, d), F32)],
        compiler_params=_cparams(("arbitrary",)),
        name="ffn_half_step",
    )(x, gpre, gpost, wg, wu, wd)


def _proj_kernel(h_ref, g_ref, wn_ref, wt_ref, u_ref, kidx_ref, k_ref, qidxT_ref, qT_ref, vT_ref, wT_ref):
    hn = _rms(h_ref[...], g_ref[...]).astype(BF16)
    nn = _dot(hn, wn_ref[...])
    su = u_ref.shape[1]
    u_ref[...] = nn[:, :su].astype(BF16)
    kidx_ref[...] = nn[:, su:su + IDX_DIM].astype(BF16)
    k_ref[...] = nn[:, su + IDX_DIM:su + IDX_DIM + HEAD_DIM].astype(BF16)
    tt = _dot_nt(wt_ref[...], hn)
    nq = qidxT_ref.shape[0]
    na = qT_ref.shape[0]
    qidxT_ref[...] = (tt[:nq] * (IDX_DIM ** -0.5)).astype(BF16)
    qT_ref[...] = (tt[nq:nq + na] * (HEAD_DIM ** -0.5)).astype(BF16)
    ones = jnp.ones((V_ROWS - HEAD_DIM, tt.shape[1]), F32)
    vT_ref[...] = jnp.concatenate([tt[nq + na:nq + na + HEAD_DIM], ones], axis=0).astype(BF16)
    wT_ref[...] = tt[nq + na + HEAD_DIM:nq + na + HEAD_DIM + IDX_HEADS] * (IDX_HEADS ** -0.5)


def _proj(h, gain, w_nn, w_t, nb, seq, tile):
    d = h.shape[1]
    su = w_nn.shape[1] - IDX_DIM - HEAD_DIM
    nq = IDX_HEADS * IDX_DIM
    na = w_t.shape[0] - nq - HEAD_DIM - IDX_HEADS
    nt = seq // tile
    tok = lambda b, i: (b * nt + i, 0)
    tokT = lambda b, i: (0, b * nt + i)
    out_shape = (
        jax.ShapeDtypeStruct((nb * seq, su), BF16),
        jax.ShapeDtypeStruct((nb * seq, IDX_DIM), BF16),
        jax.ShapeDtypeStruct((nb * seq, HEAD_DIM), BF16),
        jax.ShapeDtypeStruct((nq, nb * seq), BF16),
        jax.ShapeDtypeStruct((na, nb * seq), BF16),
        jax.ShapeDtypeStruct((V_ROWS, nb * seq), BF16),
        jax.ShapeDtypeStruct((IDX_HEADS, nb * seq), F32),
    )
    out_specs = (
        pl.BlockSpec((tile, su), tok),
        pl.BlockSpec((tile, IDX_DIM), tok),
        pl.BlockSpec((tile, HEAD_DIM), tok),
        pl.BlockSpec((nq, tile), tokT),
        pl.BlockSpec((na, tile), tokT),
        pl.BlockSpec((V_ROWS, tile), tokT),
        pl.BlockSpec((IDX_HEADS, tile), tokT),
    )
    return pl.pallas_call(
        _proj_kernel,
        out_shape=out_shape,
        grid=(nb, nt),
        in_specs=[
            pl.BlockSpec((tile, d), tok),
            _const_spec((1, d)),
            _const_spec(w_nn.shape),
            _const_spec(w_t.shape),
        ],
        out_specs=out_specs,
        compiler_params=_cparams(("arbitrary", "arbitrary")),
        name="mixer_in_proj",
    )(h, gain, w_nn, w_t)


def _s5_scan(bu_ref, xs_ref, st_ref, are_ref, aim_ref, chunks, n_steps, nb, bcast_rows):
    n_state = are_ref.shape[1]
    for c in chunks:
        lanes = slice(c * S5_LANES, (c + 1) * S5_LANES)
        lanes_im = slice(n_state + c * S5_LANES, n_state + (c + 1) * S5_LANES)
        a_re = jnp.broadcast_to(are_ref[:, lanes], (nb, S5_LANES))
        a_im = jnp.broadcast_to(aim_ref[:, lanes], (nb, S5_LANES))
        x_re = st_ref[0, :, lanes]
        x_im = st_ref[1, :, lanes]
        for t in range(n_steps):
            rows = slice(t, t + 1) if bcast_rows else slice(t * nb, (t + 1) * nb)
            b_re = bu_ref[rows, lanes]
            b_im = bu_ref[rows, lanes_im]
            x_re, x_im = (a_re * x_re - a_im * x_im + b_re,
                          a_re * x_im + a_im * x_re + b_im)
            if xs_ref is not None:
                xs_ref[rows, lanes] = x_re.astype(BF16)
                xs_ref[rows, lanes_im] = x_im.astype(BF16)
        st_ref[0, :, lanes] = x_re
        st_ref[1, :, lanes] = x_im


def _s5_kernel(u_ref, um_ref, bmat_ref, are_ref, aim_ref, cmat_ref, d_ref, wglu_ref, y_ref,
               bu_ref, xs_ref, st_ref, bum_ref):
    nb, n_steps, su = u_ref.shape
    n_state = are_ref.shape[1]
    half_u = su // 2
    half_s = n_state // 2
    chunks_per_half = half_s // S5_LANES

    def b_project(u, dst_ref):
        for hf in range(2):
            uu = u[:, hf * half_u:(hf + 1) * half_u]
            bu = _dot(uu, bmat_ref[hf])
            dst_ref[:, hf * half_s:(hf + 1) * half_s] = bu[:, :half_s]
            dst_ref[:, n_state + hf * half_s:n_state + (hf + 1) * half_s] = bu[:, half_s:]

    @pl.when(pl.program_id(0) == 0)
    def _():
        st_ref[...] = jnp.zeros_like(st_ref)
        b_project(um_ref[...], bum_ref)
        _s5_scan(bum_ref, None, st_ref, are_ref, aim_ref, range(2 * chunks_per_half), um_ref.shape[0], nb, True)

    u = pltpu.einshape("btc->tbc", u_ref[...]).reshape(n_steps * nb, su)
    b_project(u, bu_ref)
    ys = []
    for hf in range(2):
        _s5_scan(bu_ref, xs_ref, st_ref, are_ref, aim_ref,
                 range(hf * chunks_per_half, (hf + 1) * chunks_per_half), n_steps, nb, False)
        y_re = _dot(xs_ref[:, hf * half_s:(hf + 1) * half_s], cmat_ref[hf, 0])
        y_im = _dot(xs_ref[:, n_state + hf * half_s:n_state + (hf + 1) * half_s], cmat_ref[hf, 1])
        ys.append(y_re - y_im)
    y = jnp.concatenate(ys, axis=1) + d_ref[...] * u.astype(F32)
    y = jax.nn.gelu(y)
    y = y * jax.nn.sigmoid(_dot(y.astype(BF16), wglu_ref[...]))
    y_ref[...] = pltpu.einshape("tbc->btc", y.astype(BF16).reshape(n_steps, nb, su))


def _s5(u, u_meta, bmat, a_re, a_im, cmat, d_skip, w_glu):
    nb, seq, su = u.shape
    rows = S5_STEPS * nb
    n_state = a_re.shape[1]
    return pl.pallas_call(
        _s5_kernel,
        out_shape=jax.ShapeDtypeStruct((nb, seq, su), BF16),
        grid=(seq // S5_STEPS,),
        in_specs=[
            pl.BlockSpec((nb, S5_STEPS, su), lambda i: (0, i, 0)),
            _const_spec(u_meta.shape),
            _const_spec(bmat.shape),
            _const_spec(a_re.shape),
            _const_spec(a_im.shape),
            _const_spec(cmat.shape),
            _const_spec(d_skip.shape),
            _const_spec(w_glu.shape),
        ],
        out_specs=pl.BlockSpec((nb, S5_STEPS, su), lambda i: (0, i, 0)),
        scratch_shapes=[
            pltpu.VMEM((rows, 2 * n_state), F32),
            pltpu.VMEM((rows, 2 * n_state), BF16),
            pltpu.VMEM((2, nb, n_state), F32),
            pltpu.VMEM((u_meta.shape[0], 2 * n_state), F32),
        ],
        compiler_params=_cparams(("arbitrary",)),
        name="s5_mixer",
    )(u, u_meta, bmat, a_re, a_im, cmat, d_skip, w_glu)


def _s5_params(lambda_re, lambda_im, log_dt, b_re, b_im, c_re, c_im, d_skip):
    g, p, m = b_re.shape
    lr = lambda_re.astype(F32)
    li = lambda_im.astype(F32)
    dt = jnp.exp(log_dt.astype(F32))[:, None]
    mag = jnp.exp(lr * dt)
    a_re = mag * jnp.cos(li * dt)
    a_im = mag * jnp.sin(li * dt)
    den = lr * lr + li * li
    num_re = a_re - 1.0
    num_im = a_im
    coef_re = (num_re * lr + num_im * li) / den
    coef_im = (num_im * lr - num_re * li) / den
    br = b_re.astype(F32)
    bi = b_im.astype(F32)
    bbar_re = coef_re[..., None] * br - coef_im[..., None] * bi
    bbar_im = coef_re[..., None] * bi + coef_im[..., None] * br
    gh = g // 2
    eye = jnp.eye(gh, dtype=F32)

    def in_mat(bb):
        bb = bb.reshape(2, gh, p, m)
        return jnp.einsum('hgpm,gk->hgmkp', bb, eye).reshape(2, gh * m, gh * p)

    def out_mat(cc):
        cc = cc.astype(F32).reshape(2, gh, m, p)
        return jnp.einsum('hgmp,gk->hgpkm', cc, eye).reshape(2, gh * p, gh * m)

    bmat = jnp.concatenate([in_mat(bbar_re), in_mat(bbar_im)], axis=2).astype(BF16)
    cmat = jnp.stack([out_mat(c_re), out_mat(c_im)], axis=1).astype(BF16)
    return (bmat, a_re.reshape(1, g * p), a_im.reshape(1, g * p), cmat,
            d_skip.astype(F32).reshape(1, g * m))


def _colsum(x):
    return jnp.sum(x.reshape(x.shape[0] // 8, 8, x.shape[1]), axis=0)


def _colmax(x):
    return jnp.max(x.reshape(x.shape[0] // 8, 8, x.shape[1]), axis=0)


def _colmin(x):
    return jnp.min(x.reshape(x.shape[0] // 8, 8, x.shape[1]), axis=0)


def _dsa_kernel(top_k, qidxT_ref, qT_ref, wT_ref, kidx_ref, k_ref, vT_ref, kidxm_ref, km_ref, vTm_ref,
                bias_ref, biasm_ref, tri_ref, trim_ref, y_ref,
                sc_ref, scm_ref, m_ref, acc_ref, out_ref, lg_ref, p_ref):
    i = pl.program_id(1)
    tq = qT_ref.shape[1]
    tk = tq
    nkb = i + 1
    n_heads = qT_ref.shape[0] // HEAD_DIM
    kf = float(top_k)

    def kblock(kb):
        return pl.ds(pl.multiple_of(kb * tk, tk), tk)

    kc = lax.broadcasted_iota(jnp.int32, (tk, tq), 0) // CHUNK
    qc = lax.broadcasted_iota(jnp.int32, (tk, tq), 1) // CHUNK
    adm_diag = kc <= qc

    def idx_score(kidx):
        s = jnp.zeros((kidx.shape[0], tq), F32)
        for h in range(IDX_HEADS):
            sh = _dot(kidx, qidxT_ref[h * IDX_DIM:(h + 1) * IDX_DIM, :])
            s = s + jnp.maximum(sh, 0.0) * wT_ref[h:h + 1, :]
        return s

    sm = idx_score(kidxm_ref[...])
    scm_ref[...] = sm

    def score_body(kb, carry):
        mn, mx = carry
        s = idx_score(kidx_ref[kblock(kb), :])
        adm = jnp.logical_or(kb < i, adm_diag)
        sc_ref[kblock(kb), :] = jnp.where(adm, s, NEG)
        mn = jnp.minimum(mn, _colmin(jnp.where(adm, s, -NEG)))
        mx = jnp.maximum(mx, _colmax(jnp.where(adm, s, NEG)))
        return mn, mx

    mn, mx = lax.fori_loop(0, nkb, score_body, (_colmin(sm), _colmax(sm)))
    mn = jnp.min(mn, axis=0, keepdims=True)
    mx = jnp.max(mx, axis=0, keepdims=True)

    def reduce_keys(fn, combine, finish):
        def body(kb, acc):
            base = pl.multiple_of(kb * tk, tk)
            for r in range(0, tk, SCAN_ROWS):
                acc = combine(acc, fn(sc_ref[pl.ds(base + r, SCAN_ROWS), :]))
            return acc

        return finish(lax.fori_loop(0, nkb, body, fn(scm_ref[...])))

    def count_gt(t):
        return reduce_keys(lambda s: _colsum(jnp.where(s > t, 1.0, 0.0)), jnp.add,
                           lambda a: jnp.sum(a, axis=0, keepdims=True))

    n_adm = (N_META + i * tk + (lax.broadcasted_iota(jnp.int32, (1, tq), 1) // CHUNK + 1) * CHUNK).astype(F32)
    c_pos, c_nonneg = reduce_keys(
        lambda s: (_colsum(jnp.where(s > 0.0, 1.0, 0.0)), _colsum(jnp.where(s >= 0.0, 1.0, 0.0))),
        lambda a, b: (a[0] + b[0], a[1] + b[1]),
        lambda a: (jnp.sum(a[0], axis=0, keepdims=True), jnp.sum(a[1], axis=0, keepdims=True)))
    zero_vk = jnp.logical_and(c_pos < kf, c_nonneg >= kf)
    pos = c_pos >= kf
    lo0 = jnp.where(pos, 0.0, mn - (1.0 + jnp.abs(mn)))
    hi0 = jnp.where(pos, mx, 0.0)
    clo0 = jnp.where(pos, c_pos, n_adm)

    def open_rows(clo):
        return jnp.max(jnp.where(zero_vk, 0.0, clo)) > kf

    def search_cond(carry):
        it, _, _, _, go = carry
        return jnp.logical_and(it < BISECT_STEPS, go)

    def halve(carry):
        lo, hi, clo = carry
        mid = 0.5 * lo + 0.5 * hi
        c = count_gt(mid)
        up = c >= kf
        return jnp.where(up, mid, lo), jnp.where(up, hi, mid), jnp.where(up, c, clo)

    def search_body(carry):
        it, lo, hi, clo, _ = carry
        lo, hi, clo = halve(halve((lo, hi, clo)))
        return it + 2, lo, hi, clo, open_rows(clo)

    lo, hi, clo = lax.fori_loop(0, SEARCH_WARMUP, lambda _, c: halve(c), (lo0, hi0, clo0))
    _, lo, _, _, _ = lax.while_loop(search_cond, search_body, (SEARCH_WARMUP, lo, hi, clo, open_rows(clo)))

    thr = reduce_keys(lambda s: _colmin(jnp.where(s > lo, s, -NEG)), jnp.minimum,
                      lambda a: jnp.min(a, axis=0, keepdims=True))
    thr = jnp.where(zero_vk, 0.0, thr)
    need = kf - count_gt(thr)

    def sel_mask(s, tri, carry):
        eq = jnp.where(s == thr, 1.0, 0.0)
        rank = carry + _dot(tri, eq.astype(BF16))
        sel = jnp.logical_or(s > thr, jnp.logical_and(s == thr, rank < need))
        return (jnp.where(sel, 0.0, NEG).astype(BF16),
                carry + jnp.sum(_colsum(eq), axis=0, keepdims=True))

    v_rows = vT_ref.shape[0]

    def head_rows(h):
        return slice(h * HEAD_DIM, (h + 1) * HEAD_DIM)

    def acc_rows(h):
        return slice(h * v_rows, (h + 1) * v_rows)

    def colmax16(t):
        t = jnp.max(t.reshape(t.shape[0] // 16, 16, t.shape[1]), axis=0)
        return jnp.max(t, axis=0, keepdims=True).astype(F32)

    mskm, carry0 = sel_mask(scm_ref[...], trim_ref[...], jnp.zeros((1, tq), F32))
    km = km_ref[...]
    vTm = vTm_ref[...]
    lgs = [_dot(km, qT_ref[head_rows(h), :]).astype(BF16) for h in range(n_heads)]
    ps = []
    for h in range(n_heads):
        t = lgs[h] + biasm_ref[h] + mskm
        m = colmax16(t)
        m_ref[h:h + 1, :] = m
        ps.append(jnp.exp(t - m.astype(BF16)))
    for h in range(n_heads):
        acc_ref[acc_rows(h), :] = _dot(vTm, ps[h])

    def att_body(kb, carry):
        msk, carry = sel_mask(sc_ref[kblock(kb), :], tri_ref[...], carry)
        which = jnp.clip(kb - i + 2, 0, 2)
        kblk = k_ref[kblock(kb), :]
        vblk = vT_ref[:, kblock(kb)]
        for h in range(n_heads):
            lg_ref[h] = _dot(kblk, qT_ref[head_rows(h), :]).astype(BF16)
        alphas = []
        for h in range(n_heads):
            t = lg_ref[h] + bias_ref[h, which] + msk
            m_old = m_ref[h:h + 1, :]
            m_new = jnp.maximum(m_old, colmax16(t))
            m_ref[h:h + 1, :] = m_new
            p_ref[h] = jnp.exp(t - m_new.astype(BF16))
            alphas.append(jnp.exp(m_old - m_new))
        for h in range(n_heads):
            acc_ref[acc_rows(h), :] = alphas[h] * acc_ref[acc_rows(h), :] + _dot(vblk, p_ref[h])
        return carry

    lax.fori_loop(0, nkb, att_body, carry0)

    for h in range(n_heads):
        a = acc_ref[acc_rows(h), :]
        out_ref[head_rows(h), :] = a[:HEAD_DIM] / a[HEAD_DIM:HEAD_DIM + 1]
    y_ref[...] = out_ref[...].T.astype(BF16)


def _rel_bucket_np(rel):
    half = REL_BUCKETS // 2
    max_exact = half // 2
    base = np.where(rel > 0, half, 0)
    n = np.abs(rel)
    nf = np.maximum(n, 1).astype(np.float64)
    large = max_exact + (np.log(nf / max_exact) / math.log(REL_MAX_DIST / max_exact)
                         * (half - max_exact)).astype(np.int32)
    large = np.minimum(large, half - 1)
    return base + np.where(n < max_exact, n, large)


def _bias_of_rel(rb, rel, rel_lo, rel_hi):
    rels = np.arange(rel_lo, rel_hi + 1)
    buckets = _rel_bucket_np(rels)
    col = lambda b: rb[int(b)].reshape((-1,) + (1,) * rel.ndim)
    val = jnp.broadcast_to(col(buckets[0]), (rb.shape[1],) + rel.shape)
    for j in range(1, len(rels)):
        if buckets[j] != buckets[j - 1]:
            val = jnp.where(rel[None] >= int(rels[j]), col(buckets[j]), val)
    return val


def _bias_tables(rel_bias, tq, seq):
    rb = rel_bias.astype(F32)
    far = -2 * tq
    assert np.all(_rel_bucket_np(np.arange(-(seq + N_META), -tq)) == _rel_bucket_np(np.array(far)))
    d = lax.broadcasted_iota(jnp.int32, (tq, tq), 0) - lax.broadcasted_iota(jnp.int32, (tq, tq), 1)
    rel = jnp.stack([jnp.full((tq, tq), far, jnp.int32), d - tq, d])
    bias = _bias_of_rel(rb, rel, far, tq - 1)
    rel_m = (lax.broadcasted_iota(jnp.int32, (N_META, seq), 0)
             - lax.broadcasted_iota(jnp.int32, (N_META, seq), 1) - N_META)
    bias_m = _bias_of_rel(rb, rel_m, -(seq + N_META), -1)
    return bias.astype(BF16), bias_m.astype(BF16)


def _dsa(qidxT, qT, wT, kidx, k, vT, kidx_m, k_m, vT_m, bias, bias_m, nb, seq):
    tq = DSA_TILE
    nq = seq // tq
    n_att = qT.shape[0]
    n_heads = n_att // HEAD_DIM
    tri = jnp.asarray(np.tril(np.ones((tq, tq), np.float32), -1), BF16)
    tri_m = jnp.asarray(np.tril(np.ones((N_META, N_META), np.float32), -1), BF16)
    qcol = lambda b, i: (0, b * nq + i)
    return pl.pallas_call(
        functools.partial(_dsa_kernel, min(TOPK_MAX, seq // 4)),
        out_shape=jax.ShapeDtypeStruct((nb * seq, n_att), BF16),
        grid=(nb, nq),
        in_specs=[
            pl.BlockSpec((qidxT.shape[0], tq), qcol),
            pl.BlockSpec((n_att, tq), qcol),
            pl.BlockSpec((IDX_HEADS, tq), qcol),
            pl.BlockSpec((seq, IDX_DIM), lambda b, i: (b, 0)),
            pl.BlockSpec((seq, HEAD_DIM), lambda b, i: (b, 0)),
            pl.BlockSpec((V_ROWS, seq), lambda b, i: (0, b)),
            _const_spec(kidx_m.shape),
            _const_spec(k_m.shape),
            _const_spec(vT_m.shape),
            _const_spec(bias.shape),
            pl.BlockSpec((bias_m.shape[0], N_META, tq), lambda b, i: (0, 0, i)),
            _const_spec(tri.shape),
            _const_spec(tri_m.shape),
        ],
        out_specs=pl.BlockSpec((tq, n_att), lambda b, i: (b * nq + i, 0)),
        scratch_shapes=[
            pltpu.VMEM((seq, tq), F32),
            pltpu.VMEM((N_META, tq), F32),
            pltpu.VMEM((n_heads, tq), F32),
            pltpu.VMEM((n_heads * V_ROWS, tq), F32),
            pltpu.VMEM((n_att, tq), F32),
            pltpu.VMEM((n_heads, tq, tq), BF16),
            pltpu.VMEM((n_heads, tq, tq), BF16),
        ],
        compiler_params=_cparams(("arbitrary", "arbitrary")),
        name="dsa_mixer",
    )(qidxT, qT, wT, kidx, k, vT, kidx_m, k_m, vT_m, bias, bias_m, tri, tri_m)


def _merge_kernel(h_ref, ya_ref, yb_ref, gpre_ref, gpost_ref, wg_ref, wa_ref, wb_ref, wo_ref, o_ref):
    h = h_ref[...]
    d = h.shape[1]
    hn = _rms(h, gpre_ref[...]).astype(BF16)
    pa = _dot(ya_ref[...], wa_ref[...])
    pb = _dot(yb_ref[...], wb_ref[...])
    g0 = jax.nn.sigmoid(_dot(hn, wg_ref[:, :d]))
    g1 = jax.nn.sigmoid(_dot(hn, wg_ref[:, d:]))
    merged = (g0 * pa + g1 * pb).astype(BF16)
    o_ref[...] = h + _rms(_dot(merged, wo_ref[...]), gpost_ref[...])


def _merge(h, ya, yb, gpre, gpost, wg, wa, wb, wo, nb, seq, tile):
    d = h.shape[1]
    sa = wa.shape[0]
    sb = wb.shape[0]
    nt = seq // tile
    tok = lambda b, i: (b * nt + i, 0)
    return pl.pallas_call(
        _merge_kernel,
        out_shape=jax.ShapeDtypeStruct(h.shape, F32),
        grid=(nb, nt),
        in_specs=[
            pl.BlockSpec((tile, d), tok),
            pl.BlockSpec((tile, sa), tok),
            pl.BlockSpec((tile, sb), tok),
            _const_spec((1, d)),
            _const_spec((1, d)),
            _const_spec(wg.shape),
            _const_spec(wa.shape),
            _const_spec(wb.shape),
            _const_spec(wo.shape),
        ],
        out_specs=pl.BlockSpec((tile, d), tok),
        compiler_params=_cparams(("arbitrary", "arbitrary")),
        name="mixer_merge",
    )(h, ya, yb, gpre, gpost, wg, wa, wb, wo)


def kernel(x, meta_tokens, ff1_norm_pre, ff1_norm_post, mix_norm_pre, mix_norm_post, ff2_norm_pre, ff2_norm_post, ff1_w_gate, ff1_w_up, ff1_w_down, ff2_w_gate, ff2_w_up, ff2_w_down, w_in, ssm_lambda_re, ssm_lambda_im, ssm_log_dt, ssm_b_re, ssm_b_im, ssm_c_re, ssm_c_im, ssm_d, ssm_w_glu, w_branch_a, rel_bias, w_branch_b, w_out):
    nb, seq, d = x.shape
    depth = w_in.shape[0]
    assert depth == 1, "meta-token rows are only carried as keys/state of a single layer"
    assert meta_tokens.shape[0] == N_META
    su = ssm_w_glu.shape[1]
    n_att = w_branch_b.shape[1]
    row = lambda v: v.astype(F32).reshape(1, -1)
    bf = lambda w: w.astype(BF16)

    o_u, o_qi = 0, su
    o_ki = o_qi + IDX_HEADS * IDX_DIM
    o_wi = o_ki + IDX_DIM
    o_q = o_wi + IDX_HEADS
    o_k = o_q + n_att
    o_v = o_k + HEAD_DIM
    o_g = o_v + HEAD_DIM
    win = w_in[0]
    w_nn = bf(jnp.concatenate([win[:, o_u:o_qi], win[:, o_ki:o_wi], win[:, o_k:o_v]], axis=1))
    w_t = bf(jnp.concatenate([win[:, o_qi:o_ki], win[:, o_q:o_k], win[:, o_v:o_g], win[:, o_wi:o_q]], axis=1).T)
    w_gates = bf(win[:, o_g:])

    hx = x.reshape(nb * seq, d)
    pad = PROJ_TILE // 4
    hm = jnp.zeros((pad, d), x.dtype).at[:N_META].set(meta_tokens.astype(x.dtype))

    ffn1 = (row(ff1_norm_pre[0]), row(ff1_norm_post[0]), ff1_w_gate[0], ff1_w_up[0], ff1_w_down[0])
    ffn2 = (row(ff2_norm_pre[0]), row(ff2_norm_post[0]), ff2_w_gate[0], ff2_w_up[0], ff2_w_down[0])

    h1 = _ffn(hx, *ffn1, FFN_TILE)
    h1m = _ffn(hm, *ffn1, pad)

    g_mix = row(mix_norm_pre[0])
    u, kidx, k, qidxT, qT, vT, wT = _proj(h1, g_mix, w_nn, w_t, nb, seq, PROJ_TILE)
    u_m, kidx_m, k_m, _, _, vT_m, _ = _proj(h1m, g_mix, w_nn, w_t, 1, pad, pad)

    bmat, a_re, a_im, cmat, d_skip = _s5_params(
        ssm_lambda_re[0], ssm_lambda_im[0], ssm_log_dt[0], ssm_b_re[0], ssm_b_im[0],
        ssm_c_re[0], ssm_c_im[0], ssm_d[0])
    ya = _s5(u.reshape(nb, seq, su), u_m[:N_META], bmat, a_re, a_im, cmat, d_skip, bf(ssm_w_glu[0]))
    ya = ya.reshape(nb * seq, su)

    bias, bias_m = _bias_tables(rel_bias, DSA_TILE, seq)
    yb = _dsa(qidxT, qT, wT, kidx, k, vT, kidx_m[:N_META], k_m[:N_META], vT_m[:, :N_META],
              bias, bias_m, nb, seq)

    h2 = _merge(h1, ya, yb, g_mix, row(mix_norm_post[0]), w_gates, bf(w_branch_a[0]),
                bf(w_branch_b[0]), bf(w_out[0]), nb, seq, MERGE_TILE)

    out = _ffn(h2, *ffn2, FFN_TILE)
    return out.reshape(nb, seq, d)
```

```python
import functools
import math

import numpy as np
import jax
import jax.numpy as jnp
from jax import lax
from jax.experimental import pallas as pl
from jax.experimental.pallas import tpu as pltpu

F32 = jnp.float32
BF16 = jnp.bfloat16

RMS_EPS = 1e-6
CHUNK = 64
N_META = 16
SSM_GROUP = 16
SSM_STATE = 64
HEAD_DIM = 64
V_ROWS = HEAD_DIM + 16
IDX_DIM = 64
IDX_HEADS = 8
TOPK_MAX = 256
REL_BUCKETS = 32
REL_MAX_DIST = 128
NEG = -1e30
BISECT_STEPS = 64
SEARCH_WARMUP = 12

VMEM_LIMIT = 56 * 1024 * 1024

FFN_TILE = 1024
FFN_SLAB = 512
FFN_CHUNK = 256
PROJ_TILE = 1024
PROJ_SLAB = 512
META_PAD = 128
S5_STEPS = 32
S5_LANES = 512
DSA_TILE = 256
SCAN_ROWS = 64
MERGE_TILE = 1024
MERGE_SLAB = 512


def _cparams(sem):
    return pltpu.CompilerParams(dimension_semantics=sem, vmem_limit_bytes=VMEM_LIMIT)


def _const_spec(shape):
    nd = len(shape)
    return pl.BlockSpec(shape, lambda *_: (0,) * nd)


def _resident_spec(shape):
    nd = len(shape)
    return pl.BlockSpec(shape, lambda *_: (0,) * nd, pipeline_mode=pl.Buffered(1))


def _rms(x, g):
    return x * lax.rsqrt(jnp.mean(x * x, axis=-1, keepdims=True) + RMS_EPS) * g


def _dot(a, b):
    return jnp.dot(a, b, preferred_element_type=F32)


def _dot_nt(a, b):
    return lax.dot_general(a, b, (((1,), (1,)), ((), ())), preferred_element_type=F32)


def _ffn_kernel(x_ref, gpre_ref, gpost_ref, wg_ref, wu_ref, wd_ref, o_ref, acc_ref):
    n_sub, sub, _ = acc_ref.shape
    n_chunks = wg_ref.shape[1] // FFN_CHUNK
    for s in range(n_sub):
        rows = slice(s * sub, (s + 1) * sub)
        x = x_ref[rows, :]
        hn = _rms(x, gpre_ref[...]).astype(BF16)
        for c in range(n_chunks):
            cols = slice(c * FFN_CHUNK, (c + 1) * FFN_CHUNK)
            g = _dot(hn, wg_ref[:, cols])
            u = _dot(hn, wu_ref[:, cols])
            a = (jax.nn.silu(g) * u).astype(BF16)
            part = _dot(a, wd_ref[cols, :])
            if c == 0:
                acc_ref[s] = part
            else:
                acc_ref[s] += part
        o_ref[rows, :] = x + 0.5 * _rms(acc_ref[s], gpost_ref[...])


def _ffn(x, gpre, gpost, wg, wu, wd, tile):
    t, d = x.shape
    dff = wg.shape[1]
    sub = min(tile, FFN_SLAB)
    return pl.pallas_call(
        _ffn_kernel,
        out_shape=jax.ShapeDtypeStruct((t, d), F32),
        grid=(t // tile,),
        in_specs=[
            pl.BlockSpec((tile, d), lambda i: (i, 0)),
            _const_spec((1, d)),
            _const_spec((1, d)),
            _resident_spec((d, dff)),
            _resident_spec((d, dff)),
            _resident_spec((dff, d)),
        ],
        out_specs=pl.BlockSpec((tile, d), lambda i: (i, 0)),
        scratch_shapes=[pltpu.VMEM((tile // sub, sub, d), F32)],
        compiler_params=_cparams(("arbitrary",)),
        name="ffn_half_step",
    )(x, gpre, gpost, wg, wu, wd)


def _proj_kernel(h_ref, g_ref, wn_ref, wt_ref, u_ref, kidx_ref, k_ref, qidxT_ref, qT_ref, vT_ref, wT_ref):
    tile = h_ref.shape[0]
    slab = min(tile, PROJ_SLAB)
    su = u_ref.shape[1]
    nq = qidxT_ref.shape[0]
    na = qT_ref.shape[0]
    for r in range(0, tile, slab):
        rows = slice(r, r + slab)
        hn = _rms(h_ref[rows, :], g_ref[...]).astype(BF16)
        nn = _dot(hn, wn_ref[...])
        u_ref[rows, :] = nn[:, :su].astype(BF16)
        kidx_ref[rows, :] = nn[:, su:su + IDX_DIM].astype(BF16)
        k_ref[rows, :] = nn[:, su + IDX_DIM:su + IDX_DIM + HEAD_DIM].astype(BF16)
        tt = _dot_nt(wt_ref[...], hn)
        qidxT_ref[:, rows] = (tt[:nq] * (IDX_DIM ** -0.5)).astype(BF16)
        qT_ref[:, rows] = (tt[nq:nq + na] * (HEAD_DIM ** -0.5)).astype(BF16)
        ones = jnp.ones((V_ROWS - HEAD_DIM, slab), F32)
        vT_ref[:, rows] = jnp.concatenate([tt[nq + na:nq + na + HEAD_DIM], ones], axis=0).astype(BF16)
        wT_ref[:, rows] = tt[nq + na + HEAD_DIM:nq + na + HEAD_DIM + IDX_HEADS] * (IDX_HEADS ** -0.5)


def _proj(h, gain, w_nn, w_t, nb, seq, tile):
    d = h.shape[1]
    su = w_nn.shape[1] - IDX_DIM - HEAD_DIM
    nq = IDX_HEADS * IDX_DIM
    na = w_t.shape[0] - nq - HEAD_DIM - IDX_HEADS
    nt = seq // tile
    tok = lambda b, i: (b * nt + i, 0)
    tokT = lambda b, i: (0, b * nt + i)
    out_shape = (
        jax.ShapeDtypeStruct((nb * seq, su), BF16),
        jax.ShapeDtypeStruct((nb * seq, IDX_DIM), BF16),
        jax.ShapeDtypeStruct((nb * seq, HEAD_DIM), BF16),
        jax.ShapeDtypeStruct((nq, nb * seq), BF16),
        jax.ShapeDtypeStruct((na, nb * seq), BF16),
        jax.ShapeDtypeStruct((V_ROWS, nb * seq), BF16),
        jax.ShapeDtypeStruct((IDX_HEADS, nb * seq), F32),
    )
    out_specs = (
        pl.BlockSpec((tile, su), tok),
        pl.BlockSpec((tile, IDX_DIM), tok),
        pl.BlockSpec((tile, HEAD_DIM), tok),
        pl.BlockSpec((nq, tile), tokT),
        pl.BlockSpec((na, tile), tokT),
        pl.BlockSpec((V_ROWS, tile), tokT),
        pl.BlockSpec((IDX_HEADS, tile), tokT),
    )
    return pl.pallas_call(
        _proj_kernel,
        out_shape=out_shape,
        grid=(nb, nt),
        in_specs=[
            pl.BlockSpec((tile, d), tok),
            _const_spec((1, d)),
            _const_spec(w_nn.shape),
            _const_spec(w_t.shape),
        ],
        out_specs=out_specs,
        compiler_params=_cparams(("arbitrary", "arbitrary")),
        name="mixer_in_proj",
    )(h, gain, w_nn, w_t)


def _s5_scan(bu_ref, xs_ref, st_ref, are_ref, aim_ref, chunks, n_steps, nb, bcast_rows):
    n_state = are_ref.shape[1]
    for c in chunks:
        lanes = slice(c * S5_LANES, (c + 1) * S5_LANES)
        lanes_im = slice(n_state + c * S5_LANES, n_state + (c + 1) * S5_LANES)
        a_re = jnp.broadcast_to(are_ref[:, lanes], (nb, S5_LANES))
        a_im = jnp.broadcast_to(aim_ref[:, lanes], (nb, S5_LANES))
        x_re = st_ref[0, :, lanes]
        x_im = st_ref[1, :, lanes]
        for t in range(n_steps):
            rows = slice(t, t + 1) if bcast_rows else slice(t * nb, (t + 1) * nb)
            b_re = bu_ref[rows, lanes]
            b_im = bu_ref[rows, lanes_im]
            x_re, x_im = (a_re * x_re - a_im * x_im + b_re,
                          a_re * x_im + a_im * x_re + b_im)
            if xs_ref is not None:
                xs_ref[rows, lanes] = x_re.astype(BF16)
                xs_ref[rows, lanes_im] = x_im.astype(BF16)
        st_ref[0, :, lanes] = x_re
        st_ref[1, :, lanes] = x_im


def _s5_kernel(u_ref, um_ref, bmat_ref, are_ref, aim_ref, cmat_ref, d_ref, wglu_ref, y_ref,
               bu_ref, xs_ref, st_ref, bum_ref):
    nb, n_steps, su = u_ref.shape
    n_state = are_ref.shape[1]
    half_u = su // 2
    half_s = n_state // 2
    chunks_per_half = half_s // S5_LANES

    def b_project(u, dst_ref):
        for hf in range(2):
            uu = u[:, hf * half_u:(hf + 1) * half_u]
            bu = _dot(uu, bmat_ref[hf])
            dst_ref[:, hf * half_s:(hf + 1) * half_s] = bu[:, :half_s]
            dst_ref[:, n_state + hf * half_s:n_state + (hf + 1) * half_s] = bu[:, half_s:]

    @pl.when(pl.program_id(0) == 0)
    def _():
        st_ref[...] = jnp.zeros_like(st_ref)
        b_project(um_ref[...], bum_ref)
        _s5_scan(bum_ref, None, st_ref, are_ref, aim_ref, range(2 * chunks_per_half), um_ref.shape[0], nb, True)

    u = pltpu.einshape("btc->tbc", u_ref[...]).reshape(n_steps * nb, su)
    b_project(u, bu_ref)
    ys = []
    for hf in range(2):
        _s5_scan(bu_ref, xs_ref, st_ref, are_ref, aim_ref,
                 range(hf * chunks_per_half, (hf + 1) * chunks_per_half), n_steps, nb, False)
        y_re = _dot(xs_ref[:, hf * half_s:(hf + 1) * half_s], cmat_ref[hf, 0])
        y_im = _dot(xs_ref[:, n_state + hf * half_s:n_state + (hf + 1) * half_s], cmat_ref[hf, 1])
        ys.append(y_re - y_im)
    y = jnp.concatenate(ys, axis=1) + d_ref[...] * u.astype(F32)
    y = jax.nn.gelu(y)
    y = y * jax.nn.sigmoid(_dot(y.astype(BF16), wglu_ref[...]))
    y_ref[...] = pltpu.einshape("tbc->btc", y.astype(BF16).reshape(n_steps, nb, su))


def _s5(u, u_meta, bmat, a_re, a_im, cmat, d_skip, w_glu):
    nb, seq, su = u.shape
    rows = S5_STEPS * nb
    n_state = a_re.shape[1]
    return pl.pallas_call(
        _s5_kernel,
        out_shape=jax.ShapeDtypeStruct((nb, seq, su), BF16),
        grid=(seq // S5_STEPS,),
        in_specs=[
            pl.BlockSpec((nb, S5_STEPS, su), lambda i: (0, i, 0)),
            _const_spec(u_meta.shape),
            _const_spec(bmat.shape),
            _const_spec(a_re.shape),
            _const_spec(a_im.shape),
            _const_spec(cmat.shape),
            _const_spec(d_skip.shape),
            _const_spec(w_glu.shape),
        ],
        out_specs=pl.BlockSpec((nb, S5_STEPS, su), lambda i: (0, i, 0)),
        scratch_shapes=[
            pltpu.VMEM((rows, 2 * n_state), F32),
            pltpu.VMEM((rows, 2 * n_state), BF16),
            pltpu.VMEM((2, nb, n_state), F32),
            pltpu.VMEM((u_meta.shape[0], 2 * n_state), F32),
        ],
        compiler_params=_cparams(("arbitrary",)),
        name="s5_mixer",
    )(u, u_meta, bmat, a_re, a_im, cmat, d_skip, w_glu)


def _s5_params(lambda_re, lambda_im, log_dt, b_re, b_im, c_re, c_im, d_skip):
    g, p, m = b_re.shape
    lr = lambda_re.astype(F32)
    li = lambda_im.astype(F32)
    dt = jnp.exp(log_dt.astype(F32))[:, None]
    mag = jnp.exp(lr * dt)
    a_re = mag * jnp.cos(li * dt)
    a_im = mag * jnp.sin(li * dt)
    den = lr * lr + li * li
    num_re = a_re - 1.0
    num_im = a_im
    coef_re = (num_re * lr + num_im * li) / den
    coef_im = (num_im * lr - num_re * li) / den
    br = b_re.astype(F32)
    bi = b_im.astype(F32)
    bbar_re = coef_re[..., None] * br - coef_im[..., None] * bi
    bbar_im = coef_re[..., None] * bi + coef_im[..., None] * br
    gh = g // 2
    eye = jnp.eye(gh, dtype=F32)

    def in_mat(bb):
        bb = bb.reshape(2, gh, p, m)
        return jnp.einsum('hgpm,gk->hgmkp', bb, eye).reshape(2, gh * m, gh * p)

    def out_mat(cc):
        cc = cc.astype(F32).reshape(2, gh, m, p)
        return jnp.einsum('hgmp,gk->hgpkm', cc, eye).reshape(2, gh * p, gh * m)

    bmat = jnp.concatenate([in_mat(bbar_re), in_mat(bbar_im)], axis=2).astype(BF16)
    cmat = jnp.stack([out_mat(c_re), out_mat(c_im)], axis=1).astype(BF16)
    return (bmat, a_re.reshape(1, g * p), a_im.reshape(1, g * p), cmat,
            d_skip.astype(F32).reshape(1, g * m))


def _colsum(x):
    return jnp.sum(x.reshape(x.shape[0] // 8, 8, x.shape[1]), axis=0)


def _colmax(x):
    return jnp.max(x.reshape(x.shape[0] // 8, 8, x.shape[1]), axis=0)


def _colmin(x):
    return jnp.min(x.reshape(x.shape[0] // 8, 8, x.shape[1]), axis=0)


def _dsa_kernel(top_k, qidxT_ref, qT_ref, wT_ref, kidx_ref, k_ref, vT_ref, kidxm_ref, km_ref, vTm_ref,
                bias_ref, biasm_ref, tri_ref, trim_ref, y_ref,
                sc_ref, scm_ref, m_ref, acc_ref, out_ref, lg_ref, p_ref):
    i = pl.program_id(1)
    tq = qT_ref.shape[1]
    tk = tq
    nkb = i + 1
    n_heads = qT_ref.shape[0] // HEAD_DIM
    kf = float(top_k)

    def kblock(kb):
        return pl.ds(pl.multiple_of(kb * tk, tk), tk)

    kc = lax.broadcasted_iota(jnp.int32, (tk, tq), 0) // CHUNK
    qc = lax.broadcasted_iota(jnp.int32, (tk, tq), 1) // CHUNK
    adm_diag = kc <= qc

    def idx_score(kidx):
        s = jnp.zeros((kidx.shape[0], tq), F32)
        for h in range(IDX_HEADS):
            sh = _dot(kidx, qidxT_ref[h * IDX_DIM:(h + 1) * IDX_DIM, :])
            s = s + jnp.maximum(sh, 0.0) * wT_ref[h:h + 1, :]
        return s

    sm = idx_score(kidxm_ref[...])
    scm_ref[...] = sm

    def score_body(kb, carry):
        mn, mx = carry
        s = idx_score(kidx_ref[kblock(kb), :])
        adm = jnp.logical_or(kb < i, adm_diag)
        sc_ref[kblock(kb), :] = jnp.where(adm, s, NEG)
        mn = jnp.minimum(mn, _colmin(jnp.where(adm, s, -NEG)))
        mx = jnp.maximum(mx, _colmax(jnp.where(adm, s, NEG)))
        return mn, mx

    mn, mx = lax.fori_loop(0, nkb, score_body, (_colmin(sm), _colmax(sm)))
    mn = jnp.min(mn, axis=0, keepdims=True)
    mx = jnp.max(mx, axis=0, keepdims=True)

    def reduce_keys(fn, combine, finish):
        def body(kb, acc):
            base = pl.multiple_of(kb * tk, tk)
            for r in range(0, tk, SCAN_ROWS):
                acc = combine(acc, fn(sc_ref[pl.ds(base + r, SCAN_ROWS), :]))
            return acc

        return finish(lax.fori_loop(0, nkb, body, fn(scm_ref[...])))

    def count_gt(t):
        return reduce_keys(lambda s: _colsum(jnp.where(s > t, 1.0, 0.0)), jnp.add,
                           lambda a: jnp.sum(a, axis=0, keepdims=True))

    n_adm = (N_META + i * tk + (lax.broadcasted_iota(jnp.int32, (1, tq), 1) // CHUNK + 1) * CHUNK).astype(F32)
    c_pos, c_nonneg = reduce_keys(
        lambda s: (_colsum(jnp.where(s > 0.0, 1.0, 0.0)), _colsum(jnp.where(s >= 0.0, 1.0, 0.0))),
        lambda a, b: (a[0] + b[0], a[1] + b[1]),
        lambda a: (jnp.sum(a[0], axis=0, keepdims=True), jnp.sum(a[1], axis=0, keepdims=True)))
    zero_vk = jnp.logical_and(c_pos < kf, c_nonneg >= kf)
    pos = c_pos >= kf
    lo0 = jnp.where(pos, 0.0, mn - (1.0 + jnp.abs(mn)))
    hi0 = jnp.where(pos, mx, 0.0)
    clo0 = jnp.where(pos, c_pos, n_adm)

    def open_rows(clo):
        return jnp.max(jnp.where(zero_vk, 0.0, clo)) > kf

    def search_cond(carry):
        it, _, _, _, go = carry
        return jnp.logical_and(it < BISECT_STEPS, go)

    def halve(carry):
        lo, hi, clo = carry
        mid = 0.5 * lo + 0.5 * hi
        c = count_gt(mid)
        up = c >= kf
        return jnp.where(up, mid, lo), jnp.where(up, hi, mid), jnp.where(up, c, clo)

    def search_body(carry):
        it, lo, hi, clo, _ = carry
        lo, hi, clo = halve(halve((lo, hi, clo)))
        return it + 2, lo, hi, clo, open_rows(clo)

    lo, hi, clo = lax.fori_loop(0, SEARCH_WARMUP, lambda _, c: halve(c), (lo0, hi0, clo0))
    _, lo, _, _, _ = lax.while_loop(search_cond, search_body, (SEARCH_WARMUP, lo, hi, clo, open_rows(clo)))

    thr = reduce_keys(lambda s: _colmin(jnp.where(s > lo, s, -NEG)), jnp.minimum,
                      lambda a: jnp.min(a, axis=0, keepdims=True))
    thr = jnp.where(zero_vk, 0.0, thr)
    need = kf - count_gt(thr)

    def sel_mask(s, tri, carry):
        eq = jnp.where(s == thr, 1.0, 0.0)
        rank = carry + _dot(tri, eq.astype(BF16))
        sel = jnp.logical_or(s > thr, jnp.logical_and(s == thr, rank < need))
        return (jnp.where(sel, 0.0, NEG).astype(BF16),
                carry + jnp.sum(_colsum(eq), axis=0, keepdims=True))

    v_rows = vT_ref.shape[0]

    def head_rows(h):
        return slice(h * HEAD_DIM, (h + 1) * HEAD_DIM)

    def acc_rows(h):
        return slice(h * v_rows, (h + 1) * v_rows)

    def colmax16(t):
        t = jnp.max(t.reshape(t.shape[0] // 16, 16, t.shape[1]), axis=0)
        return jnp.max(t, axis=0, keepdims=True).astype(F32)

    mskm, carry0 = sel_mask(scm_ref[...], trim_ref[...], jnp.zeros((1, tq), F32))
    km = km_ref[...]
    vTm = vTm_ref[...]
    lgs = [_dot(km, qT_ref[head_rows(h), :]).astype(BF16) for h in range(n_heads)]
    ps = []
    for h in range(n_heads):
        t = lgs[h] + biasm_ref[h] + mskm
        m = colmax16(t)
        m_ref[h:h + 1, :] = m
        ps.append(jnp.exp(t - m.astype(BF16)))
    for h in range(n_heads):
        acc_ref[acc_rows(h), :] = _dot(vTm, ps[h])

    def att_body(kb, carry):
        msk, carry = sel_mask(sc_ref[kblock(kb), :], tri_ref[...], carry)
        which = jnp.clip(kb - i + 2, 0, 2)
        kblk = k_ref[kblock(kb), :]
        vblk = vT_ref[:, kblock(kb)]
        for h in range(n_heads):
            lg_ref[h] = _dot(kblk, qT_ref[head_rows(h), :]).astype(BF16)
        alphas = []
        for h in range(n_heads):
            t = lg_ref[h] + bias_ref[h, which] + msk
            m_old = m_ref[h:h + 1, :]
            m_new = jnp.maximum(m_old, colmax16(t))
            m_ref[h:h + 1, :] = m_new
            p_ref[h] = jnp.exp(t - m_new.astype(BF16))
            alphas.append(jnp.exp(m_old - m_new))
        for h in range(n_heads):
            acc_ref[acc_rows(h), :] = alphas[h] * acc_ref[acc_rows(h), :] + _dot(vblk, p_ref[h])
        return carry

    lax.fori_loop(0, nkb, att_body, carry0)

    for h in range(n_heads):
        a = acc_ref[acc_rows(h), :]
        out_ref[head_rows(h), :] = a[:HEAD_DIM] / a[HEAD_DIM:HEAD_DIM + 1]
    y_ref[...] = out_ref[...].T.astype(BF16)


def _rel_bucket_np(rel):
    half = REL_BUCKETS // 2
    max_exact = half // 2
    base = np.where(rel > 0, half, 0)
    n = np.abs(rel)
    nf = np.maximum(n, 1).astype(np.float64)
    large = max_exact + (np.log(nf / max_exact) / math.log(REL_MAX_DIST / max_exact)
                         * (half - max_exact)).astype(np.int32)
    large = np.minimum(large, half - 1)
    return base + np.where(n < max_exact, n, large)


def _bias_of_rel(rb, rel, rel_lo, rel_hi):
    rels = np.arange(rel_lo, rel_hi + 1)
    buckets = _rel_bucket_np(rels)
    col = lambda b: rb[int(b)].reshape((-1,) + (1,) * rel.ndim)
    val = jnp.broadcast_to(col(buckets[0]), (rb.shape[1],) + rel.shape)
    for j in range(1, len(rels)):
        if buckets[j] != buckets[j - 1]:
            val = jnp.where(rel[None] >= int(rels[j]), col(buckets[j]), val)
    return val


def _bias_tables(rel_bias, tq, seq):
    rb = rel_bias.astype(F32)
    far = -2 * tq
    assert np.all(_rel_bucket_np(np.arange(-(seq + N_META), -tq)) == _rel_bucket_np(np.array(far)))
    d = lax.broadcasted_iota(jnp.int32, (tq, tq), 0) - lax.broadcasted_iota(jnp.int32, (tq, tq), 1)
    rel = jnp.stack([jnp.full((tq, tq), far, jnp.int32), d - tq, d])
    bias = _bias_of_rel(rb, rel, far, tq - 1)
    rel_m = (lax.broadcasted_iota(jnp.int32, (N_META, seq), 0)
             - lax.broadcasted_iota(jnp.int32, (N_META, seq), 1) - N_META)
    bias_m = _bias_of_rel(rb, rel_m, -(seq + N_META), -1)
    return bias.astype(BF16), bias_m.astype(BF16)


def _dsa(qidxT, qT, wT, kidx, k, vT, kidx_m, k_m, vT_m, bias, bias_m, nb, seq):
    tq = DSA_TILE
    nq = seq // tq
    n_att = qT.shape[0]
    n_heads = n_att // HEAD_DIM
    tri = jnp.asarray(np.tril(np.ones((tq, tq), np.float32), -1), BF16)
    tri_m = jnp.asarray(np.tril(np.ones((N_META, N_META), np.float32), -1), BF16)
    qcol = lambda b, i: (0, b * nq + i)
    return pl.pallas_call(
        functools.partial(_dsa_kernel, min(TOPK_MAX, seq // 4)),
        out_shape=jax.ShapeDtypeStruct((nb * seq, n_att), BF16),
        grid=(nb, nq),
        in_specs=[
            pl.BlockSpec((qidxT.shape[0], tq), qcol),
            pl.BlockSpec((n_att, tq), qcol),
            pl.BlockSpec((IDX_HEADS, tq), qcol),
            pl.BlockSpec((seq, IDX_DIM), lambda b, i: (b, 0)),
            pl.BlockSpec((seq, HEAD_DIM), lambda b, i: (b, 0)),
            pl.BlockSpec((V_ROWS, seq), lambda b, i: (0, b)),
            _const_spec(kidx_m.shape),
            _const_spec(k_m.shape),
            _const_spec(vT_m.shape),
            _const_spec(bias.shape),
            pl.BlockSpec((bias_m.shape[0], N_META, tq), lambda b, i: (0, 0, i)),
            _const_spec(tri.shape),
            _const_spec(tri_m.shape),
        ],
        out_specs=pl.BlockSpec((tq, n_att), lambda b, i: (b * nq + i, 0)),
        scratch_shapes=[
            pltpu.VMEM((seq, tq), F32),
            pltpu.VMEM((N_META, tq), F32),
            pltpu.VMEM((n_heads, tq), F32),
            pltpu.VMEM((n_heads * V_ROWS, tq), F32),
            pltpu.VMEM((n_att, tq), F32),
            pltpu.VMEM((n_heads, tq, tq), BF16),
            pltpu.VMEM((n_heads, tq, tq), BF16),
        ],
        compiler_params=_cparams(("arbitrary", "arbitrary")),
        name="dsa_mixer",
    )(qidxT, qT, wT, kidx, k, vT, kidx_m, k_m, vT_m, bias, bias_m, tri, tri_m)


def _merge_kernel(h_ref, ya_ref, yb_ref, gpre_ref, gpost_ref, wg_ref, wa_ref, wb_ref, wo_ref, o_ref):
    tile, d = h_ref.shape
    slabs = [slice(r, r + MERGE_SLAB) for r in range(0, tile, MERGE_SLAB)]
    merged = []
    for rows in slabs:
        hn = _rms(h_ref[rows, :], gpre_ref[...]).astype(BF16)
        pa = _dot(ya_ref[rows, :], wa_ref[...])
        pb = _dot(yb_ref[rows, :], wb_ref[...])
        g0 = jax.nn.sigmoid(_dot(hn, wg_ref[:, :d]))
        g1 = jax.nn.sigmoid(_dot(hn, wg_ref[:, d:]))
        merged.append((g0 * pa + g1 * pb).astype(BF16))
    for rows, m in zip(slabs, merged):
        o_ref[rows, :] = h_ref[rows, :] + _rms(_dot(m, wo_ref[...]), gpost_ref[...])


def _merge(h, ya, yb, gpre, gpost, wg, wa, wb, wo, nb, seq, tile):
    d = h.shape[1]
    sa = wa.shape[0]
    sb = wb.shape[0]
    nt = seq // tile
    tok = lambda b, i: (b * nt + i, 0)
    return pl.pallas_call(
        _merge_kernel,
        out_shape=jax.ShapeDtypeStruct(h.shape, F32),
        grid=(nb, nt),
        in_specs=[
            pl.BlockSpec((tile, d), tok),
            pl.BlockSpec((tile, sa), tok),
            pl.BlockSpec((tile, sb), tok),
            _const_spec((1, d)),
            _const_spec((1, d)),
            _const_spec(wg.shape),
            _const_spec(wa.shape),
            _const_spec(wb.shape),
            _const_spec(wo.shape),
        ],
        out_specs=pl.BlockSpec((tile, d), tok),
        compiler_params=_cparams(("arbitrary", "arbitrary")),
        name="mixer_merge",
    )(h, ya, yb, gpre, gpost, wg, wa, wb, wo)


def kernel(x, meta_tokens, ff1_norm_pre, ff1_norm_post, mix_norm_pre, mix_norm_post, ff2_norm_pre, ff2_norm_post, ff1_w_gate, ff1_w_up, ff1_w_down, ff2_w_gate, ff2_w_up, ff2_w_down, w_in, ssm_lambda_re, ssm_lambda_im, ssm_log_dt, ssm_b_re, ssm_b_im, ssm_c_re, ssm_c_im, ssm_d, ssm_w_glu, w_branch_a, rel_bias, w_branch_b, w_out):
    nb, seq, d = x.shape
    depth = w_in.shape[0]
    assert depth == 1, "meta-token rows are only carried as keys/state of a single layer"
    assert meta_tokens.shape[0] == N_META
    su = ssm_w_glu.shape[1]
    n_att = w_branch_b.shape[1]
    row = lambda v: v.astype(F32).reshape(1, -1)
    bf = lambda w: w.astype(BF16)

    o_u, o_qi = 0, su
    o_ki = o_qi + IDX_HEADS * IDX_DIM
    o_wi = o_ki + IDX_DIM
    o_q = o_wi + IDX_HEADS
    o_k = o_q + n_att
    o_v = o_k + HEAD_DIM
    o_g = o_v + HEAD_DIM
    win = w_in[0]
    w_nn = bf(jnp.concatenate([win[:, o_u:o_qi], win[:, o_ki:o_wi], win[:, o_k:o_v]], axis=1))
    w_t = bf(jnp.concatenate([win[:, o_qi:o_ki], win[:, o_q:o_k], win[:, o_v:o_g], win[:, o_wi:o_q]], axis=1).T)
    w_gates = bf(win[:, o_g:])

    hx = x.reshape(nb * seq, d)
    pad = META_PAD
    hm = jnp.zeros((pad, d), x.dtype).at[:N_META].set(meta_tokens.astype(x.dtype))

    ffn1 = (row(ff1_norm_pre[0]), row(ff1_norm_post[0]), bf(ff1_w_gate[0]), bf(ff1_w_up[0]), bf(ff1_w_down[0]))
    ffn2 = (row(ff2_norm_pre[0]), row(ff2_norm_post[0]), bf(ff2_w_gate[0]), bf(ff2_w_up[0]), bf(ff2_w_down[0]))

    h1 = _ffn(hx, *ffn1, FFN_TILE)
    h1m = _ffn(hm, *ffn1, pad)

    g_mix = row(mix_norm_pre[0])
    u, kidx, k, qidxT, qT, vT, wT = _proj(h1, g_mix, w_nn, w_t, nb, seq, PROJ_TILE)
    u_m, kidx_m, k_m, _, _, vT_m, _ = _proj(h1m, g_mix, w_nn, w_t, 1, pad, pad)

    bmat, a_re, a_im, cmat, d_skip = _s5_params(
        ssm_lambda_re[0], ssm_lambda_im[0], ssm_log_dt[0], ssm_b_re[0], ssm_b_im[0],
        ssm_c_re[0], ssm_c_im[0], ssm_d[0])
    ya = _s5(u.reshape(nb, seq, su), u_m[:N_META], bmat, a_re, a_im, cmat, d_skip, bf(ssm_w_glu[0]))
    ya = ya.reshape(nb * seq, su)

    bias, bias_m = _bias_tables(rel_bias, DSA_TILE, seq)
    yb = _dsa(qidxT, qT, wT, kidx, k, vT, kidx_m[:N_META], k_m[:N_META], vT_m[:, :N_META],
              bias, bias_m, nb, seq)

    h2 = _merge(h1, ya, yb, g_mix, row(mix_norm_post[0]), w_gates, bf(w_branch_a[0]),
                bf(w_branch_b[0]), bf(w_out[0]), nb, seq, MERGE_TILE)

    out = _ffn(h2, *ffn2, FFN_TILE)
    return out.reshape(nb, seq, d)
```

```python
import functools
import math

import numpy as np
import jax
import jax.numpy as jnp
from jax import lax
from jax.experimental import pallas as pl
from jax.experimental.pallas import tpu as pltpu

F32 = jnp.float32
BF16 = jnp.bfloat16

RMS_EPS = 1e-6
CHUNK = 64
N_META = 16
SSM_GROUP = 16
SSM_STATE = 64
HEAD_DIM = 64
V_ROWS = HEAD_DIM + 16
IDX_DIM = 64
IDX_HEADS = 8
TOPK_MAX = 256
REL_BUCKETS = 32
REL_MAX_DIST = 128
NEG = -1e30
BISECT_STEPS = 64
SEARCH_WARMUP = 12

VMEM_LIMIT = 56 * 1024 * 1024

FFN_TILE = 1024
FFN_SLAB = 512
FFN_CHUNK = 256
PROJ_TILE = 1024
PROJ_SLAB = 512
META_PAD = 128
S5_STEPS = 32
S5_LANES = 512
DSA_TILE = 256
SCAN_ROWS = 64
MERGE_TILE = 1024
MERGE_SLAB = 512


def _cparams(sem):
    return pltpu.CompilerParams(dimension_semantics=sem, vmem_limit_bytes=VMEM_LIMIT)


def _const_spec(shape):
    nd = len(shape)
    return pl.BlockSpec(shape, lambda *_: (0,) * nd)


def _resident_spec(shape):
    nd = len(shape)
    return pl.BlockSpec(shape, lambda *_: (0,) * nd, pipeline_mode=pl.Buffered(1))


def _rms(x, g):
    return x * lax.rsqrt(jnp.mean(x * x, axis=-1, keepdims=True) + RMS_EPS) * g


def _dot(a, b):
    return jnp.dot(a, b, preferred_element_type=F32)


def _dot_nt(a, b):
    return lax.dot_general(a, b, (((1,), (1,)), ((), ())), preferred_element_type=F32)


def _ffn_kernel(x_ref, gpre_ref, gpost_ref, wg_ref, wu_ref, wd_ref, o_ref, acc_ref):
    n_sub, sub, _ = acc_ref.shape
    n_chunks = wg_ref.shape[1] // FFN_CHUNK
    for s in range(n_sub):
        rows = slice(s * sub, (s + 1) * sub)
        x = x_ref[rows, :]
        hn = _rms(x, gpre_ref[...]).astype(BF16)
        for c in range(n_chunks):
            cols = slice(c * FFN_CHUNK, (c + 1) * FFN_CHUNK)
            g = _dot(hn, wg_ref[:, cols])
            u = _dot(hn, wu_ref[:, cols])
            a = (jax.nn.silu(g) * u).astype(BF16)
            part = _dot(a, wd_ref[cols, :])
            if c == 0:
                acc_ref[s] = part
            else:
                acc_ref[s] += part
        o_ref[rows, :] = x + 0.5 * _rms(acc_ref[s], gpost_ref[...])


def _ffn(x, gpre, gpost, wg, wu, wd, tile):
    t, d = x.shape
    dff = wg.shape[1]
    sub = min(tile, FFN_SLAB)
    return pl.pallas_call(
        _ffn_kernel,
        out_shape=jax.ShapeDtypeStruct((t, d), F32),
        grid=(t // tile,),
        in_specs=[
            pl.BlockSpec((tile, d), lambda i: (i, 0)),
            _const_spec((1, d)),
            _const_spec((1, d)),
            _resident_spec((d, dff)),
            _resident_spec((d, dff)),
            _resident_spec((dff, d)),
        ],
        out_specs=pl.BlockSpec((tile, d), lambda i: (i, 0)),
        scratch_shapes=[pltpu.VMEM((tile // sub, sub, d), F32)],
        compiler_params=_cparams(("arbitrary",)),
        name="ffn_half_step",
    )(x, gpre, gpost, wg, wu, wd)


def _proj_kernel(h_ref, g_ref, wn_ref, wt_ref, u_ref, kidx_ref, k_ref, qidxT_ref, qT_ref, vT_ref, wT_ref):
    tile = h_ref.shape[0]
    slab = min(tile, PROJ_SLAB)
    su = u_ref.shape[1]
    nq = qidxT_ref.shape[0]
    na = qT_ref.shape[0]
    for r in range(0, tile, slab):
        rows = slice(r, r + slab)
        hn = _rms(h_ref[rows, :], g_ref[...]).astype(BF16)
        nn = _dot(hn, wn_ref[...])
        u_ref[rows, :] = nn[:, :su].astype(BF16)
        kidx_ref[rows, :] = nn[:, su:su + IDX_DIM].astype(BF16)
        k_ref[rows, :] = nn[:, su + IDX_DIM:su + IDX_DIM + HEAD_DIM].astype(BF16)
        tt = _dot_nt(wt_ref[...], hn)
        qidxT_ref[:, rows] = (tt[:nq] * (IDX_DIM ** -0.5)).astype(BF16)
        qT_ref[:, rows] = (tt[nq:nq + na] * (HEAD_DIM ** -0.5)).astype(BF16)
        ones = jnp.ones((V_ROWS - HEAD_DIM, slab), F32)
        vT_ref[:, rows] = jnp.concatenate([tt[nq + na:nq + na + HEAD_DIM], ones], axis=0).astype(BF16)
        wT_ref[:, rows] = tt[nq + na + HEAD_DIM:nq + na + HEAD_DIM + IDX_HEADS] * (IDX_HEADS ** -0.5)


def _proj(h, gain, w_nn, w_t, nb, seq, tile):
    d = h.shape[1]
    su = w_nn.shape[1] - IDX_DIM - HEAD_DIM
    nq = IDX_HEADS * IDX_DIM
    na = w_t.shape[0] - nq - HEAD_DIM - IDX_HEADS
    nt = seq // tile
    tok = lambda b, i: (b * nt + i, 0)
    tokT = lambda b, i: (0, b * nt + i)
    out_shape = (
        jax.ShapeDtypeStruct((nb * seq, su), BF16),
        jax.ShapeDtypeStruct((nb * seq, IDX_DIM), BF16),
        jax.ShapeDtypeStruct((nb * seq, HEAD_DIM), BF16),
        jax.ShapeDtypeStruct((nq, nb * seq), BF16),
        jax.ShapeDtypeStruct((na, nb * seq), BF16),
        jax.ShapeDtypeStruct((V_ROWS, nb * seq), BF16),
        jax.ShapeDtypeStruct((IDX_HEADS, nb * seq), F32),
    )
    out_specs = (
        pl.BlockSpec((tile, su), tok),
        pl.BlockSpec((tile, IDX_DIM), tok),
        pl.BlockSpec((tile, HEAD_DIM), tok),
        pl.BlockSpec((nq, tile), tokT),
        pl.BlockSpec((na, tile), tokT),
        pl.BlockSpec((V_ROWS, tile), tokT),
        pl.BlockSpec((IDX_HEADS, tile), tokT),
    )
    return pl.pallas_call(
        _proj_kernel,
        out_shape=out_shape,
        grid=(nb, nt),
        in_specs=[
            pl.BlockSpec((tile, d), tok),
            _const_spec((1, d)),
            _const_spec(w_nn.shape),
            _const_spec(w_t.shape),
        ],
        out_specs=out_specs,
        compiler_params=_cparams(("arbitrary", "arbitrary")),
        name="mixer_in_proj",
    )(h, gain, w_nn, w_t)


def _s5_scan(bu_ref, xs_ref, st_ref, are_ref, aim_ref, chunks, n_steps, nb, bcast_rows):
    n_state = are_ref.shape[1]
    for c in chunks:
        lanes = slice(c * S5_LANES, (c + 1) * S5_LANES)
        lanes_im = slice(n_state + c * S5_LANES, n_state + (c + 1) * S5_LANES)
        a_re = jnp.broadcast_to(are_ref[:, lanes], (nb, S5_LANES))
        a_im = jnp.broadcast_to(aim_ref[:, lanes], (nb, S5_LANES))
        x_re = st_ref[0, :, lanes]
        x_im = st_ref[1, :, lanes]
        for t in range(n_steps):
            rows = slice(t, t + 1) if bcast_rows else slice(t * nb, (t + 1) * nb)
            b_re = bu_ref[rows, lanes]
            b_im = bu_ref[rows, lanes_im]
            x_re, x_im = (a_re * x_re - a_im * x_im + b_re,
                          a_re * x_im + a_im * x_re + b_im)
            if xs_ref is not None:
                xs_ref[rows, lanes] = x_re.astype(BF16)
                xs_ref[rows, lanes_im] = x_im.astype(BF16)
        st_ref[0, :, lanes] = x_re
        st_ref[1, :, lanes] = x_im


def _s5_kernel(u_ref, um_ref, bmat_ref, are_ref, aim_ref, cmat_ref, d_ref, wglu_ref, y_ref,
               bu_ref, xs_ref, st_ref, bum_ref):
    nb, n_steps, su = u_ref.shape
    n_state = are_ref.shape[1]
    half_u = su // 2
    half_s = n_state // 2
    chunks_per_half = half_s // S5_LANES

    def b_project(u, dst_ref):
        for hf in range(2):
            uu = u[:, hf * half_u:(hf + 1) * half_u]
            bu = _dot(uu, bmat_ref[hf])
            dst_ref[:, hf * half_s:(hf + 1) * half_s] = bu[:, :half_s]
            dst_ref[:, n_state + hf * half_s:n_state + (hf + 1) * half_s] = bu[:, half_s:]

    @pl.when(pl.program_id(0) == 0)
    def _():
        st_ref[...] = jnp.zeros_like(st_ref)
        b_project(um_ref[...], bum_ref)
        _s5_scan(bum_ref, None, st_ref, are_ref, aim_ref, range(2 * chunks_per_half), um_ref.shape[0], nb, True)

    u = pltpu.einshape("btc->tbc", u_ref[...]).reshape(n_steps * nb, su)
    b_project(u, bu_ref)
    ys = []
    for hf in range(2):
        _s5_scan(bu_ref, xs_ref, st_ref, are_ref, aim_ref,
                 range(hf * chunks_per_half, (hf + 1) * chunks_per_half), n_steps, nb, False)
        y_re = _dot(xs_ref[:, hf * half_s:(hf + 1) * half_s], cmat_ref[hf, 0])
        y_im = _dot(xs_ref[:, n_state + hf * half_s:n_state + (hf + 1) * half_s], cmat_ref[hf, 1])
        ys.append(y_re - y_im)
    y = jnp.concatenate(ys, axis=1) + d_ref[...] * u.astype(F32)
    y = jax.nn.gelu(y)
    y = y * jax.nn.sigmoid(_dot(y.astype(BF16), wglu_ref[...]))
    y_ref[...] = pltpu.einshape("tbc->btc", y.astype(BF16).reshape(n_steps, nb, su))


def _s5(u, u_meta, bmat, a_re, a_im, cmat, d_skip, w_glu):
    nb, seq, su = u.shape
    rows = S5_STEPS * nb
    n_state = a_re.shape[1]
    return pl.pallas_call(
        _s5_kernel,
        out_shape=jax.ShapeDtypeStruct((nb, seq, su), BF16),
        grid=(seq // S5_STEPS,),
        in_specs=[
            pl.BlockSpec((nb, S5_STEPS, su), lambda i: (0, i, 0)),
            _const_spec(u_meta.shape),
            _const_spec(bmat.shape),
            _const_spec(a_re.shape),
            _const_spec(a_im.shape),
            _const_spec(cmat.shape),
            _const_spec(d_skip.shape),
            _const_spec(w_glu.shape),
        ],
        out_specs=pl.BlockSpec((nb, S5_STEPS, su), lambda i: (0, i, 0)),
        scratch_shapes=[
            pltpu.VMEM((rows, 2 * n_state), F32),
            pltpu.VMEM((rows, 2 * n_state), BF16),
            pltpu.VMEM((2, nb, n_state), F32),
            pltpu.VMEM((u_meta.shape[0], 2 * n_state), F32),
        ],
        compiler_params=_cparams(("arbitrary",)),
        name="s5_mixer",
    )(u, u_meta, bmat, a_re, a_im, cmat, d_skip, w_glu)


def _s5_params(lambda_re, lambda_im, log_dt, b_re, b_im, c_re, c_im, d_skip):
    g, p, m = b_re.shape
    lr = lambda_re.astype(F32)
    li = lambda_im.astype(F32)
    dt = jnp.exp(log_dt.astype(F32))[:, None]
    mag = jnp.exp(lr * dt)
    a_re = mag * jnp.cos(li * dt)
    a_im = mag * jnp.sin(li * dt)
    den = lr * lr + li * li
    num_re = a_re - 1.0
    num_im = a_im
    coef_re = (num_re * lr + num_im * li) / den
    coef_im = (num_im * lr - num_re * li) / den
    br = b_re.astype(F32)
    bi = b_im.astype(F32)
    bbar_re = coef_re[..., None] * br - coef_im[..., None] * bi
    bbar_im = coef_re[..., None] * bi + coef_im[..., None] * br
    gh = g // 2
    eye = jnp.eye(gh, dtype=F32)

    def in_mat(bb):
        bb = bb.reshape(2, gh, p, m)
        return jnp.einsum('hgpm,gk->hgmkp', bb, eye).reshape(2, gh * m, gh * p)

    def out_mat(cc):
        cc = cc.astype(F32).reshape(2, gh, m, p)
        return jnp.einsum('hgmp,gk->hgpkm', cc, eye).reshape(2, gh * p, gh * m)

    bmat = jnp.concatenate([in_mat(bbar_re), in_mat(bbar_im)], axis=2).astype(BF16)
    cmat = jnp.stack([out_mat(c_re), out_mat(c_im)], axis=1).astype(BF16)
    return (bmat, a_re.reshape(1, g * p), a_im.reshape(1, g * p), cmat,
            d_skip.astype(F32).reshape(1, g * m))


def _colsum(x):
    return jnp.sum(x.reshape(x.shape[0] // 8, 8, x.shape[1]), axis=0)


def _colmax(x):
    return jnp.max(x.reshape(x.shape[0] // 8, 8, x.shape[1]), axis=0)


def _colmin(x):
    return jnp.min(x.reshape(x.shape[0] // 8, 8, x.shape[1]), axis=0)


def _dsa_kernel(top_k, qidxT_ref, qT_ref, wT_ref, kidx_ref, k_ref, vT_ref, kidxm_ref, km_ref, vTm_ref,
                bias_ref, biasm_ref, tri_ref, trim_ref, y_ref,
                sc_ref, scm_ref, m_ref, acc_ref, out_ref, lg_ref, p_ref, lg2_ref, p2_ref):
    i = pl.program_id(1)
    tq = qT_ref.shape[1]
    tk = tq
    nkb = i + 1
    n_heads = qT_ref.shape[0] // HEAD_DIM
    kf = float(top_k)

    def kblock(kb):
        return pl.ds(pl.multiple_of(kb * tk, tk), tk)

    kc = lax.broadcasted_iota(jnp.int32, (tk, tq), 0) // CHUNK
    qc = lax.broadcasted_iota(jnp.int32, (tk, tq), 1) // CHUNK
    adm_diag = kc <= qc

    def idx_score(kidx):
        s = jnp.zeros((kidx.shape[0], tq), F32)
        for h in range(IDX_HEADS):
            sh = _dot(kidx, qidxT_ref[h * IDX_DIM:(h + 1) * IDX_DIM, :])
            s = s + jnp.maximum(sh, 0.0) * wT_ref[h:h + 1, :]
        return s

    sm = idx_score(kidxm_ref[...])
    scm_ref[...] = sm

    def score_body(kb, carry):
        mn, mx = carry
        s = idx_score(kidx_ref[kblock(kb), :])
        adm = jnp.logical_or(kb < i, adm_diag)
        sc_ref[kblock(kb), :] = jnp.where(adm, s, NEG)
        mn = jnp.minimum(mn, _colmin(jnp.where(adm, s, -NEG)))
        mx = jnp.maximum(mx, _colmax(jnp.where(adm, s, NEG)))
        return mn, mx

    carry = lax.fori_loop(0, nkb // 2, lambda j, c: score_body(2 * j + 1, score_body(2 * j, c)),
                          (_colmin(sm), _colmax(sm)))
    mn, mx = lax.cond(nkb % 2 == 1, lambda c: score_body(nkb - 1, c), lambda c: c, carry)
    mn = jnp.min(mn, axis=0, keepdims=True)
    mx = jnp.max(mx, axis=0, keepdims=True)

    def reduce_keys(fn, combine, finish):
        def body(kb, acc):
            base = pl.multiple_of(kb * tk, tk)
            for r in range(0, tk, SCAN_ROWS):
                acc = combine(acc, fn(sc_ref[pl.ds(base + r, SCAN_ROWS), :]))
            return acc

        return finish(lax.fori_loop(0, nkb, body, fn(scm_ref[...])))

    def count_gt(t):
        return reduce_keys(lambda s: _colsum(jnp.where(s > t, 1.0, 0.0)), jnp.add,
                           lambda a: jnp.sum(a, axis=0, keepdims=True))

    n_adm = (N_META + i * tk + (lax.broadcasted_iota(jnp.int32, (1, tq), 1) // CHUNK + 1) * CHUNK).astype(F32)
    c_pos, c_nonneg = reduce_keys(
        lambda s: (_colsum(jnp.where(s > 0.0, 1.0, 0.0)), _colsum(jnp.where(s >= 0.0, 1.0, 0.0))),
        lambda a, b: (a[0] + b[0], a[1] + b[1]),
        lambda a: (jnp.sum(a[0], axis=0, keepdims=True), jnp.sum(a[1], axis=0, keepdims=True)))
    zero_vk = jnp.logical_and(c_pos < kf, c_nonneg >= kf)
    pos = c_pos >= kf
    lo0 = jnp.where(pos, 0.0, mn - (1.0 + jnp.abs(mn)))
    hi0 = jnp.where(pos, mx, 0.0)
    clo0 = jnp.where(pos, c_pos, n_adm)

    def open_rows(clo):
        return jnp.max(jnp.where(zero_vk, 0.0, clo)) > kf

    def search_cond(carry):
        it, _, _, _, go = carry
        return jnp.logical_and(it < BISECT_STEPS, go)

    def halve(carry):
        lo, hi, clo = carry
        mid = 0.5 * lo + 0.5 * hi
        c = count_gt(mid)
        up = c >= kf
        return jnp.where(up, mid, lo), jnp.where(up, hi, mid), jnp.where(up, c, clo)

    def search_body(carry):
        it, lo, hi, clo, _ = carry
        lo, hi, clo = halve(halve((lo, hi, clo)))
        return it + 2, lo, hi, clo, open_rows(clo)

    lo, hi, clo = lax.fori_loop(0, SEARCH_WARMUP, lambda _, c: halve(c), (lo0, hi0, clo0))
    _, lo, _, _, _ = lax.while_loop(search_cond, search_body, (SEARCH_WARMUP, lo, hi, clo, open_rows(clo)))

    thr = reduce_keys(lambda s: _colmin(jnp.where(s > lo, s, -NEG)), jnp.minimum,
                      lambda a: jnp.min(a, axis=0, keepdims=True))
    thr = jnp.where(zero_vk, 0.0, thr)
    need = kf - count_gt(thr)

    def sel_mask(s, tri, carry):
        eq = jnp.where(s == thr, 1.0, 0.0)
        rank = carry + _dot(tri, eq.astype(BF16))
        sel = jnp.logical_or(s > thr, jnp.logical_and(s == thr, rank < need))
        return (jnp.where(sel, 0.0, NEG).astype(BF16),
                carry + jnp.sum(_colsum(eq), axis=0, keepdims=True))

    v_rows = vT_ref.shape[0]

    def head_rows(h):
        return slice(h * HEAD_DIM, (h + 1) * HEAD_DIM)

    def acc_rows(h):
        return slice(h * v_rows, (h + 1) * v_rows)

    def colmax16(t):
        t = jnp.max(t.reshape(t.shape[0] // 16, 16, t.shape[1]), axis=0)
        return jnp.max(t, axis=0, keepdims=True).astype(F32)

    mskm, carry0 = sel_mask(scm_ref[...], trim_ref[...], jnp.zeros((1, tq), F32))
    km = km_ref[...]
    vTm = vTm_ref[...]
    lgs = [_dot(km, qT_ref[head_rows(h), :]).astype(BF16) for h in range(n_heads)]
    ps = []
    for h in range(n_heads):
        t = lgs[h] + biasm_ref[h] + mskm
        m = colmax16(t)
        m_ref[h:h + 1, :] = m
        ps.append(jnp.exp(t - m.astype(BF16)))
    for h in range(n_heads):
        acc_ref[acc_rows(h), :] = _dot(vTm, ps[h])

    def att_body(kb, carry, lg_ref=lg_ref, p_ref=p_ref):
        msk, carry = sel_mask(sc_ref[kblock(kb), :], tri_ref[...], carry)
        which = jnp.clip(kb - i + 2, 0, 2)
        kblk = k_ref[kblock(kb), :]
        vblk = vT_ref[:, kblock(kb)]
        for h in range(n_heads):
            lg_ref[h] = _dot(kblk, qT_ref[head_rows(h), :]).astype(BF16)
        alphas = []
        for h in range(n_heads):
            t = lg_ref[h] + bias_ref[h, which] + msk
            m_old = m_ref[h:h + 1, :]
            m_new = jnp.maximum(m_old, colmax16(t))
            m_ref[h:h + 1, :] = m_new
            p_ref[h] = jnp.exp(t - m_new.astype(BF16))
            alphas.append(jnp.exp(m_old - m_new))
        for h in range(n_heads):
            acc_ref[acc_rows(h), :] = alphas[h] * acc_ref[acc_rows(h), :] + _dot(vblk, p_ref[h])
        return carry

    carry1 = lax.fori_loop(0, nkb // 2,
                           lambda j, c: att_body(2 * j + 1, att_body(2 * j, c), lg2_ref, p2_ref), carry0)
    lax.cond(nkb % 2 == 1, lambda c: att_body(nkb - 1, c), lambda c: c, carry1)

    for h in range(n_heads):
        a = acc_ref[acc_rows(h), :]
        out_ref[head_rows(h), :] = a[:HEAD_DIM] / a[HEAD_DIM:HEAD_DIM + 1]
    y_ref[...] = out_ref[...].T.astype(BF16)


def _rel_bucket_np(rel):
    half = REL_BUCKETS // 2
    max_exact = half // 2
    base = np.where(rel > 0, half, 0)
    n = np.abs(rel)
    nf = np.maximum(n, 1).astype(np.float64)
    large = max_exact + (np.log(nf / max_exact) / math.log(REL_MAX_DIST / max_exact)
                         * (half - max_exact)).astype(np.int32)
    large = np.minimum(large, half - 1)
    return base + np.where(n < max_exact, n, large)


def _bias_of_rel(rb, rel, rel_lo, rel_hi):
    rels = np.arange(rel_lo, rel_hi + 1)
    buckets = _rel_bucket_np(rels)
    col = lambda b: rb[int(b)].reshape((-1,) + (1,) * rel.ndim)
    val = jnp.broadcast_to(col(buckets[0]), (rb.shape[1],) + rel.shape)
    for j in range(1, len(rels)):
        if buckets[j] != buckets[j - 1]:
            val = jnp.where(rel[None] >= int(rels[j]), col(buckets[j]), val)
    return val


def _bias_tables(rel_bias, tq, seq):
    rb = rel_bias.astype(F32)
    far = -2 * tq
    assert np.all(_rel_bucket_np(np.arange(-(seq + N_META), -tq)) == _rel_bucket_np(np.array(far)))
    d = lax.broadcasted_iota(jnp.int32, (tq, tq), 0) - lax.broadcasted_iota(jnp.int32, (tq, tq), 1)
    rel = jnp.stack([jnp.full((tq, tq), far, jnp.int32), d - tq, d])
    bias = _bias_of_rel(rb, rel, far, tq - 1)
    rel_m = (lax.broadcasted_iota(jnp.int32, (N_META, seq), 0)
             - lax.broadcasted_iota(jnp.int32, (N_META, seq), 1) - N_META)
    bias_m = _bias_of_rel(rb, rel_m, -(seq + N_META), -1)
    return bias.astype(BF16), bias_m.astype(BF16)


def _dsa(qidxT, qT, wT, kidx, k, vT, kidx_m, k_m, vT_m, bias, bias_m, nb, seq):
    tq = DSA_TILE
    nq = seq // tq
    n_att = qT.shape[0]
    n_heads = n_att // HEAD_DIM
    tri = jnp.asarray(np.tril(np.ones((tq, tq), np.float32), -1), BF16)
    tri_m = jnp.asarray(np.tril(np.ones((N_META, N_META), np.float32), -1), BF16)
    qcol = lambda b, i: (0, b * nq + i)
    return pl.pallas_call(
        functools.partial(_dsa_kernel, min(TOPK_MAX, seq // 4)),
        out_shape=jax.ShapeDtypeStruct((nb * seq, n_att), BF16),
        grid=(nb, nq),
        in_specs=[
            pl.BlockSpec((qidxT.shape[0], tq), qcol),
            pl.BlockSpec((n_att, tq), qcol),
            pl.BlockSpec((IDX_HEADS, tq), qcol),
            pl.BlockSpec((seq, IDX_DIM), lambda b, i: (b, 0)),
            pl.BlockSpec((seq, HEAD_DIM), lambda b, i: (b, 0)),
            pl.BlockSpec((V_ROWS, seq), lambda b, i: (0, b)),
            _const_spec(kidx_m.shape),
            _const_spec(k_m.shape),
            _const_spec(vT_m.shape),
            _const_spec(bias.shape),
            pl.BlockSpec((bias_m.shape[0], N_META, tq), lambda b, i: (0, 0, i)),
            _const_spec(tri.shape),
            _const_spec(tri_m.shape),
        ],
        out_specs=pl.BlockSpec((tq, n_att), lambda b, i: (b * nq + i, 0)),
        scratch_shapes=[
            pltpu.VMEM((seq, tq), F32),
            pltpu.VMEM((N_META, tq), F32),
            pltpu.VMEM((n_heads, tq), F32),
            pltpu.VMEM((n_heads * V_ROWS, tq), F32),
            pltpu.VMEM((n_att, tq), F32),
            pltpu.VMEM((n_heads, tq, tq), BF16),
            pltpu.VMEM((n_heads, tq, tq), BF16),
            pltpu.VMEM((n_heads, tq, tq), BF16),
            pltpu.VMEM((n_heads, tq, tq), BF16),
        ],
        compiler_params=_cparams(("arbitrary", "arbitrary")),
        name="dsa_mixer",
    )(qidxT, qT, wT, kidx, k, vT, kidx_m, k_m, vT_m, bias, bias_m, tri, tri_m)


def _merge_kernel(h_ref, ya_ref, yb_ref, gpre_ref, gpost_ref, wg_ref, wa_ref, wb_ref, wo_ref, o_ref):
    tile, d = h_ref.shape
    slabs = [slice(r, r + MERGE_SLAB) for r in range(0, tile, MERGE_SLAB)]
    merged = []
    for rows in slabs:
        hn = _rms(h_ref[rows, :], gpre_ref[...]).astype(BF16)
        pa = _dot(ya_ref[rows, :], wa_ref[...])
        pb = _dot(yb_ref[rows, :], wb_ref[...])
        g0 = jax.nn.sigmoid(_dot(hn, wg_ref[:, :d]))
        g1 = jax.nn.sigmoid(_dot(hn, wg_ref[:, d:]))
        merged.append((g0 * pa + g1 * pb).astype(BF16))
    for rows, m in zip(slabs, merged):
        o_ref[rows, :] = h_ref[rows, :] + _rms(_dot(m, wo_ref[...]), gpost_ref[...])


def _merge(h, ya, yb, gpre, gpost, wg, wa, wb, wo, nb, seq, tile):
    d = h.shape[1]
    sa = wa.shape[0]
    sb = wb.shape[0]
    nt = seq // tile
    tok = lambda b, i: (b * nt + i, 0)
    return pl.pallas_call(
        _merge_kernel,
        out_shape=jax.ShapeDtypeStruct(h.shape, F32),
        grid=(nb, nt),
        in_specs=[
            pl.BlockSpec((tile, d), tok),
            pl.BlockSpec((tile, sa), tok),
            pl.BlockSpec((tile, sb), tok),
            _const_spec((1, d)),
            _const_spec((1, d)),
            _const_spec(wg.shape),
            _const_spec(wa.shape),
            _const_spec(wb.shape),
            _const_spec(wo.shape),
        ],
        out_specs=pl.BlockSpec((tile, d), tok),
        compiler_params=_cparams(("arbitrary", "arbitrary")),
        name="mixer_merge",
    )(h, ya, yb, gpre, gpost, wg, wa, wb, wo)


def kernel(x, meta_tokens, ff1_norm_pre, ff1_norm_post, mix_norm_pre, mix_norm_post, ff2_norm_pre, ff2_norm_post, ff1_w_gate, ff1_w_up, ff1_w_down, ff2_w_gate, ff2_w_up, ff2_w_down, w_in, ssm_lambda_re, ssm_lambda_im, ssm_log_dt, ssm_b_re, ssm_b_im, ssm_c_re, ssm_c_im, ssm_d, ssm_w_glu, w_branch_a, rel_bias, w_branch_b, w_out):
    nb, seq, d = x.shape
    depth = w_in.shape[0]
    assert depth == 1, "meta-token rows are only carried as keys/state of a single layer"
    assert meta_tokens.shape[0] == N_META
    su = ssm_w_glu.shape[1]
    n_att = w_branch_b.shape[1]
    row = lambda v: v.astype(F32).reshape(1, -1)
    bf = lambda w: w.astype(BF16)

    o_u, o_qi = 0, su
    o_ki = o_qi + IDX_HEADS * IDX_DIM
    o_wi = o_ki + IDX_DIM
    o_q = o_wi + IDX_HEADS
    o_k = o_q + n_att
    o_v = o_k + HEAD_DIM
    o_g = o_v + HEAD_DIM
    win = w_in[0]
    w_nn = bf(jnp.concatenate([win[:, o_u:o_qi], win[:, o_ki:o_wi], win[:, o_k:o_v]], axis=1))
    w_t = bf(jnp.concatenate([win[:, o_qi:o_ki], win[:, o_q:o_k], win[:, o_v:o_g], win[:, o_wi:o_q]], axis=1).T)
    w_gates = bf(win[:, o_g:])

    hx = x.reshape(nb * seq, d)
    pad = META_PAD
    hm = jnp.zeros((pad, d), x.dtype).at[:N_META].set(meta_tokens.astype(x.dtype))

    ffn1 = (row(ff1_norm_pre[0]), row(ff1_norm_post[0]), bf(ff1_w_gate[0]), bf(ff1_w_up[0]), bf(ff1_w_down[0]))
    ffn2 = (row(ff2_norm_pre[0]), row(ff2_norm_post[0]), bf(ff2_w_gate[0]), bf(ff2_w_up[0]), bf(ff2_w_down[0]))

    h1 = _ffn(hx, *ffn1, FFN_TILE)
    h1m = _ffn(hm, *ffn1, pad)

    g_mix = row(mix_norm_pre[0])
    u, kidx, k, qidxT, qT, vT, wT = _proj(h1, g_mix, w_nn, w_t, nb, seq, PROJ_TILE)
    u_m, kidx_m, k_m, _, _, vT_m, _ = _proj(h1m, g_mix, w_nn, w_t, 1, pad, pad)

    bmat, a_re, a_im, cmat, d_skip = _s5_params(
        ssm_lambda_re[0], ssm_lambda_im[0], ssm_log_dt[0], ssm_b_re[0], ssm_b_im[0],
        ssm_c_re[0], ssm_c_im[0], ssm_d[0])
    ya = _s5(u.reshape(nb, seq, su), u_m[:N_META], bmat, a_re, a_im, cmat, d_skip, bf(ssm_w_glu[0]))
    ya = ya.reshape(nb * seq, su)

    bias, bias_m = _bias_tables(rel_bias, DSA_TILE, seq)
    yb = _dsa(qidxT, qT, wT, kidx, k, vT, kidx_m[:N_META], k_m[:N_META], vT_m[:, :N_META],
              bias, bias_m, nb, seq)

    h2 = _merge(h1, ya, yb, g_mix, row(mix_norm_post[0]), w_gates, bf(w_branch_a[0]),
                bf(w_branch_b[0]), bf(w_out[0]), nb, seq, MERGE_TILE)

    out = _ffn(h2, *ffn2, FFN_TILE)
    return out.reshape(nb, seq, d)
```

```python
import functools
import math

import numpy as np
import jax
import jax.numpy as jnp
from jax import lax
from jax.experimental import pallas as pl
from jax.experimental.pallas import tpu as pltpu

F32 = jnp.float32
BF16 = jnp.bfloat16

RMS_EPS = 1e-6
CHUNK = 64
N_META = 16
SSM_GROUP = 16
SSM_STATE = 64
HEAD_DIM = 64
V_ROWS = HEAD_DIM + 16
IDX_DIM = 64
IDX_HEADS = 8
TOPK_MAX = 256
REL_BUCKETS = 32
REL_MAX_DIST = 128
NEG = -1e30
BISECT_STEPS = 64
COARSE_STEPS = 10
SEARCH_WARMUP = 4

VMEM_LIMIT = 56 * 1024 * 1024

FFN_TILE = 1024
FFN_SLAB = 512
FFN_CHUNK = 256
PROJ_TILE = 1024
PROJ_SLAB = 512
META_PAD = 128
S5_STEPS = 32
S5_LANES = 512
DSA_TILE = 256
SCAN_ROWS = 64
MERGE_TILE = 1024
MERGE_SLAB = 512


def _cparams(sem):
    return pltpu.CompilerParams(dimension_semantics=sem, vmem_limit_bytes=VMEM_LIMIT)


def _const_spec(shape):
    nd = len(shape)
    return pl.BlockSpec(shape, lambda *_: (0,) * nd)


def _resident_spec(shape):
    nd = len(shape)
    return pl.BlockSpec(shape, lambda *_: (0,) * nd, pipeline_mode=pl.Buffered(1))


def _rms(x, g):
    return x * lax.rsqrt(jnp.mean(x * x, axis=-1, keepdims=True) + RMS_EPS) * g


def _dot(a, b):
    return jnp.dot(a, b, preferred_element_type=F32)


def _dot_nt(a, b):
    return lax.dot_general(a, b, (((1,), (1,)), ((), ())), preferred_element_type=F32)


def _ffn_kernel(x_ref, gpre_ref, gpost_ref, wg_ref, wu_ref, wd_ref, o_ref, acc_ref):
    n_sub, sub, _ = acc_ref.shape
    n_chunks = wg_ref.shape[1] // FFN_CHUNK
    for s in range(n_sub):
        rows = slice(s * sub, (s + 1) * sub)
        x = x_ref[rows, :]
        hn = _rms(x, gpre_ref[...]).astype(BF16)
        for c in range(n_chunks):
            cols = slice(c * FFN_CHUNK, (c + 1) * FFN_CHUNK)
            g = _dot(hn, wg_ref[:, cols])
            u = _dot(hn, wu_ref[:, cols])
            a = (jax.nn.silu(g) * u).astype(BF16)
            part = _dot(a, wd_ref[cols, :])
            if c == 0:
                acc_ref[s] = part
            else:
                acc_ref[s] += part
        o_ref[rows, :] = x + 0.5 * _rms(acc_ref[s], gpost_ref[...])


def _ffn(x, gpre, gpost, wg, wu, wd, tile):
    t, d = x.shape
    dff = wg.shape[1]
    sub = min(tile, FFN_SLAB)
    return pl.pallas_call(
        _ffn_kernel,
        out_shape=jax.ShapeDtypeStruct((t, d), F32),
        grid=(t // tile,),
        in_specs=[
            pl.BlockSpec((tile, d), lambda i: (i, 0)),
            _const_spec((1, d)),
            _const_spec((1, d)),
            _resident_spec((d, dff)),
            _resident_spec((d, dff)),
            _resident_spec((dff, d)),
        ],
        out_specs=pl.BlockSpec((tile, d), lambda i: (i, 0)),
        scratch_shapes=[pltpu.VMEM((tile // sub, sub, d), F32)],
        compiler_params=_cparams(("arbitrary",)),
        name="ffn_half_step",
    )(x, gpre, gpost, wg, wu, wd)


def _proj_kernel(h_ref, g_ref, wn_ref, wt_ref, u_ref, kidx_ref, k_ref, qidxT_ref, qT_ref, vT_ref, wT_ref):
    tile = h_ref.shape[0]
    slab = min(tile, PROJ_SLAB)
    su = u_ref.shape[1]
    nq = qidxT_ref.shape[0]
    na = qT_ref.shape[0]
    for r in range(0, tile, slab):
        rows = slice(r, r + slab)
        hn = _rms(h_ref[rows, :], g_ref[...]).astype(BF16)
        nn = _dot(hn, wn_ref[...])
        u_ref[rows, :] = nn[:, :su].astype(BF16)
        kidx_ref[rows, :] = nn[:, su:su + IDX_DIM].astype(BF16)
        k_ref[rows, :] = nn[:, su + IDX_DIM:su + IDX_DIM + HEAD_DIM].astype(BF16)
        tt = _dot_nt(wt_ref[...], hn)
        qidxT_ref[:, rows] = (tt[:nq] * (IDX_DIM ** -0.5)).astype(BF16)
        qT_ref[:, rows] = (tt[nq:nq + na] * (HEAD_DIM ** -0.5)).astype(BF16)
        ones = jnp.ones((V_ROWS - HEAD_DIM, slab), F32)
        vT_ref[:, rows] = jnp.concatenate([tt[nq + na:nq + na + HEAD_DIM], ones], axis=0).astype(BF16)
        wT_ref[:, rows] = tt[nq + na + HEAD_DIM:nq + na + HEAD_DIM + IDX_HEADS] * (IDX_HEADS ** -0.5)


def _proj(h, gain, w_nn, w_t, nb, seq, tile):
    d = h.shape[1]
    su = w_nn.shape[1] - IDX_DIM - HEAD_DIM
    nq = IDX_HEADS * IDX_DIM
    na = w_t.shape[0] - nq - HEAD_DIM - IDX_HEADS
    nt = seq // tile
    tok = lambda b, i: (b * nt + i, 0)
    tokT = lambda b, i: (0, b * nt + i)
    out_shape = (
        jax.ShapeDtypeStruct((nb * seq, su), BF16),
        jax.ShapeDtypeStruct((nb * seq, IDX_DIM), BF16),
        jax.ShapeDtypeStruct((nb * seq, HEAD_DIM), BF16),
        jax.ShapeDtypeStruct((nq, nb * seq), BF16),
        jax.ShapeDtypeStruct((na, nb * seq), BF16),
        jax.ShapeDtypeStruct((V_ROWS, nb * seq), BF16),
        jax.ShapeDtypeStruct((IDX_HEADS, nb * seq), F32),
    )
    out_specs = (
        pl.BlockSpec((tile, su), tok),
        pl.BlockSpec((tile, IDX_DIM), tok),
        pl.BlockSpec((tile, HEAD_DIM), tok),
        pl.BlockSpec((nq, tile), tokT),
        pl.BlockSpec((na, tile), tokT),
        pl.BlockSpec((V_ROWS, tile), tokT),
        pl.BlockSpec((IDX_HEADS, tile), tokT),
    )
    return pl.pallas_call(
        _proj_kernel,
        out_shape=out_shape,
        grid=(nb, nt),
        in_specs=[
            pl.BlockSpec((tile, d), tok),
            _const_spec((1, d)),
            _const_spec(w_nn.shape),
            _const_spec(w_t.shape),
        ],
        out_specs=out_specs,
        compiler_params=_cparams(("arbitrary", "arbitrary")),
        name="mixer_in_proj",
    )(h, gain, w_nn, w_t)


def _s5_scan(bu_ref, xs_ref, st_ref, are_ref, aim_ref, chunks, n_steps, nb, bcast_rows):
    n_state = are_ref.shape[1]
    for c in chunks:
        lanes = slice(c * S5_LANES, (c + 1) * S5_LANES)
        lanes_im = slice(n_state + c * S5_LANES, n_state + (c + 1) * S5_LANES)
        a_re = jnp.broadcast_to(are_ref[:, lanes], (nb, S5_LANES))
        a_im = jnp.broadcast_to(aim_ref[:, lanes], (nb, S5_LANES))
        x_re = st_ref[0, :, lanes]
        x_im = st_ref[1, :, lanes]
        for t in range(n_steps):
            rows = slice(t, t + 1) if bcast_rows else slice(t * nb, (t + 1) * nb)
            b_re = bu_ref[rows, lanes]
            b_im = bu_ref[rows, lanes_im]
            x_re, x_im = (a_re * x_re - a_im * x_im + b_re,
                          a_re * x_im + a_im * x_re + b_im)
            if xs_ref is not None:
                xs_ref[rows, lanes] = x_re.astype(BF16)
                xs_ref[rows, lanes_im] = x_im.astype(BF16)
        st_ref[0, :, lanes] = x_re
        st_ref[1, :, lanes] = x_im


def _s5_kernel(u_ref, um_ref, bmat_ref, are_ref, aim_ref, cmat_ref, d_ref, wglu_ref, y_ref,
               bu_ref, xs_ref, st_ref, bum_ref):
    nb, n_steps, su = u_ref.shape
    n_state = are_ref.shape[1]
    half_u = su // 2
    half_s = n_state // 2
    chunks_per_half = half_s // S5_LANES

    def b_project(u, dst_ref):
        for hf in range(2):
            uu = u[:, hf * half_u:(hf + 1) * half_u]
            bu = _dot(uu, bmat_ref[hf])
            dst_ref[:, hf * half_s:(hf + 1) * half_s] = bu[:, :half_s]
            dst_ref[:, n_state + hf * half_s:n_state + (hf + 1) * half_s] = bu[:, half_s:]

    @pl.when(pl.program_id(0) == 0)
    def _():
        st_ref[...] = jnp.zeros_like(st_ref)
        b_project(um_ref[...], bum_ref)
        _s5_scan(bum_ref, None, st_ref, are_ref, aim_ref, range(2 * chunks_per_half), um_ref.shape[0], nb, True)

    u = pltpu.einshape("btc->tbc", u_ref[...]).reshape(n_steps * nb, su)
    b_project(u, bu_ref)
    ys = []
    for hf in range(2):
        _s5_scan(bu_ref, xs_ref, st_ref, are_ref, aim_ref,
                 range(hf * chunks_per_half, (hf + 1) * chunks_per_half), n_steps, nb, False)
        y_re = _dot(xs_ref[:, hf * half_s:(hf + 1) * half_s], cmat_ref[hf, 0])
        y_im = _dot(xs_ref[:, n_state + hf * half_s:n_state + (hf + 1) * half_s], cmat_ref[hf, 1])
        ys.append(y_re - y_im)
    y = jnp.concatenate(ys, axis=1) + d_ref[...] * u.astype(F32)
    y = jax.nn.gelu(y)
    y = y * jax.nn.sigmoid(_dot(y.astype(BF16), wglu_ref[...]))
    y_ref[...] = pltpu.einshape("tbc->btc", y.astype(BF16).reshape(n_steps, nb, su))


def _s5(u, u_meta, bmat, a_re, a_im, cmat, d_skip, w_glu):
    nb, seq, su = u.shape
    rows = S5_STEPS * nb
    n_state = a_re.shape[1]
    return pl.pallas_call(
        _s5_kernel,
        out_shape=jax.ShapeDtypeStruct((nb, seq, su), BF16),
        grid=(seq // S5_STEPS,),
        in_specs=[
            pl.BlockSpec((nb, S5_STEPS, su), lambda i: (0, i, 0)),
            _const_spec(u_meta.shape),
            _const_spec(bmat.shape),
            _const_spec(a_re.shape),
            _const_spec(a_im.shape),
            _const_spec(cmat.shape),
            _const_spec(d_skip.shape),
            _const_spec(w_glu.shape),
        ],
        out_specs=pl.BlockSpec((nb, S5_STEPS, su), lambda i: (0, i, 0)),
        scratch_shapes=[
            pltpu.VMEM((rows, 2 * n_state), F32),
            pltpu.VMEM((rows, 2 * n_state), BF16),
            pltpu.VMEM((2, nb, n_state), F32),
            pltpu.VMEM((u_meta.shape[0], 2 * n_state), F32),
        ],
        compiler_params=_cparams(("arbitrary",)),
        name="s5_mixer",
    )(u, u_meta, bmat, a_re, a_im, cmat, d_skip, w_glu)


def _s5_params(lambda_re, lambda_im, log_dt, b_re, b_im, c_re, c_im, d_skip):
    g, p, m = b_re.shape
    lr = lambda_re.astype(F32)
    li = lambda_im.astype(F32)
    dt = jnp.exp(log_dt.astype(F32))[:, None]
    mag = jnp.exp(lr * dt)
    a_re = mag * jnp.cos(li * dt)
    a_im = mag * jnp.sin(li * dt)
    den = lr * lr + li * li
    num_re = a_re - 1.0
    num_im = a_im
    coef_re = (num_re * lr + num_im * li) / den
    coef_im = (num_im * lr - num_re * li) / den
    br = b_re.astype(F32)
    bi = b_im.astype(F32)
    bbar_re = coef_re[..., None] * br - coef_im[..., None] * bi
    bbar_im = coef_re[..., None] * bi + coef_im[..., None] * br
    gh = g // 2
    eye = jnp.eye(gh, dtype=F32)

    def in_mat(bb):
        bb = bb.reshape(2, gh, p, m)
        return jnp.einsum('hgpm,gk->hgmkp', bb, eye).reshape(2, gh * m, gh * p)

    def out_mat(cc):
        cc = cc.astype(F32).reshape(2, gh, m, p)
        return jnp.einsum('hgmp,gk->hgpkm', cc, eye).reshape(2, gh * p, gh * m)

    bmat = jnp.concatenate([in_mat(bbar_re), in_mat(bbar_im)], axis=2).astype(BF16)
    cmat = jnp.stack([out_mat(c_re), out_mat(c_im)], axis=1).astype(BF16)
    return (bmat, a_re.reshape(1, g * p), a_im.reshape(1, g * p), cmat,
            d_skip.astype(F32).reshape(1, g * m))


def _colsum(x):
    return jnp.sum(x.reshape(x.shape[0] // 8, 8, x.shape[1]), axis=0)


def _colmax(x):
    return jnp.max(x.reshape(x.shape[0] // 8, 8, x.shape[1]), axis=0)


def _colmin(x):
    return jnp.min(x.reshape(x.shape[0] // 8, 8, x.shape[1]), axis=0)


def _dsa_kernel(top_k, qidxT_ref, qT_ref, wT_ref, kidx_ref, k_ref, vT_ref, kidxm_ref, km_ref, vTm_ref,
                bias_ref, biasm_ref, tri_ref, trim_ref, y_ref,
                sc_ref, scm_ref, scb_ref, scmb_ref, m_ref, acc_ref, out_ref, lg_ref, p_ref, lg2_ref, p2_ref):
    i = pl.program_id(1)
    tq = qT_ref.shape[1]
    tk = tq
    nkb = i + 1
    n_heads = qT_ref.shape[0] // HEAD_DIM
    kf = float(top_k)

    def kblock(kb):
        return pl.ds(pl.multiple_of(kb * tk, tk), tk)

    kc = lax.broadcasted_iota(jnp.int32, (tk, tq), 0) // CHUNK
    qc = lax.broadcasted_iota(jnp.int32, (tk, tq), 1) // CHUNK
    adm_diag = kc <= qc

    def idx_score(kidx):
        s = jnp.zeros((kidx.shape[0], tq), F32)
        for h in range(IDX_HEADS):
            sh = _dot(kidx, qidxT_ref[h * IDX_DIM:(h + 1) * IDX_DIM, :])
            s = s + jnp.maximum(sh, 0.0) * wT_ref[h:h + 1, :]
        return s

    sm = idx_score(kidxm_ref[...])
    scm_ref[...] = sm
    scmb_ref[...] = sm.astype(BF16)

    def score_body(kb, carry):
        mn, mx = carry
        s = idx_score(kidx_ref[kblock(kb), :])
        adm = jnp.logical_or(kb < i, adm_diag)
        s_adm = jnp.where(adm, s, NEG)
        sc_ref[kblock(kb), :] = s_adm
        scb_ref[kblock(kb), :] = s_adm.astype(BF16)
        mn = jnp.minimum(mn, _colmin(jnp.where(adm, s, -NEG)))
        mx = jnp.maximum(mx, _colmax(jnp.where(adm, s, NEG)))
        return mn, mx

    carry = lax.fori_loop(0, nkb // 2, lambda j, c: score_body(2 * j + 1, score_body(2 * j, c)),
                          (_colmin(sm), _colmax(sm)))
    mn, mx = lax.cond(nkb % 2 == 1, lambda c: score_body(nkb - 1, c), lambda c: c, carry)
    mn = jnp.min(mn, axis=0, keepdims=True)
    mx = jnp.max(mx, axis=0, keepdims=True)

    def reduce_keys(fn, combine, finish):
        def body(kb, acc):
            base = pl.multiple_of(kb * tk, tk)
            for r in range(0, tk, SCAN_ROWS):
                acc = combine(acc, fn(sc_ref[pl.ds(base + r, SCAN_ROWS), :]))
            return acc

        return finish(lax.fori_loop(0, nkb, body, fn(scm_ref[...])))

    def count_gt(t):
        return reduce_keys(lambda s: _colsum(jnp.where(s > t, 1.0, 0.0)), jnp.add,
                           lambda a: jnp.sum(a, axis=0, keepdims=True))

    n_adm = (N_META + i * tk + (lax.broadcasted_iota(jnp.int32, (1, tq), 1) // CHUNK + 1) * CHUNK).astype(F32)
    c_pos, c_nonneg = reduce_keys(
        lambda s: (_colsum(jnp.where(s > 0.0, 1.0, 0.0)), _colsum(jnp.where(s >= 0.0, 1.0, 0.0))),
        lambda a, b: (a[0] + b[0], a[1] + b[1]),
        lambda a: (jnp.sum(a[0], axis=0, keepdims=True), jnp.sum(a[1], axis=0, keepdims=True)))
    zero_vk = jnp.logical_and(c_pos < kf, c_nonneg >= kf)
    pos = c_pos >= kf
    lo0 = jnp.where(pos, 0.0, mn - (1.0 + jnp.abs(mn)))
    hi0 = jnp.where(pos, mx, 0.0)
    clo0 = jnp.where(pos, c_pos, n_adm)

    def open_rows(clo):
        return jnp.max(jnp.where(zero_vk, 0.0, clo)) > kf

    def search_cond(carry):
        it, _, _, _, go = carry
        return jnp.logical_and(it < BISECT_STEPS, go)

    def halve(carry):
        lo, hi, clo = carry
        mid = 0.5 * lo + 0.5 * hi
        c = count_gt(mid)
        up = c >= kf
        return jnp.where(up, mid, lo), jnp.where(up, hi, mid), jnp.where(up, c, clo)

    def search_body(carry):
        it, lo, hi, clo, _ = carry
        lo, hi, clo = halve(halve((lo, hi, clo)))
        return it + 2, lo, hi, clo, open_rows(clo)

    one_b = jnp.ones((), BF16)
    zero_b = jnp.zeros((), BF16)

    def count_gt_rounded(t):
        t = t.astype(BF16)

        def add_rows(accs, sb):
            for j in range(sb.shape[0] // 16):
                accs[j % len(accs)] = accs[j % len(accs)] + jnp.where(sb[16 * j:16 * (j + 1)] > t, one_b, zero_b)
            return accs

        def body(kb, accs):
            return tuple(add_rows(list(accs), scb_ref[kblock(kb), :]))

        first = jnp.where(scmb_ref[...] > t, one_b, zero_b)
        accs = lax.fori_loop(0, nkb, body, (first,) + (jnp.zeros_like(first),) * 3)
        total = (accs[0] + accs[1]) + (accs[2] + accs[3])
        return jnp.sum(total.astype(F32), axis=0, keepdims=True)

    def halve_rounded(_, carry):
        lo, hi = carry
        mid = (0.5 * lo + 0.5 * hi).astype(BF16).astype(F32)
        up = count_gt_rounded(mid) >= kf
        return jnp.where(up, mid, lo), jnp.where(up, hi, mid)

    lo, hi = lax.fori_loop(0, COARSE_STEPS, halve_rounded,
                           (lo0.astype(BF16).astype(F32), hi0.astype(BF16).astype(F32)))
    hi = hi + jnp.abs(hi) * 2.0 ** -7 + 1e-30
    clo = count_gt(lo)

    lo, hi, clo = lax.fori_loop(0, SEARCH_WARMUP, lambda _, c: halve(c), (lo, hi, clo))
    _, lo, _, _, _ = lax.while_loop(search_cond, search_body, (SEARCH_WARMUP, lo, hi, clo, open_rows(clo)))

    thr = reduce_keys(lambda s: _colmin(jnp.where(s > lo, s, -NEG)), jnp.minimum,
                      lambda a: jnp.min(a, axis=0, keepdims=True))
    thr = jnp.where(zero_vk, 0.0, thr)
    need = kf - count_gt(thr)

    def sel_mask(s, tri, carry):
        eq = jnp.where(s == thr, 1.0, 0.0)
        rank = carry + _dot(tri, eq.astype(BF16))
        sel = jnp.logical_or(s > thr, jnp.logical_and(s == thr, rank < need))
        return (jnp.where(sel, 0.0, NEG).astype(BF16),
                carry + jnp.sum(_colsum(eq), axis=0, keepdims=True))

    v_rows = vT_ref.shape[0]

    def head_rows(h):
        return slice(h * HEAD_DIM, (h + 1) * HEAD_DIM)

    def acc_rows(h):
        return slice(h * v_rows, (h + 1) * v_rows)

    def colmax16(t):
        t = jnp.max(t.reshape(t.shape[0] // 16, 16, t.shape[1]), axis=0)
        return jnp.max(t, axis=0, keepdims=True).astype(F32)

    mskm, carry0 = sel_mask(scm_ref[...], trim_ref[...], jnp.zeros((1, tq), F32))
    km = km_ref[...]
    vTm = vTm_ref[...]
    lgs = [_dot(km, qT_ref[head_rows(h), :]).astype(BF16) for h in range(n_heads)]
    ps = []
    for h in range(n_heads):
        t = lgs[h] + biasm_ref[h] + mskm
        m = colmax16(t)
        m_ref[h:h + 1, :] = m
        ps.append(jnp.exp(t - m.astype(BF16)))
    for h in range(n_heads):
        acc_ref[acc_rows(h), :] = _dot(vTm, ps[h])

    def att_body(kb, carry, lg_ref=lg_ref, p_ref=p_ref):
        msk, carry = sel_mask(sc_ref[kblock(kb), :], tri_ref[...], carry)
        which = jnp.clip(kb - i + 2, 0, 2)
        kblk = k_ref[kblock(kb), :]
        vblk = vT_ref[:, kblock(kb)]
        for h in range(n_heads):
            lg_ref[h] = _dot(kblk, qT_ref[head_rows(h), :]).astype(BF16)
        alphas = []
        for h in range(n_heads):
            t = lg_ref[h] + bias_ref[h, which] + msk
            m_old = m_ref[h:h + 1, :]
            m_new = jnp.maximum(m_old, colmax16(t))
            m_ref[h:h + 1, :] = m_new
            p_ref[h] = jnp.exp(t - m_new.astype(BF16))
            alphas.append(jnp.exp(m_old - m_new))
        for h in range(n_heads):
            acc_ref[acc_rows(h), :] = alphas[h] * acc_ref[acc_rows(h), :] + _dot(vblk, p_ref[h])
        return carry

    carry1 = lax.fori_loop(0, nkb // 2,
                           lambda j, c: att_body(2 * j + 1, att_body(2 * j, c), lg2_ref, p2_ref), carry0)
    lax.cond(nkb % 2 == 1, lambda c: att_body(nkb - 1, c), lambda c: c, carry1)

    for h in range(n_heads):
        a = acc_ref[acc_rows(h), :]
        out_ref[head_rows(h), :] = a[:HEAD_DIM] / a[HEAD_DIM:HEAD_DIM + 1]
    y_ref[...] = out_ref[...].T.astype(BF16)


def _rel_bucket_np(rel):
    half = REL_BUCKETS // 2
    max_exact = half // 2
    base = np.where(rel > 0, half, 0)
    n = np.abs(rel)
    nf = np.maximum(n, 1).astype(np.float64)
    large = max_exact + (np.log(nf / max_exact) / math.log(REL_MAX_DIST / max_exact)
                         * (half - max_exact)).astype(np.int32)
    large = np.minimum(large, half - 1)
    return base + np.where(n < max_exact, n, large)


def _bias_of_rel(rb, rel, rel_lo, rel_hi):
    rels = np.arange(rel_lo, rel_hi + 1)
    buckets = _rel_bucket_np(rels)
    col = lambda b: rb[int(b)].reshape((-1,) + (1,) * rel.ndim)
    val = jnp.broadcast_to(col(buckets[0]), (rb.shape[1],) + rel.shape)
    for j in range(1, len(rels)):
        if buckets[j] != buckets[j - 1]:
            val = jnp.where(rel[None] >= int(rels[j]), col(buckets[j]), val)
    return val


def _bias_tables(rel_bias, tq, seq):
    rb = rel_bias.astype(F32)
    far = -2 * tq
    assert np.all(_rel_bucket_np(np.arange(-(seq + N_META), -tq)) == _rel_bucket_np(np.array(far)))
    d = lax.broadcasted_iota(jnp.int32, (tq, tq), 0) - lax.broadcasted_iota(jnp.int32, (tq, tq), 1)
    rel = jnp.stack([jnp.full((tq, tq), far, jnp.int32), d - tq, d])
    bias = _bias_of_rel(rb, rel, far, tq - 1)
    rel_m = (lax.broadcasted_iota(jnp.int32, (N_META, seq), 0)
             - lax.broadcasted_iota(jnp.int32, (N_META, seq), 1) - N_META)
    bias_m = _bias_of_rel(rb, rel_m, -(seq + N_META), -1)
    return bias.astype(BF16), bias_m.astype(BF16)


def _dsa(qidxT, qT, wT, kidx, k, vT, kidx_m, k_m, vT_m, bias, bias_m, nb, seq):
    tq = DSA_TILE
    nq = seq // tq
    n_att = qT.shape[0]
    n_heads = n_att // HEAD_DIM
    tri = jnp.asarray(np.tril(np.ones((tq, tq), np.float32), -1), BF16)
    tri_m = jnp.asarray(np.tril(np.ones((N_META, N_META), np.float32), -1), BF16)
    qcol = lambda b, i: (0, b * nq + i)
    return pl.pallas_call(
        functools.partial(_dsa_kernel, min(TOPK_MAX, seq // 4)),
        out_shape=jax.ShapeDtypeStruct((nb * seq, n_att), BF16),
        grid=(nb, nq),
        in_specs=[
            pl.BlockSpec((qidxT.shape[0], tq), qcol),
            pl.BlockSpec((n_att, tq), qcol),
            pl.BlockSpec((IDX_HEADS, tq), qcol),
            pl.BlockSpec((seq, IDX_DIM), lambda b, i: (b, 0)),
            pl.BlockSpec((seq, HEAD_DIM), lambda b, i: (b, 0)),
            pl.BlockSpec((V_ROWS, seq), lambda b, i: (0, b)),
            _const_spec(kidx_m.shape),
            _const_spec(k_m.shape),
            _const_spec(vT_m.shape),
            _const_spec(bias.shape),
            pl.BlockSpec((bias_m.shape[0], N_META, tq), lambda b, i: (0, 0, i)),
            _const_spec(tri.shape),
            _const_spec(tri_m.shape),
        ],
        out_specs=pl.BlockSpec((tq, n_att), lambda b, i: (b * nq + i, 0)),
        scratch_shapes=[
            pltpu.VMEM((seq, tq), F32),
            pltpu.VMEM((N_META, tq), F32),
            pltpu.VMEM((seq, tq), BF16),
            pltpu.VMEM((N_META, tq), BF16),
            pltpu.VMEM((n_heads, tq), F32),
            pltpu.VMEM((n_heads * V_ROWS, tq), F32),
            pltpu.VMEM((n_att, tq), F32),
            pltpu.VMEM((n_heads, tq, tq), BF16),
            pltpu.VMEM((n_heads, tq, tq), BF16),
            pltpu.VMEM((n_heads, tq, tq), BF16),
            pltpu.VMEM((n_heads, tq, tq), BF16),
        ],
        compiler_params=_cparams(("arbitrary", "arbitrary")),
        name="dsa_mixer",
    )(qidxT, qT, wT, kidx, k, vT, kidx_m, k_m, vT_m, bias, bias_m, tri, tri_m)


def _merge_kernel(h_ref, ya_ref, yb_ref, gpre_ref, gpost_ref, wg_ref, wa_ref, wb_ref, wo_ref, o_ref):
    tile, d = h_ref.shape
    slabs = [slice(r, r + MERGE_SLAB) for r in range(0, tile, MERGE_SLAB)]
    merged = []
    for rows in slabs:
        hn = _rms(h_ref[rows, :], gpre_ref[...]).astype(BF16)
        pa = _dot(ya_ref[rows, :], wa_ref[...])
        pb = _dot(yb_ref[rows, :], wb_ref[...])
        g0 = jax.nn.sigmoid(_dot(hn, wg_ref[:, :d]))
        g1 = jax.nn.sigmoid(_dot(hn, wg_ref[:, d:]))
        merged.append((g0 * pa + g1 * pb).astype(BF16))
    for rows, m in zip(slabs, merged):
        o_ref[rows, :] = h_ref[rows, :] + _rms(_dot(m, wo_ref[...]), gpost_ref[...])


def _merge(h, ya, yb, gpre, gpost, wg, wa, wb, wo, nb, seq, tile):
    d = h.shape[1]
    sa = wa.shape[0]
    sb = wb.shape[0]
    nt = seq // tile
    tok = lambda b, i: (b * nt + i, 0)
    return pl.pallas_call(
        _merge_kernel,
        out_shape=jax.ShapeDtypeStruct(h.shape, F32),
        grid=(nb, nt),
        in_specs=[
            pl.BlockSpec((tile, d), tok),
            pl.BlockSpec((tile, sa), tok),
            pl.BlockSpec((tile, sb), tok),
            _const_spec((1, d)),
            _const_spec((1, d)),
            _const_spec(wg.shape),
            _const_spec(wa.shape),
            _const_spec(wb.shape),
            _const_spec(wo.shape),
        ],
        out_specs=pl.BlockSpec((tile, d), tok),
        compiler_params=_cparams(("arbitrary", "arbitrary")),
        name="mixer_merge",
    )(h, ya, yb, gpre, gpost, wg, wa, wb, wo)


def kernel(x, meta_tokens, ff1_norm_pre, ff1_norm_post, mix_norm_pre, mix_norm_post, ff2_norm_pre, ff2_norm_post, ff1_w_gate, ff1_w_up, ff1_w_down, ff2_w_gate, ff2_w_up, ff2_w_down, w_in, ssm_lambda_re, ssm_lambda_im, ssm_log_dt, ssm_b_re, ssm_b_im, ssm_c_re, ssm_c_im, ssm_d, ssm_w_glu, w_branch_a, rel_bias, w_branch_b, w_out):
    nb, seq, d = x.shape
    depth = w_in.shape[0]
    assert depth == 1, "meta-token rows are only carried as keys/state of a single layer"
    assert meta_tokens.shape[0] == N_META
    su = ssm_w_glu.shape[1]
    n_att = w_branch_b.shape[1]
    row = lambda v: v.astype(F32).reshape(1, -1)
    bf = lambda w: w.astype(BF16)

    o_u, o_qi = 0, su
    o_ki = o_qi + IDX_HEADS * IDX_DIM
    o_wi = o_ki + IDX_DIM
    o_q = o_wi + IDX_HEADS
    o_k = o_q + n_att
    o_v = o_k + HEAD_DIM
    o_g = o_v + HEAD_DIM
    win = w_in[0]
    w_nn = bf(jnp.concatenate([win[:, o_u:o_qi], win[:, o_ki:o_wi], win[:, o_k:o_v]], axis=1))
    w_t = bf(jnp.concatenate([win[:, o_qi:o_ki], win[:, o_q:o_k], win[:, o_v:o_g], win[:, o_wi:o_q]], axis=1).T)
    w_gates = bf(win[:, o_g:])

    hx = x.reshape(nb * seq, d)
    pad = META_PAD
    hm = jnp.zeros((pad, d), x.dtype).at[:N_META].set(meta_tokens.astype(x.dtype))

    ffn1 = (row(ff1_norm_pre[0]), row(ff1_norm_post[0]), bf(ff1_w_gate[0]), bf(ff1_w_up[0]), bf(ff1_w_down[0]))
    ffn2 = (row(ff2_norm_pre[0]), row(ff2_norm_post[0]), bf(ff2_w_gate[0]), bf(ff2_w_up[0]), bf(ff2_w_down[0]))

    h1 = _ffn(hx, *ffn1, FFN_TILE)
    h1m = _ffn(hm, *ffn1, pad)

    g_mix = row(mix_norm_pre[0])
    u, kidx, k, qidxT, qT, vT, wT = _proj(h1, g_mix, w_nn, w_t, nb, seq, PROJ_TILE)
    u_m, kidx_m, k_m, _, _, vT_m, _ = _proj(h1m, g_mix, w_nn, w_t, 1, pad, pad)

    bmat, a_re, a_im, cmat, d_skip = _s5_params(
        ssm_lambda_re[0], ssm_lambda_im[0], ssm_log_dt[0], ssm_b_re[0], ssm_b_im[0],
        ssm_c_re[0], ssm_c_im[0], ssm_d[0])
    ya = _s5(u.reshape(nb, seq, su), u_m[:N_META], bmat, a_re, a_im, cmat, d_skip, bf(ssm_w_glu[0]))
    ya = ya.reshape(nb * seq, su)

    bias, bias_m = _bias_tables(rel_bias, DSA_TILE, seq)
    yb = _dsa(qidxT, qT, wT, kidx, k, vT, kidx_m[:N_META], k_m[:N_META], vT_m[:, :N_META],
              bias, bias_m, nb, seq)

    h2 = _merge(h1, ya, yb, g_mix, row(mix_norm_post[0]), w_gates, bf(w_branch_a[0]),
                bf(w_branch_b[0]), bf(w_out[0]), nb, seq, MERGE_TILE)

    out = _ffn(h2, *ffn2, FFN_TILE)
    return out.reshape(nb, seq, d)
```

```python
import functools
import math

import numpy as np
import jax
import jax.numpy as jnp
from jax import lax
from jax.experimental import pallas as pl
from jax.experimental.pallas import tpu as pltpu

F32 = jnp.float32
BF16 = jnp.bfloat16

RMS_EPS = 1e-6
CHUNK = 64
N_META = 16
HEAD_DIM = 64
V_ROWS = HEAD_DIM + 16
IDX_DIM = 64
IDX_HEADS = 8
TOPK_MAX = 256
REL_BUCKETS = 32
REL_MAX_DIST = 128
NEG = -1e30
BISECT_STEPS = 64
COARSE_STEPS = 11
SEARCH_WARMUP = 3

VMEM_LIMIT = 56 * 1024 * 1024

FFN_TILE = 1024
FFN_SLAB = 512
FFN_CHUNK = 256
PROJ_TILE = 1024
PROJ_SLAB = 512
META_PAD = 128
S5_STEPS = 32
S5_LANES = 512
DSA_TILE = 256
SCAN_ROWS = 64
MERGE_TILE = 1024
MERGE_SLAB = 256


def _cparams(sem):
    return pltpu.CompilerParams(dimension_semantics=sem, vmem_limit_bytes=VMEM_LIMIT)


def _const_spec(shape):
    nd = len(shape)
    return pl.BlockSpec(shape, lambda *_: (0,) * nd)


def _resident_spec(shape):
    nd = len(shape)
    return pl.BlockSpec(shape, lambda *_: (0,) * nd, pipeline_mode=pl.Buffered(1))


def _rms(x, g):
    return x * lax.rsqrt(jnp.mean(x * x, axis=-1, keepdims=True) + RMS_EPS) * g


def _dot(a, b):
    return jnp.dot(a, b, preferred_element_type=F32)


def _dot_nt(a, b):
    return lax.dot_general(a, b, (((1,), (1,)), ((), ())), preferred_element_type=F32)


def _ffn_kernel(x_ref, gpre_ref, gpost_ref, wg_ref, wu_ref, wd_ref, o_ref, acc_ref):
    n_sub, sub, _ = acc_ref.shape
    n_chunks = wg_ref.shape[1] // FFN_CHUNK
    for s in range(n_sub):
        rows = slice(s * sub, (s + 1) * sub)
        x = x_ref[rows, :]
        hn = _rms(x, gpre_ref[...]).astype(BF16)
        for c in range(n_chunks):
            cols = slice(c * FFN_CHUNK, (c + 1) * FFN_CHUNK)
            g = _dot(hn, wg_ref[:, cols])
            u = _dot(hn, wu_ref[:, cols])
            a = (jax.nn.silu(g) * u).astype(BF16)
            part = _dot(a, wd_ref[cols, :])
            if c == 0:
                acc_ref[s] = part
            else:
                acc_ref[s] += part
        o_ref[rows, :] = x + 0.5 * _rms(acc_ref[s], gpost_ref[...])


def _ffn(x, gpre, gpost, wg, wu, wd, tile):
    t, d = x.shape
    dff = wg.shape[1]
    sub = min(tile, FFN_SLAB)
    return pl.pallas_call(
        _ffn_kernel,
        out_shape=jax.ShapeDtypeStruct((t, d), F32),
        grid=(t // tile,),
        in_specs=[
            pl.BlockSpec((tile, d), lambda i: (i, 0)),
            _const_spec((1, d)),
            _const_spec((1, d)),
            _resident_spec((d, dff)),
            _resident_spec((d, dff)),
            _resident_spec((dff, d)),
        ],
        out_specs=pl.BlockSpec((tile, d), lambda i: (i, 0)),
        scratch_shapes=[pltpu.VMEM((tile // sub, sub, d), F32)],
        compiler_params=_cparams(("arbitrary",)),
        name="ffn_half_step",
    )(x, gpre, gpost, wg, wu, wd)


def _proj_kernel(h_ref, g_ref, wn_ref, wt_ref, u_ref, kidx_ref, k_ref, qidxT_ref, qT_ref, vT_ref, wT_ref):
    tile = h_ref.shape[0]
    slab = min(tile, PROJ_SLAB)
    su = u_ref.shape[1]
    nq = qidxT_ref.shape[0]
    na = qT_ref.shape[0]
    for r in range(0, tile, slab):
        rows = slice(r, r + slab)
        hn = _rms(h_ref[rows, :], g_ref[...]).astype(BF16)
        nn = _dot(hn, wn_ref[...])
        u_ref[rows, :] = nn[:, :su].astype(BF16)
        kidx_ref[rows, :] = nn[:, su:su + IDX_DIM].astype(BF16)
        k_ref[rows, :] = nn[:, su + IDX_DIM:su + IDX_DIM + HEAD_DIM].astype(BF16)
        tt = _dot_nt(wt_ref[...], hn)
        qidxT_ref[:, rows] = (tt[:nq] * (IDX_DIM ** -0.5)).astype(BF16)
        qT_ref[:, rows] = (tt[nq:nq + na] * (HEAD_DIM ** -0.5)).astype(BF16)
        ones = jnp.ones((V_ROWS - HEAD_DIM, slab), F32)
        vT_ref[:, rows] = jnp.concatenate([tt[nq + na:nq + na + HEAD_DIM], ones], axis=0).astype(BF16)
        wT_ref[:, rows] = tt[nq + na + HEAD_DIM:nq + na + HEAD_DIM + IDX_HEADS] * (IDX_HEADS ** -0.5)


def _proj(h, gain, w_nn, w_t, nb, seq, tile):
    d = h.shape[1]
    su = w_nn.shape[1] - IDX_DIM - HEAD_DIM
    nq = IDX_HEADS * IDX_DIM
    na = w_t.shape[0] - nq - HEAD_DIM - IDX_HEADS
    nt = seq // tile
    tok = lambda b, i: (b * nt + i, 0)
    tokT = lambda b, i: (0, b * nt + i)
    out_shape = (
        jax.ShapeDtypeStruct((nb * seq, su), BF16),
        jax.ShapeDtypeStruct((nb * seq, IDX_DIM), BF16),
        jax.ShapeDtypeStruct((nb * seq, HEAD_DIM), BF16),
        jax.ShapeDtypeStruct((nq, nb * seq), BF16),
        jax.ShapeDtypeStruct((na, nb * seq), BF16),
        jax.ShapeDtypeStruct((V_ROWS, nb * seq), BF16),
        jax.ShapeDtypeStruct((IDX_HEADS, nb * seq), F32),
    )
    out_specs = (
        pl.BlockSpec((tile, su), tok),
        pl.BlockSpec((tile, IDX_DIM), tok),
        pl.BlockSpec((tile, HEAD_DIM), tok),
        pl.BlockSpec((nq, tile), tokT),
        pl.BlockSpec((na, tile), tokT),
        pl.BlockSpec((V_ROWS, tile), tokT),
        pl.BlockSpec((IDX_HEADS, tile), tokT),
    )
    return pl.pallas_call(
        _proj_kernel,
        out_shape=out_shape,
        grid=(nb, nt),
        in_specs=[
            pl.BlockSpec((tile, d), tok),
            _const_spec((1, d)),
            _const_spec(w_nn.shape),
            _const_spec(w_t.shape),
        ],
        out_specs=out_specs,
        compiler_params=_cparams(("arbitrary", "arbitrary")),
        name="mixer_in_proj",
    )(h, gain, w_nn, w_t)


def _s5_scan(bu_ref, xs_ref, st_ref, are_ref, aim_ref, chunks, n_steps, nb, bcast_rows):
    n_state = are_ref.shape[1]
    for c in chunks:
        lanes = slice(c * S5_LANES, (c + 1) * S5_LANES)
        lanes_im = slice(n_state + c * S5_LANES, n_state + (c + 1) * S5_LANES)
        a_re = jnp.broadcast_to(are_ref[:, lanes], (nb, S5_LANES))
        a_im = jnp.broadcast_to(aim_ref[:, lanes], (nb, S5_LANES))
        x_re = st_ref[0, :, lanes]
        x_im = st_ref[1, :, lanes]
        for t in range(n_steps):
            rows = slice(t, t + 1) if bcast_rows else slice(t * nb, (t + 1) * nb)
            b_re = bu_ref[rows, lanes]
            b_im = bu_ref[rows, lanes_im]
            x_re, x_im = (a_re * x_re - a_im * x_im + b_re,
                          a_re * x_im + a_im * x_re + b_im)
            if xs_ref is not None:
                xs_ref[rows, lanes] = x_re.astype(BF16)
                xs_ref[rows, lanes_im] = x_im.astype(BF16)
        st_ref[0, :, lanes] = x_re
        st_ref[1, :, lanes] = x_im


def _s5_kernel(u_ref, um_ref, bmat_ref, are_ref, aim_ref, cmat_ref, d_ref, wglu_ref, y_ref,
               bu_ref, xs_ref, st_ref, bum_ref):
    nb, n_steps, su = u_ref.shape
    n_state = are_ref.shape[1]
    half_u = su // 2
    half_s = n_state // 2
    chunks_per_half = half_s // S5_LANES

    def b_project(u, dst_ref):
        for hf in range(2):
            uu = u[:, hf * half_u:(hf + 1) * half_u]
            bu = _dot(uu, bmat_ref[hf])
            dst_ref[:, hf * half_s:(hf + 1) * half_s] = bu[:, :half_s]
            dst_ref[:, n_state + hf * half_s:n_state + (hf + 1) * half_s] = bu[:, half_s:]

    @pl.when(pl.program_id(0) == 0)
    def _():
        st_ref[...] = jnp.zeros_like(st_ref)
        b_project(um_ref[...], bum_ref)
        _s5_scan(bum_ref, None, st_ref, are_ref, aim_ref, range(2 * chunks_per_half), um_ref.shape[0], nb, True)

    u = pltpu.einshape("btc->tbc", u_ref[...]).reshape(n_steps * nb, su)
    b_project(u, bu_ref)
    ys = []
    for hf in range(2):
        _s5_scan(bu_ref, xs_ref, st_ref, are_ref, aim_ref,
                 range(hf * chunks_per_half, (hf + 1) * chunks_per_half), n_steps, nb, False)
        y_re = _dot(xs_ref[:, hf * half_s:(hf + 1) * half_s], cmat_ref[hf, 0])
        y_im = _dot(xs_ref[:, n_state + hf * half_s:n_state + (hf + 1) * half_s], cmat_ref[hf, 1])
        ys.append(y_re - y_im)
    y = jnp.concatenate(ys, axis=1) + d_ref[...] * u.astype(F32)
    y = jax.nn.gelu(y)
    y = y * jax.nn.sigmoid(_dot(y.astype(BF16), wglu_ref[...]))
    y_ref[...] = pltpu.einshape("tbc->btc", y.astype(BF16).reshape(n_steps, nb, su))


def _s5(u, u_meta, bmat, a_re, a_im, cmat, d_skip, w_glu):
    nb, seq, su = u.shape
    rows = S5_STEPS * nb
    n_state = a_re.shape[1]
    return pl.pallas_call(
        _s5_kernel,
        out_shape=jax.ShapeDtypeStruct((nb, seq, su), BF16),
        grid=(seq // S5_STEPS,),
        in_specs=[
            pl.BlockSpec((nb, S5_STEPS, su), lambda i: (0, i, 0)),
            _const_spec(u_meta.shape),
            _const_spec(bmat.shape),
            _const_spec(a_re.shape),
            _const_spec(a_im.shape),
            _const_spec(cmat.shape),
            _const_spec(d_skip.shape),
            _const_spec(w_glu.shape),
        ],
        out_specs=pl.BlockSpec((nb, S5_STEPS, su), lambda i: (0, i, 0)),
        scratch_shapes=[
            pltpu.VMEM((rows, 2 * n_state), F32),
            pltpu.VMEM((rows, 2 * n_state), BF16),
            pltpu.VMEM((2, nb, n_state), F32),
            pltpu.VMEM((u_meta.shape[0], 2 * n_state), F32),
        ],
        compiler_params=_cparams(("arbitrary",)),
        name="s5_mixer",
    )(u, u_meta, bmat, a_re, a_im, cmat, d_skip, w_glu)


def _s5_params(lambda_re, lambda_im, log_dt, b_re, b_im, c_re, c_im, d_skip):
    g, p, m = b_re.shape
    lr = lambda_re.astype(F32)
    li = lambda_im.astype(F32)
    dt = jnp.exp(log_dt.astype(F32))[:, None]
    mag = jnp.exp(lr * dt)
    a_re = mag * jnp.cos(li * dt)
    a_im = mag * jnp.sin(li * dt)
    den = lr * lr + li * li
    num_re = a_re - 1.0
    num_im = a_im
    coef_re = (num_re * lr + num_im * li) / den
    coef_im = (num_im * lr - num_re * li) / den
    br = b_re.astype(F32)
    bi = b_im.astype(F32)
    bbar_re = coef_re[..., None] * br - coef_im[..., None] * bi
    bbar_im = coef_re[..., None] * bi + coef_im[..., None] * br
    gh = g // 2
    eye = jnp.eye(gh, dtype=F32)

    def in_mat(bb):
        bb = bb.reshape(2, gh, p, m)
        return jnp.einsum('hgpm,gk->hgmkp', bb, eye).reshape(2, gh * m, gh * p)

    def out_mat(cc):
        cc = cc.astype(F32).reshape(2, gh, m, p)
        return jnp.einsum('hgmp,gk->hgpkm', cc, eye).reshape(2, gh * p, gh * m)

    bmat = jnp.concatenate([in_mat(bbar_re), in_mat(bbar_im)], axis=2).astype(BF16)
    cmat = jnp.stack([out_mat(c_re), out_mat(c_im)], axis=1).astype(BF16)
    return (bmat, a_re.reshape(1, g * p), a_im.reshape(1, g * p), cmat,
            d_skip.astype(F32).reshape(1, g * m))


def _colsum(x):
    return jnp.sum(x.reshape(x.shape[0] // 8, 8, x.shape[1]), axis=0)


def _colmax(x):
    return jnp.max(x.reshape(x.shape[0] // 8, 8, x.shape[1]), axis=0)


def _colmin(x):
    return jnp.min(x.reshape(x.shape[0] // 8, 8, x.shape[1]), axis=0)


def _dsa_kernel(top_k, qidxT_ref, qT_ref, wT_ref, kidx_ref, k_ref, vT_ref, kidxm_ref, km_ref, vTm_ref,
                bias_ref, biasm_ref, tri_ref, trim_ref, y_ref,
                sc_ref, scm_ref, scb_ref, scmb_ref, m_ref, acc_ref, out_ref,
                lg_ref, p_ref, lg2_ref, p2_ref):
    i = pl.program_id(1)
    tq = qT_ref.shape[1]
    tk = tq
    nkb = i + 1
    n_heads = qT_ref.shape[0] // HEAD_DIM
    kf = float(top_k)

    def kblock(kb):
        return pl.ds(pl.multiple_of(kb * tk, tk), tk)

    def head_rows(h):
        return slice(h * HEAD_DIM, (h + 1) * HEAD_DIM)

    kc = lax.broadcasted_iota(jnp.int32, (tk, tq), 0) // CHUNK
    qc = lax.broadcasted_iota(jnp.int32, (tk, tq), 1) // CHUNK
    adm_diag = kc <= qc

    def idx_score(kidx):
        s = jnp.zeros((kidx.shape[0], tq), F32)
        for h in range(IDX_HEADS):
            sh = _dot(kidx, qidxT_ref[h * IDX_DIM:(h + 1) * IDX_DIM, :])
            s = s + jnp.maximum(sh, 0.0) * wT_ref[h:h + 1, :]
        return s

    sm = idx_score(kidxm_ref[...])
    scm_ref[...] = sm
    scmb_ref[...] = sm.astype(BF16)

    def score_body(kb, carry):
        mn, mx = carry
        s = idx_score(kidx_ref[kblock(kb), :])
        adm = jnp.logical_or(kb < i, adm_diag)
        s_adm = jnp.where(adm, s, NEG)
        sc_ref[kblock(kb), :] = s_adm
        scb_ref[kblock(kb), :] = s_adm.astype(BF16)
        mn = jnp.minimum(mn, _colmin(jnp.where(adm, s, -NEG)))
        mx = jnp.maximum(mx, _colmax(jnp.where(adm, s, NEG)))
        return mn, mx

    carry = lax.fori_loop(0, nkb // 2, lambda j, c: score_body(2 * j + 1, score_body(2 * j, c)),
                          (_colmin(sm), _colmax(sm)))
    mn, mx = lax.cond(nkb % 2 == 1, lambda c: score_body(nkb - 1, c), lambda c: c, carry)
    mn = jnp.min(mn, axis=0, keepdims=True)
    mx = jnp.max(mx, axis=0, keepdims=True)

    def reduce_keys(fn, combine, finish):
        def body(kb, acc):
            base = pl.multiple_of(kb * tk, tk)
            for r in range(0, tk, SCAN_ROWS):
                acc = combine(acc, fn(sc_ref[pl.ds(base + r, SCAN_ROWS), :]))
            return acc

        return finish(lax.fori_loop(0, nkb, body, fn(scm_ref[...])))

    def count_gt(t):
        return reduce_keys(lambda s: _colsum(jnp.where(s > t, 1.0, 0.0)), jnp.add,
                           lambda a: jnp.sum(a, axis=0, keepdims=True))

    n_adm = (N_META + i * tk + (lax.broadcasted_iota(jnp.int32, (1, tq), 1) // CHUNK + 1) * CHUNK).astype(F32)
    c_pos, c_nonneg = reduce_keys(
        lambda s: (_colsum(jnp.where(s > 0.0, 1.0, 0.0)), _colsum(jnp.where(s >= 0.0, 1.0, 0.0))),
        lambda a, b: (a[0] + b[0], a[1] + b[1]),
        lambda a: (jnp.sum(a[0], axis=0, keepdims=True), jnp.sum(a[1], axis=0, keepdims=True)))
    zero_vk = jnp.logical_and(c_pos < kf, c_nonneg >= kf)
    pos = c_pos >= kf
    lo0 = jnp.where(pos, 0.0, mn - (1.0 + jnp.abs(mn)))
    hi0 = jnp.where(pos, mx, 0.0)
    clo0 = jnp.where(pos, c_pos, n_adm)

    def open_rows(clo):
        return jnp.max(jnp.where(zero_vk, 0.0, clo)) > kf

    def search_cond(carry):
        it, _, _, _, go = carry
        return jnp.logical_and(it < BISECT_STEPS, go)

    def halve(carry):
        lo, hi, clo = carry
        mid = 0.5 * lo + 0.5 * hi
        c = count_gt(mid)
        up = c >= kf
        return jnp.where(up, mid, lo), jnp.where(up, hi, mid), jnp.where(up, c, clo)

    def search_body(carry):
        it, lo, hi, clo, _ = carry
        lo, hi, clo = halve(halve((lo, hi, clo)))
        return it + 2, lo, hi, clo, open_rows(clo)

    one_b = jnp.ones((), BF16)
    zero_b = jnp.zeros((), BF16)

    def count_gt_rounded(t):
        t = t.astype(BF16)

        def add_rows(accs, sb):
            for j in range(sb.shape[0] // 16):
                accs[j % len(accs)] = accs[j % len(accs)] + jnp.where(sb[16 * j:16 * (j + 1)] > t, one_b, zero_b)
            return accs

        def body(kb, accs):
            return tuple(add_rows(list(accs), scb_ref[kblock(kb), :]))

        first = jnp.where(scmb_ref[...] > t, one_b, zero_b)
        accs = lax.fori_loop(0, nkb, body, (first,) + (jnp.zeros_like(first),) * 3)
        total = (accs[0] + accs[1]) + (accs[2] + accs[3])
        return jnp.sum(total.astype(F32), axis=0, keepdims=True)

    def halve_rounded(_, carry):
        lo, hi = carry
        mid = (0.5 * lo + 0.5 * hi).astype(BF16).astype(F32)
        up = count_gt_rounded(mid) >= kf
        return jnp.where(up, mid, lo), jnp.where(up, hi, mid)

    lo, hi = lax.fori_loop(0, COARSE_STEPS, halve_rounded,
                           (lo0.astype(BF16).astype(F32), hi0.astype(BF16).astype(F32)))
    hi = hi + jnp.abs(hi) * 2.0 ** -7 + 1e-30
    clo = count_gt(lo)

    lo, hi, clo = lax.fori_loop(0, SEARCH_WARMUP, lambda _, c: halve(c), (lo, hi, clo))
    _, lo, _, _, _ = lax.while_loop(search_cond, search_body, (SEARCH_WARMUP, lo, hi, clo, open_rows(clo)))

    thr = reduce_keys(lambda s: _colmin(jnp.where(s > lo, s, -NEG)), jnp.minimum,
                      lambda a: jnp.min(a, axis=0, keepdims=True))
    thr = jnp.where(zero_vk, 0.0, thr)
    need = kf - count_gt(thr)

    def sel_mask(s, tri, carry):
        eq = jnp.where(s == thr, 1.0, 0.0)
        rank = carry + _dot(tri, eq.astype(BF16))
        sel = jnp.logical_or(s > thr, jnp.logical_and(s == thr, rank < need))
        return (jnp.where(sel, 0.0, NEG).astype(BF16),
                carry + jnp.sum(_colsum(eq), axis=0, keepdims=True))

    v_rows = vT_ref.shape[0]

    def acc_rows(h):
        return slice(h * v_rows, (h + 1) * v_rows)

    def colmax16(t):
        t = jnp.max(t.reshape(t.shape[0] // 16, 16, t.shape[1]), axis=0)
        return jnp.max(t, axis=0, keepdims=True).astype(F32)

    mskm, carry0 = sel_mask(scm_ref[...], trim_ref[...], jnp.zeros((1, tq), F32))
    km = km_ref[...]
    vTm = vTm_ref[...]
    lgs = [_dot(km, qT_ref[head_rows(h), :]).astype(BF16) for h in range(n_heads)]
    ps = []
    for h in range(n_heads):
        t = lgs[h] + biasm_ref[h] + mskm
        m = colmax16(t)
        m_ref[h:h + 1, :] = m
        ps.append(jnp.exp(t - m.astype(BF16)))
    for h in range(n_heads):
        acc_ref[acc_rows(h), :] = _dot(vTm, ps[h])

    def att_body(kb, carry, lg_ref=lg_ref, p_ref=p_ref):
        msk, carry = sel_mask(sc_ref[kblock(kb), :], tri_ref[...], carry)
        which = jnp.clip(kb - i + 2, 0, 2)
        kblk = k_ref[kblock(kb), :]
        vblk = vT_ref[:, kblock(kb)]
        for h in range(n_heads):
            lg_ref[h] = _dot(kblk, qT_ref[head_rows(h), :]).astype(BF16)
        alphas = []
        for h in range(n_heads):
            t = lg_ref[h] + bias_ref[h, which] + msk
            m_old = m_ref[h:h + 1, :]
            m_new = jnp.maximum(m_old, colmax16(t))
            m_ref[h:h + 1, :] = m_new
            p_ref[h] = jnp.exp(t - m_new.astype(BF16))
            alphas.append(jnp.exp(m_old - m_new))
        for h in range(n_heads):
            acc_ref[acc_rows(h), :] = alphas[h] * acc_ref[acc_rows(h), :] + _dot(vblk, p_ref[h])
        return carry

    carry1 = lax.fori_loop(0, nkb // 2,
                           lambda j, c: att_body(2 * j + 1, att_body(2 * j, c), lg2_ref, p2_ref), carry0)
    lax.cond(nkb % 2 == 1, lambda c: att_body(nkb - 1, c), lambda c: c, carry1)

    for h in range(n_heads):
        a = acc_ref[acc_rows(h), :]
        out_ref[head_rows(h), :] = a[:HEAD_DIM] / a[HEAD_DIM:HEAD_DIM + 1]
    y_ref[...] = out_ref[...].T.astype(BF16)


def _rel_bucket_np(rel):
    half = REL_BUCKETS // 2
    max_exact = half // 2
    base = np.where(rel > 0, half, 0)
    n = np.abs(rel)
    nf = np.maximum(n, 1).astype(np.float64)
    large = max_exact + (np.log(nf / max_exact) / math.log(REL_MAX_DIST / max_exact)
                         * (half - max_exact)).astype(np.int32)
    large = np.minimum(large, half - 1)
    return base + np.where(n < max_exact, n, large)


def _bias_of_rel(rb, rel, rel_lo, rel_hi):
    rels = np.arange(rel_lo, rel_hi + 1)
    buckets = _rel_bucket_np(rels)
    col = lambda b: rb[int(b)].reshape((-1,) + (1,) * rel.ndim)
    val = jnp.broadcast_to(col(buckets[0]), (rb.shape[1],) + rel.shape)
    for j in range(1, len(rels)):
        if buckets[j] != buckets[j - 1]:
            val = jnp.where(rel[None] >= int(rels[j]), col(buckets[j]), val)
    return val


def _bias_tables(rel_bias, tq, seq):
    rb = rel_bias.astype(F32)
    far = -2 * tq
    assert np.all(_rel_bucket_np(np.arange(-(seq + N_META), -tq)) == _rel_bucket_np(np.array(far)))
    d = lax.broadcasted_iota(jnp.int32, (tq, tq), 0) - lax.broadcasted_iota(jnp.int32, (tq, tq), 1)
    rel = jnp.stack([jnp.full((tq, tq), far, jnp.int32), d - tq, d])
    bias = _bias_of_rel(rb, rel, far, tq - 1)
    rel_m = (lax.broadcasted_iota(jnp.int32, (N_META, seq), 0)
             - lax.broadcasted_iota(jnp.int32, (N_META, seq), 1) - N_META)
    bias_m = _bias_of_rel(rb, rel_m, -(seq + N_META), -1)
    return bias.astype(BF16), bias_m.astype(BF16)


def _dsa(qidxT, qT, wT, kidx, k, vT, kidx_m, k_m, vT_m, bias, bias_m, nb, seq):
    tq = DSA_TILE
    nq = seq // tq
    n_att = qT.shape[0]
    n_heads = n_att // HEAD_DIM
    tri = jnp.asarray(np.tril(np.ones((tq, tq), np.float32), -1), BF16)
    tri_m = jnp.asarray(np.tril(np.ones((N_META, N_META), np.float32), -1), BF16)
    qcol = lambda b, i: (0, b * nq + i)
    return pl.pallas_call(
        functools.partial(_dsa_kernel, min(TOPK_MAX, seq // 4)),
        out_shape=jax.ShapeDtypeStruct((nb * seq, n_att), BF16),
        grid=(nb, nq),
        in_specs=[
            pl.BlockSpec((qidxT.shape[0], tq), qcol),
            pl.BlockSpec((n_att, tq), qcol),
            pl.BlockSpec((IDX_HEADS, tq), qcol),
            pl.BlockSpec((seq, IDX_DIM), lambda b, i: (b, 0)),
            pl.BlockSpec((seq, HEAD_DIM), lambda b, i: (b, 0)),
            pl.BlockSpec((V_ROWS, seq), lambda b, i: (0, b)),
            _const_spec(kidx_m.shape),
            _const_spec(k_m.shape),
            _const_spec(vT_m.shape),
            _const_spec(bias.shape),
            pl.BlockSpec((bias_m.shape[0], N_META, tq), lambda b, i: (0, 0, i)),
            _const_spec(tri.shape),
            _const_spec(tri_m.shape),
        ],
        out_specs=pl.BlockSpec((tq, n_att), lambda b, i: (b * nq + i, 0)),
        scratch_shapes=[
            pltpu.VMEM((seq, tq), F32),
            pltpu.VMEM((N_META, tq), F32),
            pltpu.VMEM((seq, tq), BF16),
            pltpu.VMEM((N_META, tq), BF16),
            pltpu.VMEM((n_heads, tq), F32),
            pltpu.VMEM((n_heads * V_ROWS, tq), F32),
            pltpu.VMEM((n_att, tq), F32),
            pltpu.VMEM((n_heads, tq, tq), BF16),
            pltpu.VMEM((n_heads, tq, tq), BF16),
            pltpu.VMEM((n_heads, tq, tq), BF16),
            pltpu.VMEM((n_heads, tq, tq), BF16),
        ],
        compiler_params=_cparams(("arbitrary", "arbitrary")),
        name="dsa_mixer",
    )(qidxT, qT, wT, kidx, k, vT, kidx_m, k_m, vT_m, bias, bias_m, tri, tri_m)


def _merge_kernel(h_ref, ya_ref, yb_ref, gpre_ref, gpost_ref, wg_ref, wa_ref, wb_ref, wo_ref, o_ref):
    tile, d = h_ref.shape
    slabs = [slice(r, r + MERGE_SLAB) for r in range(0, tile, MERGE_SLAB)]
    merged = []
    for rows in slabs:
        hn = _rms(h_ref[rows, :], gpre_ref[...]).astype(BF16)
        pa = _dot(ya_ref[rows, :], wa_ref[...])
        pb = _dot(yb_ref[rows, :], wb_ref[...])
        g0 = jax.nn.sigmoid(_dot(hn, wg_ref[:, :d]))
        g1 = jax.nn.sigmoid(_dot(hn, wg_ref[:, d:]))
        merged.append((g0 * pa + g1 * pb).astype(BF16))
    for rows, m in zip(slabs, merged):
        o_ref[rows, :] = h_ref[rows, :] + _rms(_dot(m, wo_ref[...]), gpost_ref[...])


def _merge(h, ya, yb, gpre, gpost, wg, wa, wb, wo, nb, seq, tile):
    d = h.shape[1]
    sa = wa.shape[0]
    sb = wb.shape[0]
    nt = seq // tile
    tok = lambda b, i: (b * nt + i, 0)
    return pl.pallas_call(
        _merge_kernel,
        out_shape=jax.ShapeDtypeStruct(h.shape, F32),
        grid=(nb, nt),
        in_specs=[
            pl.BlockSpec((tile, d), tok),
            pl.BlockSpec((tile, sa), tok),
            pl.BlockSpec((tile, sb), tok),
            _const_spec((1, d)),
            _const_spec((1, d)),
            _const_spec(wg.shape),
            _const_spec(wa.shape),
            _const_spec(wb.shape),
            _const_spec(wo.shape),
        ],
        out_specs=pl.BlockSpec((tile, d), tok),
        compiler_params=_cparams(("arbitrary", "arbitrary")),
        name="mixer_merge",
    )(h, ya, yb, gpre, gpost, wg, wa, wb, wo)


def kernel(x, meta_tokens, ff1_norm_pre, ff1_norm_post, mix_norm_pre, mix_norm_post, ff2_norm_pre, ff2_norm_post, ff1_w_gate, ff1_w_up, ff1_w_down, ff2_w_gate, ff2_w_up, ff2_w_down, w_in, ssm_lambda_re, ssm_lambda_im, ssm_log_dt, ssm_b_re, ssm_b_im, ssm_c_re, ssm_c_im, ssm_d, ssm_w_glu, w_branch_a, rel_bias, w_branch_b, w_out):
    nb, seq, d = x.shape
    depth = w_in.shape[0]
    assert depth == 1, "meta-token rows are only carried as keys/state of a single layer"
    assert meta_tokens.shape[0] == N_META
    su = ssm_w_glu.shape[1]
    n_att = w_branch_b.shape[1]
    row = lambda v: v.astype(F32).reshape(1, -1)
    bf = lambda w: w.astype(BF16)

    o_u, o_qi = 0, su
    o_ki = o_qi + IDX_HEADS * IDX_DIM
    o_wi = o_ki + IDX_DIM
    o_q = o_wi + IDX_HEADS
    o_k = o_q + n_att
    o_v = o_k + HEAD_DIM
    o_g = o_v + HEAD_DIM
    win = w_in[0]
    w_nn = bf(jnp.concatenate([win[:, o_u:o_qi], win[:, o_ki:o_wi], win[:, o_k:o_v]], axis=1))
    w_t = bf(jnp.concatenate([win[:, o_qi:o_ki], win[:, o_q:o_k], win[:, o_v:o_g], win[:, o_wi:o_q]], axis=1).T)
    w_gates = bf(win[:, o_g:])

    hx = x.reshape(nb * seq, d)
    pad = META_PAD
    hm = jnp.zeros((pad, d), x.dtype).at[:N_META].set(meta_tokens.astype(x.dtype))

    ffn1 = (row(ff1_norm_pre[0]), row(ff1_norm_post[0]), bf(ff1_w_gate[0]), bf(ff1_w_up[0]), bf(ff1_w_down[0]))
    ffn2 = (row(ff2_norm_pre[0]), row(ff2_norm_post[0]), bf(ff2_w_gate[0]), bf(ff2_w_up[0]), bf(ff2_w_down[0]))

    h1 = _ffn(hx, *ffn1, FFN_TILE)
    h1m = _ffn(hm, *ffn1, pad)

    g_mix = row(mix_norm_pre[0])
    u, kidx, k, qidxT, qT, vT, wT = _proj(h1, g_mix, w_nn, w_t, nb, seq, PROJ_TILE)
    u_m, kidx_m, k_m, _, _, vT_m, _ = _proj(h1m, g_mix, w_nn, w_t, 1, pad, pad)

    bmat, a_re, a_im, cmat, d_skip = _s5_params(
        ssm_lambda_re[0], ssm_lambda_im[0], ssm_log_dt[0], ssm_b_re[0], ssm_b_im[0],
        ssm_c_re[0], ssm_c_im[0], ssm_d[0])
    ya = _s5(u.reshape(nb, seq, su), u_m[:N_META], bmat, a_re, a_im, cmat, d_skip, bf(ssm_w_glu[0]))
    ya = ya.reshape(nb * seq, su)

    bias, bias_m = _bias_tables(rel_bias, DSA_TILE, seq)
    yb = _dsa(qidxT, qT, wT, kidx, k, vT, kidx_m[:N_META], k_m[:N_META], vT_m[:, :N_META],
              bias, bias_m, nb, seq)

    h2 = _merge(h1, ya, yb, g_mix, row(mix_norm_post[0]), w_gates, bf(w_branch_a[0]),
                bf(w_branch_b[0]), bf(w_out[0]), nb, seq, MERGE_TILE)

    out = _ffn(h2, *ffn2, FFN_TILE)
    return out.reshape(nb, seq, d)
```

```python
import functools
import math

import numpy as np
import jax
import jax.numpy as jnp
from jax import lax
from jax.experimental import pallas as pl
from jax.experimental.pallas import tpu as pltpu

F32 = jnp.float32
BF16 = jnp.bfloat16

RMS_EPS = 1e-6
CHUNK = 64
N_META = 16
HEAD_DIM = 64
V_ROWS = HEAD_DIM + 16
IDX_DIM = 64
IDX_HEADS = 8
TOPK_MAX = 256
REL_BUCKETS = 32
REL_MAX_DIST = 128
NEG = -1e30
BISECT_STEPS = 64
COARSE_STEPS = 10
SEARCH_WARMUP = 4

VMEM_LIMIT = 56 * 1024 * 1024

FFN_TILE = 1024
FFN_SLAB = 512
FFN_CHUNK = 256
PROJ_TILE = 1024
PROJ_SLAB = 512
META_PAD = 128
S5_STEPS = 32
S5_LANES = 512
DSA_TILE = 256
SCAN_ROWS = 64
MERGE_TILE = 1024
MERGE_SLAB = 256


def _cparams(sem):
    return pltpu.CompilerParams(dimension_semantics=sem, vmem_limit_bytes=VMEM_LIMIT)


def _const_spec(shape):
    nd = len(shape)
    return pl.BlockSpec(shape, lambda *_: (0,) * nd)


def _resident_spec(shape):
    nd = len(shape)
    return pl.BlockSpec(shape, lambda *_: (0,) * nd, pipeline_mode=pl.Buffered(1))


def _rms(x, g):
    return x * lax.rsqrt(jnp.mean(x * x, axis=-1, keepdims=True) + RMS_EPS) * g


def _dot(a, b):
    return jnp.dot(a, b, preferred_element_type=F32)


def _dot_nt(a, b):
    return lax.dot_general(a, b, (((1,), (1,)), ((), ())), preferred_element_type=F32)


def _ffn_kernel(x_ref, gpre_ref, gpost_ref, wg_ref, wu_ref, wd_ref, o_ref, acc_ref):
    n_sub, sub, _ = acc_ref.shape
    n_chunks = wg_ref.shape[1] // FFN_CHUNK
    for s in range(n_sub):
        rows = slice(s * sub, (s + 1) * sub)
        x = x_ref[rows, :]
        hn = _rms(x, gpre_ref[...]).astype(BF16)
        for c in range(n_chunks):
            cols = slice(c * FFN_CHUNK, (c + 1) * FFN_CHUNK)
            g = _dot(hn, wg_ref[:, cols])
            u = _dot(hn, wu_ref[:, cols])
            a = (jax.nn.silu(g) * u).astype(BF16)
            part = _dot(a, wd_ref[cols, :])
            if c == 0:
                acc_ref[s] = part
            else:
                acc_ref[s] += part
        o_ref[rows, :] = x + 0.5 * _rms(acc_ref[s], gpost_ref[...])


def _ffn(x, gpre, gpost, wg, wu, wd, tile):
    t, d = x.shape
    dff = wg.shape[1]
    sub = min(tile, FFN_SLAB)
    return pl.pallas_call(
        _ffn_kernel,
        out_shape=jax.ShapeDtypeStruct((t, d), F32),
        grid=(t // tile,),
        in_specs=[
            pl.BlockSpec((tile, d), lambda i: (i, 0)),
            _const_spec((1, d)),
            _const_spec((1, d)),
            _resident_spec((d, dff)),
            _resident_spec((d, dff)),
            _resident_spec((dff, d)),
        ],
        out_specs=pl.BlockSpec((tile, d), lambda i: (i, 0)),
        scratch_shapes=[pltpu.VMEM((tile // sub, sub, d), F32)],
        compiler_params=_cparams(("arbitrary",)),
        name="ffn_half_step",
    )(x, gpre, gpost, wg, wu, wd)


def _proj_kernel(h_ref, g_ref, wn_ref, wt_ref, u_ref, kidx_ref, k_ref, qidxT_ref, qT_ref, vT_ref, wT_ref):
    tile = h_ref.shape[0]
    slab = min(tile, PROJ_SLAB)
    su = u_ref.shape[1]
    nq = qidxT_ref.shape[0]
    na = qT_ref.shape[0]
    for r in range(0, tile, slab):
        rows = slice(r, r + slab)
        hn = _rms(h_ref[rows, :], g_ref[...]).astype(BF16)
        nn = _dot(hn, wn_ref[...])
        u_ref[rows, :] = nn[:, :su].astype(BF16)
        kidx_ref[rows, :] = nn[:, su:su + IDX_DIM].astype(BF16)
        k_ref[rows, :] = nn[:, su + IDX_DIM:su + IDX_DIM + HEAD_DIM].astype(BF16)
        tt = _dot_nt(wt_ref[...], hn)
        qidxT_ref[:, rows] = (tt[:nq] * (IDX_DIM ** -0.5)).astype(BF16)
        qT_ref[:, rows] = (tt[nq:nq + na] * (HEAD_DIM ** -0.5)).astype(BF16)
        ones = jnp.ones((V_ROWS - HEAD_DIM, slab), F32)
        vT_ref[:, rows] = jnp.concatenate([tt[nq + na:nq + na + HEAD_DIM], ones], axis=0).astype(BF16)
        wT_ref[:, rows] = tt[nq + na + HEAD_DIM:nq + na + HEAD_DIM + IDX_HEADS] * (IDX_HEADS ** -0.5)


def _proj(h, gain, w_nn, w_t, nb, seq, tile):
    d = h.shape[1]
    su = w_nn.shape[1] - IDX_DIM - HEAD_DIM
    nq = IDX_HEADS * IDX_DIM
    na = w_t.shape[0] - nq - HEAD_DIM - IDX_HEADS
    nt = seq // tile
    tok = lambda b, i: (b * nt + i, 0)
    tokT = lambda b, i: (0, b * nt + i)
    out_shape = (
        jax.ShapeDtypeStruct((nb * seq, su), BF16),
        jax.ShapeDtypeStruct((nb * seq, IDX_DIM), BF16),
        jax.ShapeDtypeStruct((nb * seq, HEAD_DIM), BF16),
        jax.ShapeDtypeStruct((nq, nb * seq), BF16),
        jax.ShapeDtypeStruct((na, nb * seq), BF16),
        jax.ShapeDtypeStruct((V_ROWS, nb * seq), BF16),
        jax.ShapeDtypeStruct((IDX_HEADS, nb * seq), F32),
    )
    out_specs = (
        pl.BlockSpec((tile, su), tok),
        pl.BlockSpec((tile, IDX_DIM), tok),
        pl.BlockSpec((tile, HEAD_DIM), tok),
        pl.BlockSpec((nq, tile), tokT),
        pl.BlockSpec((na, tile), tokT),
        pl.BlockSpec((V_ROWS, tile), tokT),
        pl.BlockSpec((IDX_HEADS, tile), tokT),
    )
    return pl.pallas_call(
        _proj_kernel,
        out_shape=out_shape,
        grid=(nb, nt),
        in_specs=[
            pl.BlockSpec((tile, d), tok),
            _const_spec((1, d)),
            _const_spec(w_nn.shape),
            _const_spec(w_t.shape),
        ],
        out_specs=out_specs,
        compiler_params=_cparams(("arbitrary", "arbitrary")),
        name="mixer_in_proj",
    )(h, gain, w_nn, w_t)


def _s5_scan(bu_ref, xs_ref, st_ref, are_ref, aim_ref, chunks, n_steps, nb, bcast_rows):
    n_state = are_ref.shape[1]
    for c in chunks:
        lanes = slice(c * S5_LANES, (c + 1) * S5_LANES)
        lanes_im = slice(n_state + c * S5_LANES, n_state + (c + 1) * S5_LANES)
        a_re = jnp.broadcast_to(are_ref[:, lanes], (nb, S5_LANES))
        a_im = jnp.broadcast_to(aim_ref[:, lanes], (nb, S5_LANES))
        x_re = st_ref[0, :, lanes]
        x_im = st_ref[1, :, lanes]
        for t in range(n_steps):
            rows = slice(t, t + 1) if bcast_rows else slice(t * nb, (t + 1) * nb)
            b_re = bu_ref[rows, lanes]
            b_im = bu_ref[rows, lanes_im]
            x_re, x_im = (a_re * x_re - a_im * x_im + b_re,
                          a_re * x_im + a_im * x_re + b_im)
            if xs_ref is not None:
                xs_ref[rows, lanes] = x_re.astype(BF16)
                xs_ref[rows, lanes_im] = x_im.astype(BF16)
        st_ref[0, :, lanes] = x_re
        st_ref[1, :, lanes] = x_im


def _s5_kernel(u_ref, um_ref, bmat_ref, are_ref, aim_ref, cmat_ref, d_ref, wglu_ref, y_ref,
               bu_ref, xs_ref, st_ref, bum_ref):
    nb, n_steps, su = u_ref.shape
    n_state = are_ref.shape[1]
    half_u = su // 2
    half_s = n_state // 2
    chunks_per_half = half_s // S5_LANES

    def b_project(u, dst_ref):
        for hf in range(2):
            uu = u[:, hf * half_u:(hf + 1) * half_u]
            bu = _dot(uu, bmat_ref[hf])
            dst_ref[:, hf * half_s:(hf + 1) * half_s] = bu[:, :half_s]
            dst_ref[:, n_state + hf * half_s:n_state + (hf + 1) * half_s] = bu[:, half_s:]

    @pl.when(pl.program_id(0) == 0)
    def _():
        st_ref[...] = jnp.zeros_like(st_ref)
        b_project(um_ref[...], bum_ref)
        _s5_scan(bum_ref, None, st_ref, are_ref, aim_ref, range(2 * chunks_per_half), um_ref.shape[0], nb, True)

    u = pltpu.einshape("btc->tbc", u_ref[...]).reshape(n_steps * nb, su)
    b_project(u, bu_ref)
    ys = []
    for hf in range(2):
        _s5_scan(bu_ref, xs_ref, st_ref, are_ref, aim_ref,
                 range(hf * chunks_per_half, (hf + 1) * chunks_per_half), n_steps, nb, False)
        y_re = _dot(xs_ref[:, hf * half_s:(hf + 1) * half_s], cmat_ref[hf, 0])
        y_im = _dot(xs_ref[:, n_state + hf * half_s:n_state + (hf + 1) * half_s], cmat_ref[hf, 1])
        ys.append(y_re - y_im)
    y = jnp.concatenate(ys, axis=1) + d_ref[...] * u.astype(F32)
    y = jax.nn.gelu(y)
    y = y * jax.nn.sigmoid(_dot(y.astype(BF16), wglu_ref[...]))
    y_ref[...] = pltpu.einshape("tbc->btc", y.astype(BF16).reshape(n_steps, nb, su))


def _s5(u, u_meta, bmat, a_re, a_im, cmat, d_skip, w_glu):
    nb, seq, su = u.shape
    rows = S5_STEPS * nb
    n_state = a_re.shape[1]
    return pl.pallas_call(
        _s5_kernel,
        out_shape=jax.ShapeDtypeStruct((nb, seq, su), BF16),
        grid=(seq // S5_STEPS,),
        in_specs=[
            pl.BlockSpec((nb, S5_STEPS, su), lambda i: (0, i, 0)),
            _const_spec(u_meta.shape),
            _const_spec(bmat.shape),
            _const_spec(a_re.shape),
            _const_spec(a_im.shape),
            _const_spec(cmat.shape),
            _const_spec(d_skip.shape),
            _const_spec(w_glu.shape),
        ],
        out_specs=pl.BlockSpec((nb, S5_STEPS, su), lambda i: (0, i, 0)),
        scratch_shapes=[
            pltpu.VMEM((rows, 2 * n_state), F32),
            pltpu.VMEM((rows, 2 * n_state), BF16),
            pltpu.VMEM((2, nb, n_state), F32),
            pltpu.VMEM((u_meta.shape[0], 2 * n_state), F32),
        ],
        compiler_params=_cparams(("arbitrary",)),
        name="s5_mixer",
    )(u, u_meta, bmat, a_re, a_im, cmat, d_skip, w_glu)


def _s5_params(lambda_re, lambda_im, log_dt, b_re, b_im, c_re, c_im, d_skip):
    g, p, m = b_re.shape
    lr = lambda_re.astype(F32)
    li = lambda_im.astype(F32)
    dt = jnp.exp(log_dt.astype(F32))[:, None]
    mag = jnp.exp(lr * dt)
    a_re = mag * jnp.cos(li * dt)
    a_im = mag * jnp.sin(li * dt)
    den = lr * lr + li * li
    num_re = a_re - 1.0
    num_im = a_im
    coef_re = (num_re * lr + num_im * li) / den
    coef_im = (num_im * lr - num_re * li) / den
    br = b_re.astype(F32)
    bi = b_im.astype(F32)
    bbar_re = coef_re[..., None] * br - coef_im[..., None] * bi
    bbar_im = coef_re[..., None] * bi + coef_im[..., None] * br
    gh = g // 2
    eye = jnp.eye(gh, dtype=F32)

    def in_mat(bb):
        bb = bb.reshape(2, gh, p, m)
        return jnp.einsum('hgpm,gk->hgmkp', bb, eye).reshape(2, gh * m, gh * p)

    def out_mat(cc):
        cc = cc.astype(F32).reshape(2, gh, m, p)
        return jnp.einsum('hgmp,gk->hgpkm', cc, eye).reshape(2, gh * p, gh * m)

    bmat = jnp.concatenate([in_mat(bbar_re), in_mat(bbar_im)], axis=2).astype(BF16)
    cmat = jnp.stack([out_mat(c_re), out_mat(c_im)], axis=1).astype(BF16)
    return (bmat, a_re.reshape(1, g * p), a_im.reshape(1, g * p), cmat,
            d_skip.astype(F32).reshape(1, g * m))


def _colsum(x):
    return jnp.sum(x.reshape(x.shape[0] // 8, 8, x.shape[1]), axis=0)


def _colmax(x):
    return jnp.max(x.reshape(x.shape[0] // 8, 8, x.shape[1]), axis=0)


def _colmin(x):
    return jnp.min(x.reshape(x.shape[0] // 8, 8, x.shape[1]), axis=0)


def _dsa_kernel(top_k, qidxT_ref, qT_ref, wT_ref, kidx_ref, k_ref, vT_ref, kidxm_ref, km_ref, vTm_ref,
                bias_ref, biasm_ref, tri_ref, trim_ref, y_ref,
                sc_ref, scm_ref, scb_ref, scmb_ref, m_ref, acc_ref, out_ref,
                lg_ref, p_ref, lg2_ref, p2_ref):
    i = pl.program_id(1)
    tq = qT_ref.shape[1]
    tk = tq
    nkb = i + 1
    n_heads = qT_ref.shape[0] // HEAD_DIM
    kf = float(top_k)

    def kblock(kb):
        return pl.ds(pl.multiple_of(kb * tk, tk), tk)

    def head_rows(h):
        return slice(h * HEAD_DIM, (h + 1) * HEAD_DIM)

    kc = lax.broadcasted_iota(jnp.int32, (tk, tq), 0) // CHUNK
    qc = lax.broadcasted_iota(jnp.int32, (tk, tq), 1) // CHUNK
    adm_diag = kc <= qc

    def idx_score(kidx):
        s = jnp.zeros((kidx.shape[0], tq), F32)
        for h in range(IDX_HEADS):
            sh = _dot(kidx, qidxT_ref[h * IDX_DIM:(h + 1) * IDX_DIM, :])
            s = s + jnp.maximum(sh, 0.0) * wT_ref[h:h + 1, :]
        return s

    sm = idx_score(kidxm_ref[...])
    scm_ref[...] = sm
    scmb_ref[...] = sm.astype(BF16)

    def score_body(kb, carry):
        mn, mx = carry
        s = idx_score(kidx_ref[kblock(kb), :])
        adm = jnp.logical_or(kb < i, adm_diag)
        s_adm = jnp.where(adm, s, NEG)
        sc_ref[kblock(kb), :] = s_adm
        scb_ref[kblock(kb), :] = s_adm.astype(BF16)
        mn = jnp.minimum(mn, _colmin(jnp.where(adm, s, -NEG)))
        mx = jnp.maximum(mx, _colmax(jnp.where(adm, s, NEG)))
        return mn, mx

    carry = lax.fori_loop(0, nkb // 2, lambda j, c: score_body(2 * j + 1, score_body(2 * j, c)),
                          (_colmin(sm), _colmax(sm)))
    mn, mx = lax.cond(nkb % 2 == 1, lambda c: score_body(nkb - 1, c), lambda c: c, carry)
    mn = jnp.min(mn, axis=0, keepdims=True)
    mx = jnp.max(mx, axis=0, keepdims=True)

    def reduce_keys(fn, combine, finish):
        def body(kb, acc):
            base = pl.multiple_of(kb * tk, tk)
            for r in range(0, tk, SCAN_ROWS):
                acc = combine(acc, fn(sc_ref[pl.ds(base + r, SCAN_ROWS), :]))
            return acc

        return finish(lax.fori_loop(0, nkb, body, fn(scm_ref[...])))

    def count_gt(t):
        return reduce_keys(lambda s: _colsum(jnp.where(s > t, 1.0, 0.0)), jnp.add,
                           lambda a: jnp.sum(a, axis=0, keepdims=True))

    n_adm = (N_META + i * tk + (lax.broadcasted_iota(jnp.int32, (1, tq), 1) // CHUNK + 1) * CHUNK).astype(F32)
    c_pos, c_nonneg = reduce_keys(
        lambda s: (_colsum(jnp.where(s > 0.0, 1.0, 0.0)), _colsum(jnp.where(s >= 0.0, 1.0, 0.0))),
        lambda a, b: (a[0] + b[0], a[1] + b[1]),
        lambda a: (jnp.sum(a[0], axis=0, keepdims=True), jnp.sum(a[1], axis=0, keepdims=True)))
    zero_vk = jnp.logical_and(c_pos < kf, c_nonneg >= kf)
    pos = c_pos >= kf
    lo0 = jnp.where(pos, 0.0, mn - (1.0 + jnp.abs(mn)))
    hi0 = jnp.where(pos, mx, 0.0)
    clo0 = jnp.where(pos, c_pos, n_adm)

    def open_rows(clo):
        return jnp.max(jnp.where(zero_vk, 0.0, clo)) > kf

    def search_cond(carry):
        it, _, _, _, go = carry
        return jnp.logical_and(it < BISECT_STEPS, go)

    def halve(carry):
        lo, hi, clo = carry
        mid = 0.5 * lo + 0.5 * hi
        c = count_gt(mid)
        up = c >= kf
        return jnp.where(up, mid, lo), jnp.where(up, hi, mid), jnp.where(up, c, clo)

    def search_body(carry):
        it, lo, hi, clo, _ = carry
        lo, hi, clo = halve(halve((lo, hi, clo)))
        return it + 2, lo, hi, clo, open_rows(clo)

    one_b = jnp.ones((), BF16)
    zero_b = jnp.zeros((), BF16)

    def count_gt_rounded(t):
        t = t.astype(BF16)

        def add_rows(accs, sb):
            for j in range(sb.shape[0] // 16):
                accs[j % len(accs)] = accs[j % len(accs)] + jnp.where(sb[16 * j:16 * (j + 1)] > t, one_b, zero_b)
            return accs

        def body(kb, accs):
            return tuple(add_rows(list(accs), scb_ref[kblock(kb), :]))

        first = jnp.where(scmb_ref[...] > t, one_b, zero_b)
        accs = lax.fori_loop(0, nkb, body, (first,) + (jnp.zeros_like(first),) * 3)
        total = (accs[0] + accs[1]) + (accs[2] + accs[3])
        return jnp.sum(total.astype(F32), axis=0, keepdims=True)

    def halve_rounded(_, carry):
        lo, hi = carry
        mid = (0.5 * lo + 0.5 * hi).astype(BF16).astype(F32)
        up = count_gt_rounded(mid) >= kf
        return jnp.where(up, mid, lo), jnp.where(up, hi, mid)

    lo, hi = lax.fori_loop(0, COARSE_STEPS, halve_rounded,
                           (lo0.astype(BF16).astype(F32), hi0.astype(BF16).astype(F32)))
    hi = hi + jnp.abs(hi) * 2.0 ** -7 + 1e-30
    clo = count_gt(lo)

    lo, hi, clo = lax.fori_loop(0, SEARCH_WARMUP, lambda _, c: halve(c), (lo, hi, clo))
    _, lo, _, _, _ = lax.while_loop(search_cond, search_body, (SEARCH_WARMUP, lo, hi, clo, open_rows(clo)))

    thr = reduce_keys(lambda s: _colmin(jnp.where(s > lo, s, -NEG)), jnp.minimum,
                      lambda a: jnp.min(a, axis=0, keepdims=True))
    thr = jnp.where(zero_vk, 0.0, thr)
    need = kf - count_gt(thr)

    def sel_mask(s, tri, carry):
        eq = jnp.where(s == thr, 1.0, 0.0)
        rank = carry + _dot(tri, eq.astype(BF16))
        sel = jnp.logical_or(s > thr, jnp.logical_and(s == thr, rank < need))
        return (jnp.where(sel, 0.0, NEG).astype(BF16),
                carry + jnp.sum(_colsum(eq), axis=0, keepdims=True))

    v_rows = vT_ref.shape[0]

    def acc_rows(h):
        return slice(h * v_rows, (h + 1) * v_rows)

    def colmax16(t):
        t = jnp.max(t.reshape(t.shape[0] // 16, 16, t.shape[1]), axis=0)
        return jnp.max(t, axis=0, keepdims=True).astype(F32)

    mskm, carry0 = sel_mask(scm_ref[...], trim_ref[...], jnp.zeros((1, tq), F32))
    km = km_ref[...]
    vTm = vTm_ref[...]
    lgs = [_dot(km, qT_ref[head_rows(h), :]).astype(BF16) for h in range(n_heads)]
    ps = []
    for h in range(n_heads):
        t = lgs[h] + biasm_ref[h] + mskm
        m = colmax16(t)
        m_ref[h:h + 1, :] = m
        ps.append(jnp.exp(t - m.astype(BF16)))
    for h in range(n_heads):
        acc_ref[acc_rows(h), :] = _dot(vTm, ps[h])

    def att_body(kb, carry, lg_ref=lg_ref, p_ref=p_ref):
        msk, carry = sel_mask(sc_ref[kblock(kb), :], tri_ref[...], carry)
        which = jnp.clip(kb - i + 2, 0, 2)
        kblk = k_ref[kblock(kb), :]
        vblk = vT_ref[:, kblock(kb)]
        for h in range(n_heads):
            lg_ref[h] = _dot(kblk, qT_ref[head_rows(h), :]).astype(BF16)
        alphas = []
        for h in range(n_heads):
            t = lg_ref[h] + bias_ref[h, which] + msk
            m_old = m_ref[h:h + 1, :]
            m_new = jnp.maximum(m_old, colmax16(t))
            m_ref[h:h + 1, :] = m_new
            p_ref[h] = jnp.exp(t - m_new.astype(BF16))
            alphas.append(jnp.exp(m_old - m_new))
        for h in range(n_heads):
            acc_ref[acc_rows(h), :] = alphas[h] * acc_ref[acc_rows(h), :] + _dot(vblk, p_ref[h])
        return carry

    carry1 = lax.fori_loop(0, nkb // 2,
                           lambda j, c: att_body(2 * j + 1, att_body(2 * j, c), lg2_ref, p2_ref), carry0)
    lax.cond(nkb % 2 == 1, lambda c: att_body(nkb - 1, c), lambda c: c, carry1)

    for h in range(n_heads):
        a = acc_ref[acc_rows(h), :]
        out_ref[head_rows(h), :] = a[:HEAD_DIM] / a[HEAD_DIM:HEAD_DIM + 1]
    y_ref[...] = out_ref[...].T.astype(BF16)


def _rel_bucket_np(rel):
    half = REL_BUCKETS // 2
    max_exact = half // 2
    base = np.where(rel > 0, half, 0)
    n = np.abs(rel)
    nf = np.maximum(n, 1).astype(np.float64)
    large = max_exact + (np.log(nf / max_exact) / math.log(REL_MAX_DIST / max_exact)
                         * (half - max_exact)).astype(np.int32)
    large = np.minimum(large, half - 1)
    return base + np.where(n < max_exact, n, large)


def _bias_of_rel(rb, rel, rel_lo, rel_hi):
    rels = np.arange(rel_lo, rel_hi + 1)
    buckets = _rel_bucket_np(rels)
    col = lambda b: rb[int(b)].reshape((-1,) + (1,) * rel.ndim)
    val = jnp.broadcast_to(col(buckets[0]), (rb.shape[1],) + rel.shape)
    for j in range(1, len(rels)):
        if buckets[j] != buckets[j - 1]:
            val = jnp.where(rel[None] >= int(rels[j]), col(buckets[j]), val)
    return val


def _bias_tables(rel_bias, tq, seq):
    rb = rel_bias.astype(F32)
    far = -2 * tq
    assert np.all(_rel_bucket_np(np.arange(-(seq + N_META), -tq)) == _rel_bucket_np(np.array(far)))
    d = lax.broadcasted_iota(jnp.int32, (tq, tq), 0) - lax.broadcasted_iota(jnp.int32, (tq, tq), 1)
    rel = jnp.stack([jnp.full((tq, tq), far, jnp.int32), d - tq, d])
    bias = _bias_of_rel(rb, rel, far, tq - 1)
    rel_m = (lax.broadcasted_iota(jnp.int32, (N_META, seq), 0)
             - lax.broadcasted_iota(jnp.int32, (N_META, seq), 1) - N_META)
    bias_m = _bias_of_rel(rb, rel_m, -(seq + N_META), -1)
    return bias.astype(BF16), bias_m.astype(BF16)


def _dsa(qidxT, qT, wT, kidx, k, vT, kidx_m, k_m, vT_m, bias, bias_m, nb, seq):
    tq = DSA_TILE
    nq = seq // tq
    n_att = qT.shape[0]
    n_heads = n_att // HEAD_DIM
    tri = jnp.asarray(np.tril(np.ones((tq, tq), np.float32), -1), BF16)
    tri_m = jnp.asarray(np.tril(np.ones((N_META, N_META), np.float32), -1), BF16)
    qcol = lambda b, i: (0, b * nq + i)
    return pl.pallas_call(
        functools.partial(_dsa_kernel, min(TOPK_MAX, seq // 4)),
        out_shape=jax.ShapeDtypeStruct((nb * seq, n_att), BF16),
        grid=(nb, nq),
        in_specs=[
            pl.BlockSpec((qidxT.shape[0], tq), qcol),
            pl.BlockSpec((n_att, tq), qcol),
            pl.BlockSpec((IDX_HEADS, tq), qcol),
            pl.BlockSpec((seq, IDX_DIM), lambda b, i: (b, 0)),
            pl.BlockSpec((seq, HEAD_DIM), lambda b, i: (b, 0)),
            pl.BlockSpec((V_ROWS, seq), lambda b, i: (0, b)),
            _const_spec(kidx_m.shape),
            _const_spec(k_m.shape),
            _const_spec(vT_m.shape),
            _const_spec(bias.shape),
            pl.BlockSpec((bias_m.shape[0], N_META, tq), lambda b, i: (0, 0, i)),
            _const_spec(tri.shape),
            _const_spec(tri_m.shape),
        ],
        out_specs=pl.BlockSpec((tq, n_att), lambda b, i: (b * nq + i, 0)),
        scratch_shapes=[
            pltpu.VMEM((seq, tq), F32),
            pltpu.VMEM((N_META, tq), F32),
            pltpu.VMEM((seq, tq), BF16),
            pltpu.VMEM((N_META, tq), BF16),
            pltpu.VMEM((n_heads, tq), F32),
            pltpu.VMEM((n_heads * V_ROWS, tq), F32),
            pltpu.VMEM((n_att, tq), F32),
            pltpu.VMEM((n_heads, tq, tq), BF16),
            pltpu.VMEM((n_heads, tq, tq), BF16),
            pltpu.VMEM((n_heads, tq, tq), BF16),
            pltpu.VMEM((n_heads, tq, tq), BF16),
        ],
        compiler_params=_cparams(("arbitrary", "arbitrary")),
        name="dsa_mixer",
    )(qidxT, qT, wT, kidx, k, vT, kidx_m, k_m, vT_m, bias, bias_m, tri, tri_m)


def _merge_kernel(h_ref, ya_ref, yb_ref, gpre_ref, gpost_ref, wg_ref, wa_ref, wb_ref, wo_ref, o_ref):
    tile, d = h_ref.shape
    slabs = [slice(r, r + MERGE_SLAB) for r in range(0, tile, MERGE_SLAB)]
    merged = []
    for rows in slabs:
        hn = _rms(h_ref[rows, :], gpre_ref[...]).astype(BF16)
        pa = _dot(ya_ref[rows, :], wa_ref[...])
        pb = _dot(yb_ref[rows, :], wb_ref[...])
        g0 = jax.nn.sigmoid(_dot(hn, wg_ref[:, :d]))
        g1 = jax.nn.sigmoid(_dot(hn, wg_ref[:, d:]))
        merged.append((g0 * pa + g1 * pb).astype(BF16))
    for rows, m in zip(slabs, merged):
        o_ref[rows, :] = h_ref[rows, :] + _rms(_dot(m, wo_ref[...]), gpost_ref[...])


def _merge(h, ya, yb, gpre, gpost, wg, wa, wb, wo, nb, seq, tile):
    d = h.shape[1]
    sa = wa.shape[0]
    sb = wb.shape[0]
    nt = seq // tile
    tok = lambda b, i: (b * nt + i, 0)
    return pl.pallas_call(
        _merge_kernel,
        out_shape=jax.ShapeDtypeStruct(h.shape, F32),
        grid=(nb, nt),
        in_specs=[
            pl.BlockSpec((tile, d), tok),
            pl.BlockSpec((tile, sa), tok),
            pl.BlockSpec((tile, sb), tok),
            _const_spec((1, d)),
            _const_spec((1, d)),
            _const_spec(wg.shape),
            _const_spec(wa.shape),
            _const_spec(wb.shape),
            _const_spec(wo.shape),
        ],
        out_specs=pl.BlockSpec((tile, d), tok),
        compiler_params=_cparams(("arbitrary", "arbitrary")),
        name="mixer_merge",
    )(h, ya, yb, gpre, gpost, wg, wa, wb, wo)


def kernel(x, meta_tokens, ff1_norm_pre, ff1_norm_post, mix_norm_pre, mix_norm_post, ff2_norm_pre, ff2_norm_post, ff1_w_gate, ff1_w_up, ff1_w_down, ff2_w_gate, ff2_w_up, ff2_w_down, w_in, ssm_lambda_re, ssm_lambda_im, ssm_log_dt, ssm_b_re, ssm_b_im, ssm_c_re, ssm_c_im, ssm_d, ssm_w_glu, w_branch_a, rel_bias, w_branch_b, w_out):
    nb, seq, d = x.shape
    depth = w_in.shape[0]
    assert depth == 1, "meta-token rows are only carried as keys/state of a single layer"
    assert meta_tokens.shape[0] == N_META
    su = ssm_w_glu.shape[1]
    n_att = w_branch_b.shape[1]
    row = lambda v: v.astype(F32).reshape(1, -1)
    bf = lambda w: w.astype(BF16)

    o_u, o_qi = 0, su
    o_ki = o_qi + IDX_HEADS * IDX_DIM
    o_wi = o_ki + IDX_DIM
    o_q = o_wi + IDX_HEADS
    o_k = o_q + n_att
    o_v = o_k + HEAD_DIM
    o_g = o_v + HEAD_DIM
    win = w_in[0]
    w_nn = bf(jnp.concatenate([win[:, o_u:o_qi], win[:, o_ki:o_wi], win[:, o_k:o_v]], axis=1))
    w_t = bf(jnp.concatenate([win[:, o_qi:o_ki], win[:, o_q:o_k], win[:, o_v:o_g], win[:, o_wi:o_q]], axis=1).T)
    w_gates = bf(win[:, o_g:])

    hx = x.reshape(nb * seq, d)
    pad = META_PAD
    hm = jnp.zeros((pad, d), x.dtype).at[:N_META].set(meta_tokens.astype(x.dtype))

    ffn1 = (row(ff1_norm_pre[0]), row(ff1_norm_post[0]), bf(ff1_w_gate[0]), bf(ff1_w_up[0]), bf(ff1_w_down[0]))
    ffn2 = (row(ff2_norm_pre[0]), row(ff2_norm_post[0]), bf(ff2_w_gate[0]), bf(ff2_w_up[0]), bf(ff2_w_down[0]))

    h1 = _ffn(hx, *ffn1, FFN_TILE)
    h1m = _ffn(hm, *ffn1, pad)

    g_mix = row(mix_norm_pre[0])
    u, kidx, k, qidxT, qT, vT, wT = _proj(h1, g_mix, w_nn, w_t, nb, seq, PROJ_TILE)
    u_m, kidx_m, k_m, _, _, vT_m, _ = _proj(h1m, g_mix, w_nn, w_t, 1, pad, pad)

    bmat, a_re, a_im, cmat, d_skip = _s5_params(
        ssm_lambda_re[0], ssm_lambda_im[0], ssm_log_dt[0], ssm_b_re[0], ssm_b_im[0],
        ssm_c_re[0], ssm_c_im[0], ssm_d[0])
    ya = _s5(u.reshape(nb, seq, su), u_m[:N_META], bmat, a_re, a_im, cmat, d_skip, bf(ssm_w_glu[0]))
    ya = ya.reshape(nb * seq, su)

    bias, bias_m = _bias_tables(rel_bias, DSA_TILE, seq)
    yb = _dsa(qidxT, qT, wT, kidx, k, vT, kidx_m[:N_META], k_m[:N_META], vT_m[:, :N_META],
              bias, bias_m, nb, seq)

    h2 = _merge(h1, ya, yb, g_mix, row(mix_norm_post[0]), w_gates, bf(w_branch_a[0]),
                bf(w_branch_b[0]), bf(w_out[0]), nb, seq, MERGE_TILE)

    out = _ffn(h2, *ffn2, FFN_TILE)
    return out.reshape(nb, seq, d)
```

```python
import functools
import math

import numpy as np
import jax
import jax.numpy as jnp
from jax import lax
from jax.experimental import pallas as pl
from jax.experimental.pallas import tpu as pltpu

F32 = jnp.float32
BF16 = jnp.bfloat16

RMS_EPS = 1e-6
CHUNK = 64
N_META = 16
HEAD_DIM = 64
V_ROWS = HEAD_DIM + 16
IDX_DIM = 64
IDX_HEADS = 8
TOPK_MAX = 256
REL_BUCKETS = 32
REL_MAX_DIST = 128
NEG = -1e30
BISECT_STEPS = 64
COARSE_STEPS = 10
SEARCH_WARMUP = 6

VMEM_LIMIT = 56 * 1024 * 1024

FFN_TILE = 1024
FFN_SLAB = 512
FFN_CHUNK = 256
PROJ_TILE = 1024
PROJ_SLAB = 512
META_PAD = 128
S5_STEPS = 32
S5_LANES = 512
DSA_TILE = 256
SCAN_ROWS = 64
MERGE_TILE = 1024
MERGE_SLAB = 256


def _cparams(sem):
    return pltpu.CompilerParams(dimension_semantics=sem, vmem_limit_bytes=VMEM_LIMIT)


def _const_spec(shape):
    nd = len(shape)
    return pl.BlockSpec(shape, lambda *_: (0,) * nd)


def _resident_spec(shape):
    nd = len(shape)
    return pl.BlockSpec(shape, lambda *_: (0,) * nd, pipeline_mode=pl.Buffered(1))


def _rms(x, g):
    return x * lax.rsqrt(jnp.mean(x * x, axis=-1, keepdims=True) + RMS_EPS) * g


def _dot(a, b):
    return jnp.dot(a, b, preferred_element_type=F32)


def _dot_nt(a, b):
    return lax.dot_general(a, b, (((1,), (1,)), ((), ())), preferred_element_type=F32)


def _ffn_kernel(x_ref, gpre_ref, gpost_ref, wg_ref, wu_ref, wd_ref, o_ref, acc_ref):
    n_sub, sub, _ = acc_ref.shape
    n_chunks = wg_ref.shape[1] // FFN_CHUNK
    for s in range(n_sub):
        rows = slice(s * sub, (s + 1) * sub)
        x = x_ref[rows, :]
        hn = _rms(x, gpre_ref[...]).astype(BF16)
        for c in range(n_chunks):
            cols = slice(c * FFN_CHUNK, (c + 1) * FFN_CHUNK)
            g = _dot(hn, wg_ref[:, cols])
            u = _dot(hn, wu_ref[:, cols])
            a = (jax.nn.silu(g) * u).astype(BF16)
            part = _dot(a, wd_ref[cols, :])
            if c == 0:
                acc_ref[s] = part
            else:
                acc_ref[s] += part
        o_ref[rows, :] = x + 0.5 * _rms(acc_ref[s], gpost_ref[...])


def _ffn(x, gpre, gpost, wg, wu, wd, tile):
    t, d = x.shape
    dff = wg.shape[1]
    sub = min(tile, FFN_SLAB)
    return pl.pallas_call(
        _ffn_kernel,
        out_shape=jax.ShapeDtypeStruct((t, d), F32),
        grid=(t // tile,),
        in_specs=[
            pl.BlockSpec((tile, d), lambda i: (i, 0)),
            _const_spec((1, d)),
            _const_spec((1, d)),
            _resident_spec((d, dff)),
            _resident_spec((d, dff)),
            _resident_spec((dff, d)),
        ],
        out_specs=pl.BlockSpec((tile, d), lambda i: (i, 0)),
        scratch_shapes=[pltpu.VMEM((tile // sub, sub, d), F32)],
        compiler_params=_cparams(("arbitrary",)),
        name="ffn_half_step",
    )(x, gpre, gpost, wg, wu, wd)


def _proj_kernel(h_ref, g_ref, wn_ref, wt_ref, u_ref, kidx_ref, k_ref, qidxT_ref, qT_ref, vT_ref, wT_ref):
    tile = h_ref.shape[0]
    slab = min(tile, PROJ_SLAB)
    su = u_ref.shape[1]
    nq = qidxT_ref.shape[0]
    na = qT_ref.shape[0]
    for r in range(0, tile, slab):
        rows = slice(r, r + slab)
        hn = _rms(h_ref[rows, :], g_ref[...]).astype(BF16)
        nn = _dot(hn, wn_ref[...])
        u_ref[rows, :] = nn[:, :su].astype(BF16)
        kidx_ref[rows, :] = nn[:, su:su + IDX_DIM].astype(BF16)
        k_ref[rows, :] = nn[:, su + IDX_DIM:su + IDX_DIM + HEAD_DIM].astype(BF16)
        tt = _dot_nt(wt_ref[...], hn)
        qidxT_ref[:, rows] = (tt[:nq] * (IDX_DIM ** -0.5)).astype(BF16)
        qT_ref[:, rows] = (tt[nq:nq + na] * (HEAD_DIM ** -0.5)).astype(BF16)
        ones = jnp.ones((V_ROWS - HEAD_DIM, slab), F32)
        vT_ref[:, rows] = jnp.concatenate([tt[nq + na:nq + na + HEAD_DIM], ones], axis=0).astype(BF16)
        wT_ref[:, rows] = tt[nq + na + HEAD_DIM:nq + na + HEAD_DIM + IDX_HEADS] * (IDX_HEADS ** -0.5)


def _proj(h, gain, w_nn, w_t, nb, seq, tile):
    d = h.shape[1]
    su = w_nn.shape[1] - IDX_DIM - HEAD_DIM
    nq = IDX_HEADS * IDX_DIM
    na = w_t.shape[0] - nq - HEAD_DIM - IDX_HEADS
    nt = seq // tile
    tok = lambda b, i: (b * nt + i, 0)
    tokT = lambda b, i: (0, b * nt + i)
    out_shape = (
        jax.ShapeDtypeStruct((nb * seq, su), BF16),
        jax.ShapeDtypeStruct((nb * seq, IDX_DIM), BF16),
        jax.ShapeDtypeStruct((nb * seq, HEAD_DIM), BF16),
        jax.ShapeDtypeStruct((nq, nb * seq), BF16),
        jax.ShapeDtypeStruct((na, nb * seq), BF16),
        jax.ShapeDtypeStruct((V_ROWS, nb * seq), BF16),
        jax.ShapeDtypeStruct((IDX_HEADS, nb * seq), F32),
    )
    out_specs = (
        pl.BlockSpec((tile, su), tok),
        pl.BlockSpec((tile, IDX_DIM), tok),
        pl.BlockSpec((tile, HEAD_DIM), tok),
        pl.BlockSpec((nq, tile), tokT),
        pl.BlockSpec((na, tile), tokT),
        pl.BlockSpec((V_ROWS, tile), tokT),
        pl.BlockSpec((IDX_HEADS, tile), tokT),
    )
    return pl.pallas_call(
        _proj_kernel,
        out_shape=out_shape,
        grid=(nb, nt),
        in_specs=[
            pl.BlockSpec((tile, d), tok),
            _const_spec((1, d)),
            _const_spec(w_nn.shape),
            _const_spec(w_t.shape),
        ],
        out_specs=out_specs,
        compiler_params=_cparams(("arbitrary", "arbitrary")),
        name="mixer_in_proj",
    )(h, gain, w_nn, w_t)


def _s5_scan(bu_ref, xs_ref, st_ref, are_ref, aim_ref, chunks, n_steps, nb, bcast_rows):
    n_state = are_ref.shape[1]
    for c in chunks:
        lanes = slice(c * S5_LANES, (c + 1) * S5_LANES)
        lanes_im = slice(n_state + c * S5_LANES, n_state + (c + 1) * S5_LANES)
        a_re = jnp.broadcast_to(are_ref[:, lanes], (nb, S5_LANES))
        a_im = jnp.broadcast_to(aim_ref[:, lanes], (nb, S5_LANES))
        x_re = st_ref[0, :, lanes]
        x_im = st_ref[1, :, lanes]
        for t in range(n_steps):
            rows = slice(t, t + 1) if bcast_rows else slice(t * nb, (t + 1) * nb)
            b_re = bu_ref[rows, lanes]
            b_im = bu_ref[rows, lanes_im]
            x_re, x_im = (a_re * x_re - a_im * x_im + b_re,
                          a_re * x_im + a_im * x_re + b_im)
            if xs_ref is not None:
                xs_ref[rows, lanes] = x_re.astype(BF16)
                xs_ref[rows, lanes_im] = x_im.astype(BF16)
        st_ref[0, :, lanes] = x_re
        st_ref[1, :, lanes] = x_im


def _s5_kernel(u_ref, um_ref, bmat_ref, are_ref, aim_ref, cmat_ref, d_ref, wglu_ref, y_ref,
               bu_ref, xs_ref, st_ref, bum_ref):
    nb, n_steps, su = u_ref.shape
    n_state = are_ref.shape[1]
    half_u = su // 2
    half_s = n_state // 2
    chunks_per_half = half_s // S5_LANES

    def b_project(u, dst_ref):
        for hf in range(2):
            uu = u[:, hf * half_u:(hf + 1) * half_u]
            bu = _dot(uu, bmat_ref[hf])
            dst_ref[:, hf * half_s:(hf + 1) * half_s] = bu[:, :half_s]
            dst_ref[:, n_state + hf * half_s:n_state + (hf + 1) * half_s] = bu[:, half_s:]

    @pl.when(pl.program_id(0) == 0)
    def _():
        st_ref[...] = jnp.zeros_like(st_ref)
        b_project(um_ref[...], bum_ref)
        _s5_scan(bum_ref, None, st_ref, are_ref, aim_ref, range(2 * chunks_per_half), um_ref.shape[0], nb, True)

    u = pltpu.einshape("btc->tbc", u_ref[...]).reshape(n_steps * nb, su)
    b_project(u, bu_ref)
    ys = []
    for hf in range(2):
        _s5_scan(bu_ref, xs_ref, st_ref, are_ref, aim_ref,
                 range(hf * chunks_per_half, (hf + 1) * chunks_per_half), n_steps, nb, False)
        y_re = _dot(xs_ref[:, hf * half_s:(hf + 1) * half_s], cmat_ref[hf, 0])
        y_im = _dot(xs_ref[:, n_state + hf * half_s:n_state + (hf + 1) * half_s], cmat_ref[hf, 1])
        ys.append(y_re - y_im)
    y = jnp.concatenate(ys, axis=1) + d_ref[...] * u.astype(F32)
    y = jax.nn.gelu(y)
    y = y * jax.nn.sigmoid(_dot(y.astype(BF16), wglu_ref[...]))
    y_ref[...] = pltpu.einshape("tbc->btc", y.astype(BF16).reshape(n_steps, nb, su))


def _s5(u, u_meta, bmat, a_re, a_im, cmat, d_skip, w_glu):
    nb, seq, su = u.shape
    rows = S5_STEPS * nb
    n_state = a_re.shape[1]
    return pl.pallas_call(
        _s5_kernel,
        out_shape=jax.ShapeDtypeStruct((nb, seq, su), BF16),
        grid=(seq // S5_STEPS,),
        in_specs=[
            pl.BlockSpec((nb, S5_STEPS, su), lambda i: (0, i, 0)),
            _const_spec(u_meta.shape),
            _const_spec(bmat.shape),
            _const_spec(a_re.shape),
            _const_spec(a_im.shape),
            _const_spec(cmat.shape),
            _const_spec(d_skip.shape),
            _const_spec(w_glu.shape),
        ],
        out_specs=pl.BlockSpec((nb, S5_STEPS, su), lambda i: (0, i, 0)),
        scratch_shapes=[
            pltpu.VMEM((rows, 2 * n_state), F32),
            pltpu.VMEM((rows, 2 * n_state), BF16),
            pltpu.VMEM((2, nb, n_state), F32),
            pltpu.VMEM((u_meta.shape[0], 2 * n_state), F32),
        ],
        compiler_params=_cparams(("arbitrary",)),
        name="s5_mixer",
    )(u, u_meta, bmat, a_re, a_im, cmat, d_skip, w_glu)


def _s5_params(lambda_re, lambda_im, log_dt, b_re, b_im, c_re, c_im, d_skip):
    g, p, m = b_re.shape
    lr = lambda_re.astype(F32)
    li = lambda_im.astype(F32)
    dt = jnp.exp(log_dt.astype(F32))[:, None]
    mag = jnp.exp(lr * dt)
    a_re = mag * jnp.cos(li * dt)
    a_im = mag * jnp.sin(li * dt)
    den = lr * lr + li * li
    num_re = a_re - 1.0
    num_im = a_im
    coef_re = (num_re * lr + num_im * li) / den
    coef_im = (num_im * lr - num_re * li) / den
    br = b_re.astype(F32)
    bi = b_im.astype(F32)
    bbar_re = coef_re[..., None] * br - coef_im[..., None] * bi
    bbar_im = coef_re[..., None] * bi + coef_im[..., None] * br
    gh = g // 2
    eye = jnp.eye(gh, dtype=F32)

    def in_mat(bb):
        bb = bb.reshape(2, gh, p, m)
        return jnp.einsum('hgpm,gk->hgmkp', bb, eye).reshape(2, gh * m, gh * p)

    def out_mat(cc):
        cc = cc.astype(F32).reshape(2, gh, m, p)
        return jnp.einsum('hgmp,gk->hgpkm', cc, eye).reshape(2, gh * p, gh * m)

    bmat = jnp.concatenate([in_mat(bbar_re), in_mat(bbar_im)], axis=2).astype(BF16)
    cmat = jnp.stack([out_mat(c_re), out_mat(c_im)], axis=1).astype(BF16)
    return (bmat, a_re.reshape(1, g * p), a_im.reshape(1, g * p), cmat,
            d_skip.astype(F32).reshape(1, g * m))


def _colsum(x):
    return jnp.sum(x.reshape(x.shape[0] // 8, 8, x.shape[1]), axis=0)


def _colmax(x):
    return jnp.max(x.reshape(x.shape[0] // 8, 8, x.shape[1]), axis=0)


def _colmin(x):
    return jnp.min(x.reshape(x.shape[0] // 8, 8, x.shape[1]), axis=0)


def _dsa_kernel(top_k, qidxT_ref, qT_ref, wT_ref, kidx_ref, k_ref, vT_ref, kidxm_ref, km_ref, vTm_ref,
                bias_ref, biasm_ref, tri_ref, trim_ref, y_ref,
                sc_ref, scm_ref, scb_ref, scmb_ref, m_ref, acc_ref, out_ref,
                lg_ref, p_ref, lg2_ref, p2_ref):
    i = pl.program_id(1)
    tq = qT_ref.shape[1]
    tk = tq
    nkb = i + 1
    n_heads = qT_ref.shape[0] // HEAD_DIM
    kf = float(top_k)

    def kblock(kb):
        return pl.ds(pl.multiple_of(kb * tk, tk), tk)

    def head_rows(h):
        return slice(h * HEAD_DIM, (h + 1) * HEAD_DIM)

    kc = lax.broadcasted_iota(jnp.int32, (tk, tq), 0) // CHUNK
    qc = lax.broadcasted_iota(jnp.int32, (tk, tq), 1) // CHUNK
    adm_diag = kc <= qc

    def idx_score(kidx):
        s = jnp.zeros((kidx.shape[0], tq), F32)
        for h in range(IDX_HEADS):
            sh = _dot(kidx, qidxT_ref[h * IDX_DIM:(h + 1) * IDX_DIM, :])
            s = s + jnp.maximum(sh, 0.0) * wT_ref[h:h + 1, :]
        return s

    sm = idx_score(kidxm_ref[...])
    scm_ref[...] = sm
    scmb_ref[...] = sm.astype(BF16)

    def score_body(kb, carry):
        mn, mx = carry
        s = idx_score(kidx_ref[kblock(kb), :])
        adm = jnp.logical_or(kb < i, adm_diag)
        s_adm = jnp.where(adm, s, NEG)
        sc_ref[kblock(kb), :] = s_adm
        scb_ref[kblock(kb), :] = s_adm.astype(BF16)
        mn = jnp.minimum(mn, _colmin(jnp.where(adm, s, -NEG)))
        mx = jnp.maximum(mx, _colmax(jnp.where(adm, s, NEG)))
        return mn, mx

    carry = lax.fori_loop(0, nkb // 2, lambda j, c: score_body(2 * j + 1, score_body(2 * j, c)),
                          (_colmin(sm), _colmax(sm)))
    mn, mx = lax.cond(nkb % 2 == 1, lambda c: score_body(nkb - 1, c), lambda c: c, carry)
    mn = jnp.min(mn, axis=0, keepdims=True)
    mx = jnp.max(mx, axis=0, keepdims=True)

    def reduce_keys(fn, combine, finish):
        def body(kb, acc):
            base = pl.multiple_of(kb * tk, tk)
            for r in range(0, tk, SCAN_ROWS):
                acc = combine(acc, fn(sc_ref[pl.ds(base + r, SCAN_ROWS), :]))
            return acc

        return finish(lax.fori_loop(0, nkb, body, fn(scm_ref[...])))

    def count_gt(t):
        return reduce_keys(lambda s: _colsum(jnp.where(s > t, 1.0, 0.0)), jnp.add,
                           lambda a: jnp.sum(a, axis=0, keepdims=True))

    n_adm = (N_META + i * tk + (lax.broadcasted_iota(jnp.int32, (1, tq), 1) // CHUNK + 1) * CHUNK).astype(F32)
    c_pos, c_nonneg = reduce_keys(
        lambda s: (_colsum(jnp.where(s > 0.0, 1.0, 0.0)), _colsum(jnp.where(s >= 0.0, 1.0, 0.0))),
        lambda a, b: (a[0] + b[0], a[1] + b[1]),
        lambda a: (jnp.sum(a[0], axis=0, keepdims=True), jnp.sum(a[1], axis=0, keepdims=True)))
    zero_vk = jnp.logical_and(c_pos < kf, c_nonneg >= kf)
    pos = c_pos >= kf
    lo0 = jnp.where(pos, 0.0, mn - (1.0 + jnp.abs(mn)))
    hi0 = jnp.where(pos, mx, 0.0)
    clo0 = jnp.where(pos, c_pos, n_adm)

    def open_rows(clo):
        return jnp.max(jnp.where(zero_vk, 0.0, clo)) > kf

    def search_cond(carry):
        it, _, _, _, go = carry
        return jnp.logical_and(it < BISECT_STEPS, go)

    def halve(carry):
        lo, hi, clo = carry
        mid = 0.5 * lo + 0.5 * hi
        c = count_gt(mid)
        up = c >= kf
        return jnp.where(up, mid, lo), jnp.where(up, hi, mid), jnp.where(up, c, clo)

    def search_body(carry):
        it, lo, hi, clo, _ = carry
        lo, hi, clo = halve(halve((lo, hi, clo)))
        return it + 2, lo, hi, clo, open_rows(clo)

    one_b = jnp.ones((), BF16)
    zero_b = jnp.zeros((), BF16)

    def count_gt_rounded(t):
        t = t.astype(BF16)

        def add_rows(accs, sb):
            for j in range(sb.shape[0] // 16):
                accs[j % len(accs)] = accs[j % len(accs)] + jnp.where(sb[16 * j:16 * (j + 1)] > t, one_b, zero_b)
            return accs

        def body(kb, accs):
            return tuple(add_rows(list(accs), scb_ref[kblock(kb), :]))

        first = jnp.where(scmb_ref[...] > t, one_b, zero_b)
        accs = lax.fori_loop(0, nkb, body, (first,) + (jnp.zeros_like(first),) * 3)
        total = (accs[0] + accs[1]) + (accs[2] + accs[3])
        return jnp.sum(total.astype(F32), axis=0, keepdims=True)

    def halve_rounded(_, carry):
        lo, hi = carry
        mid = (0.5 * lo + 0.5 * hi).astype(BF16).astype(F32)
        up = count_gt_rounded(mid) >= kf
        return jnp.where(up, mid, lo), jnp.where(up, hi, mid)

    lo, hi = lax.fori_loop(0, COARSE_STEPS, halve_rounded,
                           (lo0.astype(BF16).astype(F32), hi0.astype(BF16).astype(F32)))
    hi = hi + jnp.abs(hi) * 2.0 ** -7 + 1e-30
    clo = count_gt(lo)

    lo, hi, clo = lax.fori_loop(0, SEARCH_WARMUP, lambda _, c: halve(c), (lo, hi, clo))
    _, lo, _, _, _ = lax.while_loop(search_cond, search_body, (SEARCH_WARMUP, lo, hi, clo, open_rows(clo)))

    thr = reduce_keys(lambda s: _colmin(jnp.where(s > lo, s, -NEG)), jnp.minimum,
                      lambda a: jnp.min(a, axis=0, keepdims=True))
    thr = jnp.where(zero_vk, 0.0, thr)
    need = kf - count_gt(thr)

    def sel_mask(s, tri, carry):
        eq = jnp.where(s == thr, 1.0, 0.0)
        rank = carry + _dot(tri, eq.astype(BF16))
        sel = jnp.logical_or(s > thr, jnp.logical_and(s == thr, rank < need))
        return (jnp.where(sel, 0.0, NEG).astype(BF16),
                carry + jnp.sum(_colsum(eq), axis=0, keepdims=True))

    v_rows = vT_ref.shape[0]

    def acc_rows(h):
        return slice(h * v_rows, (h + 1) * v_rows)

    def colmax16(t):
        t = jnp.max(t.reshape(t.shape[0] // 16, 16, t.shape[1]), axis=0)
        return jnp.max(t, axis=0, keepdims=True).astype(F32)

    mskm, carry0 = sel_mask(scm_ref[...], trim_ref[...], jnp.zeros((1, tq), F32))
    km = km_ref[...]
    vTm = vTm_ref[...]
    lgs = [_dot(km, qT_ref[head_rows(h), :]).astype(BF16) for h in range(n_heads)]
    ps = []
    for h in range(n_heads):
        t = lgs[h] + biasm_ref[h] + mskm
        m = colmax16(t)
        m_ref[h:h + 1, :] = m
        ps.append(jnp.exp(t - m.astype(BF16)))
    for h in range(n_heads):
        acc_ref[acc_rows(h), :] = _dot(vTm, ps[h])

    def att_body(kb, carry, lg_ref=lg_ref, p_ref=p_ref):
        msk, carry = sel_mask(sc_ref[kblock(kb), :], tri_ref[...], carry)
        which = jnp.clip(kb - i + 2, 0, 2)
        kblk = k_ref[kblock(kb), :]
        vblk = vT_ref[:, kblock(kb)]
        for h in range(n_heads):
            lg_ref[h] = _dot(kblk, qT_ref[head_rows(h), :]).astype(BF16)
        alphas = []
        for h in range(n_heads):
            t = lg_ref[h] + bias_ref[h, which] + msk
            m_old = m_ref[h:h + 1, :]
            m_new = jnp.maximum(m_old, colmax16(t))
            m_ref[h:h + 1, :] = m_new
            p_ref[h] = jnp.exp(t - m_new.astype(BF16))
            alphas.append(jnp.exp(m_old - m_new))
        for h in range(n_heads):
            acc_ref[acc_rows(h), :] = alphas[h] * acc_ref[acc_rows(h), :] + _dot(vblk, p_ref[h])
        return carry

    carry1 = lax.fori_loop(0, nkb // 2,
                           lambda j, c: att_body(2 * j + 1, att_body(2 * j, c), lg2_ref, p2_ref), carry0)
    lax.cond(nkb % 2 == 1, lambda c: att_body(nkb - 1, c), lambda c: c, carry1)

    for h in range(n_heads):
        a = acc_ref[acc_rows(h), :]
        out_ref[head_rows(h), :] = a[:HEAD_DIM] / a[HEAD_DIM:HEAD_DIM + 1]
    y_ref[...] = out_ref[...].T.astype(BF16)


def _rel_bucket_np(rel):
    half = REL_BUCKETS // 2
    max_exact = half // 2
    base = np.where(rel > 0, half, 0)
    n = np.abs(rel)
    nf = np.maximum(n, 1).astype(np.float64)
    large = max_exact + (np.log(nf / max_exact) / math.log(REL_MAX_DIST / max_exact)
                         * (half - max_exact)).astype(np.int32)
    large = np.minimum(large, half - 1)
    return base + np.where(n < max_exact, n, large)


def _bias_of_rel(rb, rel, rel_lo, rel_hi):
    rels = np.arange(rel_lo, rel_hi + 1)
    buckets = _rel_bucket_np(rels)
    col = lambda b: rb[int(b)].reshape((-1,) + (1,) * rel.ndim)
    val = jnp.broadcast_to(col(buckets[0]), (rb.shape[1],) + rel.shape)
    for j in range(1, len(rels)):
        if buckets[j] != buckets[j - 1]:
            val = jnp.where(rel[None] >= int(rels[j]), col(buckets[j]), val)
    return val


def _bias_tables(rel_bias, tq, seq):
    rb = rel_bias.astype(F32)
    far = -2 * tq
    assert np.all(_rel_bucket_np(np.arange(-(seq + N_META), -tq)) == _rel_bucket_np(np.array(far)))
    d = lax.broadcasted_iota(jnp.int32, (tq, tq), 0) - lax.broadcasted_iota(jnp.int32, (tq, tq), 1)
    rel = jnp.stack([jnp.full((tq, tq), far, jnp.int32), d - tq, d])
    bias = _bias_of_rel(rb, rel, far, tq - 1)
    rel_m = (lax.broadcasted_iota(jnp.int32, (N_META, seq), 0)
             - lax.broadcasted_iota(jnp.int32, (N_META, seq), 1) - N_META)
    bias_m = _bias_of_rel(rb, rel_m, -(seq + N_META), -1)
    return bias.astype(BF16), bias_m.astype(BF16)


def _dsa(qidxT, qT, wT, kidx, k, vT, kidx_m, k_m, vT_m, bias, bias_m, nb, seq):
    tq = DSA_TILE
    nq = seq // tq
    n_att = qT.shape[0]
    n_heads = n_att // HEAD_DIM
    tri = jnp.asarray(np.tril(np.ones((tq, tq), np.float32), -1), BF16)
    tri_m = jnp.asarray(np.tril(np.ones((N_META, N_META), np.float32), -1), BF16)
    qcol = lambda b, i: (0, b * nq + i)
    return pl.pallas_call(
        functools.partial(_dsa_kernel, min(TOPK_MAX, seq // 4)),
        out_shape=jax.ShapeDtypeStruct((nb * seq, n_att), BF16),
        grid=(nb, nq),
        in_specs=[
            pl.BlockSpec((qidxT.shape[0], tq), qcol),
            pl.BlockSpec((n_att, tq), qcol),
            pl.BlockSpec((IDX_HEADS, tq), qcol),
            pl.BlockSpec((seq, IDX_DIM), lambda b, i: (b, 0)),
            pl.BlockSpec((seq, HEAD_DIM), lambda b, i: (b, 0)),
            pl.BlockSpec((V_ROWS, seq), lambda b, i: (0, b)),
            _const_spec(kidx_m.shape),
            _const_spec(k_m.shape),
            _const_spec(vT_m.shape),
            _const_spec(bias.shape),
            pl.BlockSpec((bias_m.shape[0], N_META, tq), lambda b, i: (0, 0, i)),
            _const_spec(tri.shape),
            _const_spec(tri_m.shape),
        ],
        out_specs=pl.BlockSpec((tq, n_att), lambda b, i: (b * nq + i, 0)),
        scratch_shapes=[
            pltpu.VMEM((seq, tq), F32),
            pltpu.VMEM((N_META, tq), F32),
            pltpu.VMEM((seq, tq), BF16),
            pltpu.VMEM((N_META, tq), BF16),
            pltpu.VMEM((n_heads, tq), F32),
            pltpu.VMEM((n_heads * V_ROWS, tq), F32),
            pltpu.VMEM((n_att, tq), F32),
            pltpu.VMEM((n_heads, tq, tq), BF16),
            pltpu.VMEM((n_heads, tq, tq), BF16),
            pltpu.VMEM((n_heads, tq, tq), BF16),
            pltpu.VMEM((n_heads, tq, tq), BF16),
        ],
        compiler_params=_cparams(("arbitrary", "arbitrary")),
        name="dsa_mixer",
    )(qidxT, qT, wT, kidx, k, vT, kidx_m, k_m, vT_m, bias, bias_m, tri, tri_m)


def _merge_kernel(h_ref, ya_ref, yb_ref, gpre_ref, gpost_ref, wg_ref, wa_ref, wb_ref, wo_ref, o_ref):
    tile, d = h_ref.shape
    slabs = [slice(r, r + MERGE_SLAB) for r in range(0, tile, MERGE_SLAB)]
    merged = []
    for rows in slabs:
        hn = _rms(h_ref[rows, :], gpre_ref[...]).astype(BF16)
        pa = _dot(ya_ref[rows, :], wa_ref[...])
        pb = _dot(yb_ref[rows, :], wb_ref[...])
        g0 = jax.nn.sigmoid(_dot(hn, wg_ref[:, :d]))
        g1 = jax.nn.sigmoid(_dot(hn, wg_ref[:, d:]))
        merged.append((g0 * pa + g1 * pb).astype(BF16))
    for rows, m in zip(slabs, merged):
        o_ref[rows, :] = h_ref[rows, :] + _rms(_dot(m, wo_ref[...]), gpost_ref[...])


def _merge(h, ya, yb, gpre, gpost, wg, wa, wb, wo, nb, seq, tile):
    d = h.shape[1]
    sa = wa.shape[0]
    sb = wb.shape[0]
    nt = seq // tile
    tok = lambda b, i: (b * nt + i, 0)
    return pl.pallas_call(
        _merge_kernel,
        out_shape=jax.ShapeDtypeStruct(h.shape, F32),
        grid=(nb, nt),
        in_specs=[
            pl.BlockSpec((tile, d), tok),
            pl.BlockSpec((tile, sa), tok),
            pl.BlockSpec((tile, sb), tok),
            _const_spec((1, d)),
            _const_spec((1, d)),
            _const_spec(wg.shape),
            _const_spec(wa.shape),
            _const_spec(wb.shape),
            _const_spec(wo.shape),
        ],
        out_specs=pl.BlockSpec((tile, d), tok),
        compiler_params=_cparams(("arbitrary", "arbitrary")),
        name="mixer_merge",
    )(h, ya, yb, gpre, gpost, wg, wa, wb, wo)


def kernel(x, meta_tokens, ff1_norm_pre, ff1_norm_post, mix_norm_pre, mix_norm_post, ff2_norm_pre, ff2_norm_post, ff1_w_gate, ff1_w_up, ff1_w_down, ff2_w_gate, ff2_w_up, ff2_w_down, w_in, ssm_lambda_re, ssm_lambda_im, ssm_log_dt, ssm_b_re, ssm_b_im, ssm_c_re, ssm_c_im, ssm_d, ssm_w_glu, w_branch_a, rel_bias, w_branch_b, w_out):
    nb, seq, d = x.shape
    depth = w_in.shape[0]
    assert depth == 1, "meta-token rows are only carried as keys/state of a single layer"
    assert meta_tokens.shape[0] == N_META
    su = ssm_w_glu.shape[1]
    n_att = w_branch_b.shape[1]
    row = lambda v: v.astype(F32).reshape(1, -1)
    bf = lambda w: w.astype(BF16)

    o_u, o_qi = 0, su
    o_ki = o_qi + IDX_HEADS * IDX_DIM
    o_wi = o_ki + IDX_DIM
    o_q = o_wi + IDX_HEADS
    o_k = o_q + n_att
    o_v = o_k + HEAD_DIM
    o_g = o_v + HEAD_DIM
    win = w_in[0]
    w_nn = bf(jnp.concatenate([win[:, o_u:o_qi], win[:, o_ki:o_wi], win[:, o_k:o_v]], axis=1))
    w_t = bf(jnp.concatenate([win[:, o_qi:o_ki], win[:, o_q:o_k], win[:, o_v:o_g], win[:, o_wi:o_q]], axis=1).T)
    w_gates = bf(win[:, o_g:])

    hx = x.reshape(nb * seq, d)
    pad = META_PAD
    hm = jnp.zeros((pad, d), x.dtype).at[:N_META].set(meta_tokens.astype(x.dtype))

    ffn1 = (row(ff1_norm_pre[0]), row(ff1_norm_post[0]), bf(ff1_w_gate[0]), bf(ff1_w_up[0]), bf(ff1_w_down[0]))
    ffn2 = (row(ff2_norm_pre[0]), row(ff2_norm_post[0]), bf(ff2_w_gate[0]), bf(ff2_w_up[0]), bf(ff2_w_down[0]))

    h1 = _ffn(hx, *ffn1, FFN_TILE)
    h1m = _ffn(hm, *ffn1, pad)

    g_mix = row(mix_norm_pre[0])
    u, kidx, k, qidxT, qT, vT, wT = _proj(h1, g_mix, w_nn, w_t, nb, seq, PROJ_TILE)
    u_m, kidx_m, k_m, _, _, vT_m, _ = _proj(h1m, g_mix, w_nn, w_t, 1, pad, pad)

    bmat, a_re, a_im, cmat, d_skip = _s5_params(
        ssm_lambda_re[0], ssm_lambda_im[0], ssm_log_dt[0], ssm_b_re[0], ssm_b_im[0],
        ssm_c_re[0], ssm_c_im[0], ssm_d[0])
    ya = _s5(u.reshape(nb, seq, su), u_m[:N_META], bmat, a_re, a_im, cmat, d_skip, bf(ssm_w_glu[0]))
    ya = ya.reshape(nb * seq, su)

    bias, bias_m = _bias_tables(rel_bias, DSA_TILE, seq)
    yb = _dsa(qidxT, qT, wT, kidx, k, vT, kidx_m[:N_META], k_m[:N_META], vT_m[:, :N_META],
              bias, bias_m, nb, seq)

    h2 = _merge(h1, ya, yb, g_mix, row(mix_norm_post[0]), w_gates, bf(w_branch_a[0]),
                bf(w_branch_b[0]), bf(w_out[0]), nb, seq, MERGE_TILE)

    out = _ffn(h2, *ffn2, FFN_TILE)
    return out.reshape(nb, seq, d)
```
